```python
import jax, jax.numpy as jnp
from jax import lax
import numpy as np

D_MODEL = 1024
BATCH = 4
SEQ = 4096
DEPTH = 1

D_FF = 2816
POOL_WIDTH = D_MODEL // 2
POOL_WINDOWS = (2, 4, 8, 16)
N_POOL_GROUPS = len(POOL_WINDOWS)
POOL_GROUP = POOL_WIDTH // N_POOL_GROUPS
HEAD_DIM = 64
N_HEADS = D_MODEL // HEAD_DIM
N_KV_HEADS = N_HEADS // 8
GQA_GROUP = N_HEADS // N_KV_HEADS
WINDOW = 128
BLOCK = 128
ATTN_WIDTH = N_HEADS * HEAD_DIM
KV_WIDTH = N_KV_HEADS * HEAD_DIM
N_BRANCHES = 2
IN_WIDTH = POOL_WIDTH + ATTN_WIDTH + 2 * KV_WIDTH + N_BRANCHES * D_MODEL
RMS_EPS = 1e-6

kernel_name = "hybrid_pool_swa_sink_macaron_layer"


def rmsnorm(x, g):
    xf = x.astype(jnp.float32)
    y = xf * lax.rsqrt(jnp.mean(xf * xf, axis=-1, keepdims=True) + RMS_EPS)
    return (y * g.astype(jnp.float32)).astype(x.dtype)


def swiglu(x, w_gate, w_up, w_down):
    return (jax.nn.silu(x @ w_gate) * (x @ w_up)) @ w_down


def causal_pool_mixer(xp, pool_w, pool_scale):
    B, S, P = xp.shape
    xf = xp.astype(jnp.float32)
    csum = jnp.concatenate([jnp.zeros((B, 1, P), jnp.float32), jnp.cumsum(xf, axis=1)], axis=1)
    t = jnp.arange(S)
    pooled = []
    for gi, w in enumerate(POOL_WINDOWS):
        cg = csum[..., gi * POOL_GROUP:(gi + 1) * POOL_GROUP]
        start = jnp.maximum(t + 1 - w, 0)
        window_sum = cg[:, 1:, :] - cg[:, start, :]
        count = jnp.minimum(t + 1, w).astype(jnp.float32)
        pooled.append(window_sum / count[None, :, None])
    pooled = (jnp.concatenate(pooled, axis=-1) - xf).astype(xp.dtype)
    pooled = pooled.reshape(B, S, N_POOL_GROUPS, POOL_GROUP)
    mixed = jnp.einsum('bsgc,gcd->bsgd', pooled, pool_w).reshape(B, S, P)
    return mixed * pool_scale


def sliding_window_sink_attention(q, k, v, q_norm, k_norm, sinks):
    B, S = q.shape[0], q.shape[1]
    nb = S // BLOCK
    q = rmsnorm(q, q_norm)
    k = rmsnorm(k, k_norm)
    qb = q.reshape(B, nb, BLOCK, N_KV_HEADS, GQA_GROUP, HEAD_DIM)
    kb = k.reshape(B, nb, BLOCK, N_KV_HEADS, HEAD_DIM)
    vb = v.reshape(B, nb, BLOCK, N_KV_HEADS, HEAD_DIM)
    pad = ((0, 0), (1, 0), (0, 0), (0, 0), (0, 0))
    kk = jnp.concatenate([jnp.pad(kb, pad)[:, :-1], kb], axis=2)
    vv = jnp.concatenate([jnp.pad(vb, pad)[:, :-1], vb], axis=2)
    scores = jnp.einsum('bnqhgd,bnkhd->bnhgqk', qb, kk).astype(jnp.float32) * (HEAD_DIM ** -0.5)
    qi = jnp.arange(BLOCK)[:, None]
    kj = jnp.arange(2 * BLOCK)[None, :] - BLOCK
    rel = qi - kj
    band = (rel >= 0) & (rel < WINDOW)
    in_seq = (jnp.arange(nb)[:, None, None] > 0) | (kj[None] >= 0)
    mask = band[None] & in_seq
    scores = jnp.where(mask[None, :, None, None], scores, jnp.finfo(jnp.float32).min)
    sink = jnp.broadcast_to(sinks.astype(jnp.float32).reshape(1, 1, N_KV_HEADS, GQA_GROUP, 1, 1),
                            scores.shape[:-1] + (1,))
    probs = jax.nn.softmax(jnp.concatenate([scores, sink], axis=-1), axis=-1)[..., :-1]
    out = jnp.einsum('bnhgqk,bnkhd->bnqhgd', probs.astype(vv.dtype), vv)
    return out.reshape(B, S, ATTN_WIDTH)


def setup_inputs(seed: int = 0) -> dict:
    key = jax.random.key(seed)
    ks = jax.random.split(key, 24)
    f32 = jnp.float32

    def nrm(k, shape, fan_in):
        return jax.random.normal(k, shape, f32) * (fan_in ** -0.5)

    def gain(k, shape):
        return jnp.ones(shape, f32) + 0.02 * jax.random.normal(k, shape, f32)

    return {
        "x": jax.random.normal(ks[0], (BATCH, SEQ, D_MODEL), f32),
        "ffn1_norm": gain(ks[1], (D_MODEL,)),
        "ffn1_w_gate": nrm(ks[2], (D_MODEL, D_FF), D_MODEL),
        "ffn1_w_up": nrm(ks[3], (D_MODEL, D_FF), D_MODEL),
        "ffn1_w_down": nrm(ks[4], (D_FF, D_MODEL), D_FF),
        "mix_norm": gain(ks[5], (D_MODEL,)),
        "w_in": nrm(ks[6], (D_MODEL, IN_WIDTH), D_MODEL),
        "pool_w": nrm(ks[7], (N_POOL_GROUPS, POOL_GROUP, POOL_GROUP), POOL_GROUP),
        "pool_scale": gain(ks[8], (POOL_WIDTH,)),
        "w_pool_out": nrm(ks[9], (POOL_WIDTH, D_MODEL), POOL_WIDTH),
        "q_norm": gain(ks[10], (HEAD_DIM,)),
        "k_norm": gain(ks[11], (HEAD_DIM,)),
        "sinks": 0.5 * jax.random.normal(ks[12], (N_HEADS,), f32),
        "w_attn_out": nrm(ks[13], (ATTN_WIDTH, D_MODEL), ATTN_WIDTH),
        "gate_bias": 0.01 * jax.random.normal(ks[14], (N_BRANCHES * D_MODEL,), f32),
        "w_out": nrm(ks[15], (D_MODEL, D_MODEL), D_MODEL),
        "ffn2_norm": gain(ks[16], (D_MODEL,)),
        "ffn2_w_gate": nrm(ks[17], (D_MODEL, D_FF), D_MODEL),
        "ffn2_w_up": nrm(ks[18], (D_MODEL, D_FF), D_MODEL),
        "ffn2_w_down": nrm(ks[19], (D_FF, D_MODEL), D_FF),
    }


def reference(x, ffn1_norm, ffn1_w_gate, ffn1_w_up, ffn1_w_down, mix_norm, w_in, pool_w,
              pool_scale, w_pool_out, q_norm, k_norm, sinks, w_attn_out, gate_bias, w_out,
              ffn2_norm, ffn2_w_gate, ffn2_w_up, ffn2_w_down):
    B, S, _ = x.shape
    h = x
    for _layer in range(DEPTH):
        h = h + 0.5 * swiglu(rmsnorm(h, ffn1_norm), ffn1_w_gate, ffn1_w_up, ffn1_w_down)
        u = rmsnorm(h, mix_norm)
        proj = u @ w_in
        o = 0
        xp = proj[..., o:o + POOL_WIDTH]; o += POOL_WIDTH
        q = proj[..., o:o + ATTN_WIDTH].reshape(B, S, N_HEADS, HEAD_DIM); o += ATTN_WIDTH
        k = proj[..., o:o + KV_WIDTH].reshape(B, S, N_KV_HEADS, HEAD_DIM); o += KV_WIDTH
        v = proj[..., o:o + KV_WIDTH].reshape(B, S, N_KV_HEADS, HEAD_DIM); o += KV_WIDTH
        gates = jax.nn.sigmoid(proj[..., o:o + N_BRANCHES * D_MODEL] + gate_bias)
        gate_pool = gates[..., :D_MODEL]
        gate_attn = gates[..., D_MODEL:]
        branch_pool = causal_pool_mixer(xp, pool_w, pool_scale) @ w_pool_out
        branch_attn = sliding_window_sink_attention(q, k, v, q_norm, k_norm, sinks) @ w_attn_out
        merged = gate_pool * branch_pool + gate_attn * branch_attn
        h = h + merged @ w_out
        h = h + 0.5 * swiglu(rmsnorm(h, ffn2_norm), ffn2_w_gate, ffn2_w_up, ffn2_w_down)
    return h
```

```python
import functools

import jax
import jax.numpy as jnp
from jax import lax
from jax.experimental import pallas as pl
from jax.experimental.pallas import tpu as pltpu

F32 = jnp.float32
BF16 = jnp.bfloat16

RMS_EPS = 1e-6
HEAD_DIM = 64
N_HEADS = 16
N_KV_HEADS = 2
PAIRS_PER_KV = N_HEADS // N_KV_HEADS // 2
BLOCK = 128
POOL_WINDOWS = (2, 4, 8, 16)
POOL_GROUP = 128
POOL_HALO = 16
LANES = 128
V7X_VMEM_BYTES = 64 * 1024 * 1024
VMEM_LIMIT = V7X_VMEM_BYTES - 8 * 1024 * 1024

FFN_TILE = 512
FFN_CHUNK = 256
MIX_TILE = 512
MIX_COL_CHUNK = 256


def _dot(a, b):
    return jnp.dot(a, b, preferred_element_type=F32)


def _dot_nt(a, b):
    return lax.dot_general(a, b, (((1,), (1,)), ((), ())), preferred_element_type=F32)


def _rmsnorm_rows(x, g):
    ms = jnp.mean(x * x, axis=-1, keepdims=True)
    return x * lax.rsqrt(ms + RMS_EPS) * g


def _ffn_kernel(x_ref, g_ref, wg_ref, wu_ref, wd_ref, o_ref, acc_ref, *, d_ff, chunk):
    x = x_ref[...]
    xn = _rmsnorm_rows(x, g_ref[...]).astype(BF16)
    for c in range(d_ff // chunk):
        cols = slice(c * chunk, (c + 1) * chunk)
        g = _dot(xn, wg_ref[:, cols])
        u = _dot(xn, wu_ref[:, cols])
        a = (g * jax.nn.sigmoid(g) * u).astype(BF16)
        d = _dot(a, wd_ref[cols, :])
        if c == 0:
            acc_ref[...] = d
        else:
            acc_ref[...] += d
    o_ref[...] = x + 0.5 * acc_ref[...]


def _resident(shape):
    return pl.BlockSpec(shape, lambda i: (0,) * len(shape), pipeline_mode=pl.Buffered(1))


def _ffn(x, norm_g, w_gate, w_up, w_down, *, name):
    n, d = x.shape
    d_ff = w_gate.shape[1]
    assert n % FFN_TILE == 0 and d_ff % FFN_CHUNK == 0
    row_spec = pl.BlockSpec((FFN_TILE, d), lambda i: (i, 0))
    return pl.pallas_call(
        functools.partial(_ffn_kernel, d_ff=d_ff, chunk=FFN_CHUNK),
        grid=(n // FFN_TILE,),
        in_specs=[row_spec, _resident((1, d)), _resident((d, d_ff)), _resident((d, d_ff)),
                  _resident((d_ff, d))],
        out_specs=row_spec,
        out_shape=jax.ShapeDtypeStruct((n, d), F32),
        scratch_shapes=[pltpu.VMEM((FFN_TILE, d), F32)],
        compiler_params=pltpu.CompilerParams(dimension_semantics=("arbitrary",),
                                             vmem_limit_bytes=VMEM_LIMIT),
        name=name,
    )(x, norm_g.reshape(1, d), w_gate.astype(BF16), w_up.astype(BF16), w_down.astype(BF16))


def _head_mean_sq(x, blockdiag):
    x2 = x * x
    hi = x2.astype(BF16)
    lo = (x2 - hi.astype(F32)).astype(BF16)
    return (_dot(hi, blockdiag) + _dot(lo, blockdiag)) * (1.0 / HEAD_DIM)


def _split_kv_heads(x):
    swapped = pltpu.roll(x, HEAD_DIM, axis=1)
    low = lax.broadcasted_iota(jnp.int32, x.shape, 1) < HEAD_DIM
    zero = jnp.zeros_like(x)
    parts = (jnp.where(low, x, zero), jnp.where(low, zero, swapped),
             jnp.where(low, swapped, zero), jnp.where(low, zero, x))
    return tuple(p.astype(BF16) for p in parts)


def _mixer_kernel(sinks_ref, h_ref, mixg_ref, w_in_ref, gbias_ref, poolw_ref, pscale_ref, wpo_ref,
                  qg_ref, kg_ref, wao_ref, wout_ref, o_ref,
                  u_scr, xp_scr, q_scr, ka_scr, kb_scr, ve_scr, vo_scr, attn_scr, mixed_scr,
                  merged_scr, *, tiles_per_seq, d_model, pool_width, attn_width, kv_width):
    tm = h_ref.shape[0]
    i = pl.program_id(0)
    tile_in_seq = i % tiles_per_seq
    seq_start = tile_in_seq == 0
    lane128 = lax.broadcasted_iota(jnp.int32, (1, LANES), 1)

    off_q = pool_width
    off_k = off_q + attn_width
    off_v = off_k + kv_width
    off_gp = off_v + kv_width
    off_ga = off_gp + d_model

    @pl.when(seq_start)
    def _():
        xp_scr[0:POOL_HALO, :] = jnp.zeros((POOL_HALO, pool_width), F32)
        zk = jnp.zeros((BLOCK, LANES), BF16)
        ones_e = jnp.broadcast_to(jnp.where(lane128 < HEAD_DIM, 1.0, 0.0).astype(BF16),
                                  (BLOCK + tm, LANES))
        ones_o = jnp.broadcast_to(jnp.where(lane128 < HEAD_DIM, 0.0, 1.0).astype(BF16),
                                  (BLOCK + tm, LANES))
        for kv in range(N_KV_HEADS):
            ka_scr[kv, 0:BLOCK, :] = zk
            kb_scr[kv, 0:BLOCK, :] = zk
            ve_scr[kv, 0:BLOCK, 0:LANES] = zk
            vo_scr[kv, 0:BLOCK, 0:LANES] = zk
            ve_scr[kv, :, LANES:2 * LANES] = ones_e
            vo_scr[kv, :, LANES:2 * LANES] = ones_o

    h = h_ref[...]
    u_scr[...] = _rmsnorm_rows(h, mixg_ref[...]).astype(BF16)

    xp = _dot(u_scr[...], w_in_ref[:, 0:pool_width])
    xp_scr[POOL_HALO:POOL_HALO + tm, :] = xp

    r256 = lax.broadcasted_iota(jnp.int32, (2 * LANES, 2 * LANES), 0) // HEAD_DIM
    c256 = lax.broadcasted_iota(jnp.int32, (2 * LANES, 2 * LANES), 1) // HEAD_DIM
    blockdiag = jnp.where(r256 == c256, 1.0, 0.0).astype(BF16)

    q_gain = qg_ref[...] * (HEAD_DIM ** -0.5)
    for c in range(attn_width // (2 * LANES)):
        cols = slice(c * 2 * LANES, (c + 1) * 2 * LANES)
        q = _dot(u_scr[...], w_in_ref[:, off_q + c * 2 * LANES: off_q + (c + 1) * 2 * LANES])
        ms = _head_mean_sq(q, blockdiag)
        q_scr[:, cols] = (q * lax.rsqrt(ms + RMS_EPS) * q_gain).astype(BF16)

    k = _dot(u_scr[...], w_in_ref[:, off_k:off_k + kv_width])
    ms = _head_mean_sq(k, blockdiag[0:LANES, 0:LANES])
    k = k * lax.rsqrt(ms + RMS_EPS) * kg_ref[...]
    ka0, kb0, ka1, kb1 = _split_kv_heads(k)
    ka_scr[0, BLOCK:BLOCK + tm, :] = ka0
    kb_scr[0, BLOCK:BLOCK + tm, :] = kb0
    ka_scr[1, BLOCK:BLOCK + tm, :] = ka1
    kb_scr[1, BLOCK:BLOCK + tm, :] = kb1

    v = _dot(u_scr[...], w_in_ref[:, off_v:off_v + kv_width])
    va0, vb0, va1, vb1 = _split_kv_heads(v)
    ve_scr[0, BLOCK:BLOCK + tm, 0:LANES] = va0
    vo_scr[0, BLOCK:BLOCK + tm, 0:LANES] = vb0
    ve_scr[1, BLOCK:BLOCK + tm, 0:LANES] = va1
    vo_scr[1, BLOCK:BLOCK + tm, 0:LANES] = vb1

    t = tile_in_seq * tm + lax.broadcasted_iota(jnp.int32, (tm, 1), 0)
    for gi, w in enumerate(POOL_WINDOWS):
        cols = slice(gi * POOL_GROUP, (gi + 1) * POOL_GROUP)
        cur = xp_scr[POOL_HALO:POOL_HALO + tm, cols]
        wsum = cur
        for j in range(1, w):
            wsum = wsum + xp_scr[POOL_HALO - j:POOL_HALO - j + tm, cols]
        count = jnp.minimum(t + 1, w).astype(F32)
        pooled = (wsum / count - cur).astype(BF16)
        mixed = _dot(pooled, poolw_ref[gi]) * pscale_ref[:, cols]
        mixed_scr[:, cols] = mixed.astype(BF16)

    row = lax.broadcasted_iota(jnp.int32, (BLOCK, 2 * BLOCK), 0)
    col = lax.broadcasted_iota(jnp.int32, (BLOCK, 2 * BLOCK), 1)
    band = (col > row) & (col <= row + BLOCK)
    neg = jnp.finfo(F32).min

    def attn_block(b, carry):
        row0 = pl.multiple_of(b * BLOCK, BLOCK)
        first_key = jnp.where(seq_start & (b == 0), BLOCK, 0)
        mask = band & (col >= first_key)
        for kv in range(N_KV_HEADS):
            base = kv * PAIRS_PER_KV
            qs = jnp.concatenate(
                [q_scr[pl.ds(row0, BLOCK), (base + j) * LANES:(base + j + 1) * LANES]
                 for j in range(PAIRS_PER_KV)], axis=0)
            scores = (_dot_nt(qs, ka_scr[kv, pl.ds(row0, 2 * BLOCK), :]),
                      _dot_nt(qs, kb_scr[kv, pl.ds(row0, 2 * BLOCK), :]))
            probs = ([], [])
            sink_terms = []
            for j in range(PAIRS_PER_KV):
                es = []
                for par in range(2):
                    s = scores[par][j * BLOCK:(j + 1) * BLOCK]
                    s = jnp.where(mask, s, neg)
                    sink = sinks_ref[kv * 2 * PAIRS_PER_KV + 2 * j + par]
                    m = jnp.maximum(jnp.max(s, axis=-1, keepdims=True), sink)
                    probs[par].append(jnp.exp(s - m).astype(BF16))
                    es.append(jnp.exp(sink - m))
                sink_terms.append(jnp.where(lane128 < HEAD_DIM, es[0], es[1]))
            p_e = jnp.concatenate(probs[0], axis=0)
            p_o = jnp.concatenate(probs[1], axis=0)
            acc = (_dot(p_e, ve_scr[kv, pl.ds(row0, 2 * BLOCK), :])
                   + _dot(p_o, vo_scr[kv, pl.ds(row0, 2 * BLOCK), :]))
            den = acc[:, LANES:2 * LANES] + jnp.concatenate(sink_terms, axis=0)
            out = (acc[:, 0:LANES] / den).astype(BF16)
            for j in range(PAIRS_PER_KV):
                attn_scr[pl.ds(row0, BLOCK), (base + j) * LANES:(base + j + 1) * LANES] = (
                    out[j * BLOCK:(j + 1) * BLOCK])
        return carry

    lax.fori_loop(0, tm // BLOCK, attn_block, 0)

    for c in range(d_model // MIX_COL_CHUNK):
        cols = slice(c * MIX_COL_CHUNK, (c + 1) * MIX_COL_CHUNK)
        gp = jax.nn.sigmoid(
            _dot(u_scr[...], w_in_ref[:, off_gp + c * MIX_COL_CHUNK: off_gp + (c + 1) * MIX_COL_CHUNK])
            + gbias_ref[:, cols])
        ga = jax.nn.sigmoid(
            _dot(u_scr[...], w_in_ref[:, off_ga + c * MIX_COL_CHUNK: off_ga + (c + 1) * MIX_COL_CHUNK])
            + gbias_ref[:, d_model + c * MIX_COL_CHUNK: d_model + (c + 1) * MIX_COL_CHUNK])
        bp = _dot(mixed_scr[...], wpo_ref[:, cols])
        ba = _dot(attn_scr[...], wao_ref[:, cols])
        merged_scr[:, cols] = (gp * bp + ga * ba).astype(BF16)
    o_ref[...] = h + _dot(merged_scr[...], wout_ref[...])

    xp_scr[0:POOL_HALO, :] = xp_scr[tm:tm + POOL_HALO, :]
    for kv in range(N_KV_HEADS):
        ka_scr[kv, 0:BLOCK, :] = ka_scr[kv, tm:tm + BLOCK, :]
        kb_scr[kv, 0:BLOCK, :] = kb_scr[kv, tm:tm + BLOCK, :]
        ve_scr[kv, 0:BLOCK, 0:LANES] = ve_scr[kv, tm:tm + BLOCK, 0:LANES]
        vo_scr[kv, 0:BLOCK, 0:LANES] = vo_scr[kv, tm:tm + BLOCK, 0:LANES]


def _mixer(h, seq_len, mix_norm, w_in, pool_w, pool_scale, w_pool_out, q_norm, k_norm, sinks,
           w_attn_out, gate_bias, w_out):
    n, d = h.shape
    tm = MIX_TILE
    pool_width = w_pool_out.shape[0]
    attn_width = w_attn_out.shape[0]
    kv_width = N_KV_HEADS * HEAD_DIM
    in_width = w_in.shape[1]
    assert seq_len % tm == 0 and tm % BLOCK == 0
    assert pool_width == len(POOL_WINDOWS) * POOL_GROUP and attn_width == N_HEADS * HEAD_DIM
    assert in_width == pool_width + attn_width + 2 * kv_width + 2 * d and kv_width == LANES
    row_spec = pl.BlockSpec((tm, d), lambda i: (i, 0))
    kern = functools.partial(_mixer_kernel, tiles_per_seq=seq_len // tm, d_model=d,
                             pool_width=pool_width, attn_width=attn_width, kv_width=kv_width)
    return pl.pallas_call(
        kern,
        grid=(n // tm,),
        in_specs=[pl.BlockSpec(memory_space=pltpu.SMEM),
                  row_spec, _resident((1, d)), _resident((d, in_width)), _resident((1, 2 * d)),
                  _resident(pool_w.shape), _resident((1, pool_width)), _resident((pool_width, d)),
                  _resident((1, 2 * LANES)), _resident((1, LANES)),
                  _resident((attn_width, d)), _resident((d, d))],
        out_specs=row_spec,
        out_shape=jax.ShapeDtypeStruct((n, d), F32),
        scratch_shapes=[
            pltpu.VMEM((tm, d), BF16),
            pltpu.VMEM((POOL_HALO + tm, pool_width), F32),
            pltpu.VMEM((tm, attn_width), BF16),
            pltpu.VMEM((N_KV_HEADS, BLOCK + tm, LANES), BF16),
            pltpu.VMEM((N_KV_HEADS, BLOCK + tm, LANES), BF16),
            pltpu.VMEM((N_KV_HEADS, BLOCK + tm, 2 * LANES), BF16),
            pltpu.VMEM((N_KV_HEADS, BLOCK + tm, 2 * LANES), BF16),
            pltpu.VMEM((tm, attn_width), BF16),
            pltpu.VMEM((tm, pool_width), BF16),
            pltpu.VMEM((tm, d), BF16),
        ],
        compiler_params=pltpu.CompilerParams(dimension_semantics=("arbitrary",),
                                             vmem_limit_bytes=VMEM_LIMIT),
        name="mixer",
    )(sinks, h, mix_norm.reshape(1, d), w_in.astype(BF16), gate_bias.reshape(1, 2 * d),
      pool_w.astype(BF16), pool_scale.reshape(1, pool_width), w_pool_out.astype(BF16),
      jnp.tile(q_norm, 2 * LANES // HEAD_DIM).reshape(1, 2 * LANES),
      jnp.tile(k_norm, LANES // HEAD_DIM).reshape(1, LANES),
      w_attn_out.astype(BF16), w_out.astype(BF16))


def kernel(x, ffn1_norm, ffn1_w_gate, ffn1_w_up, ffn1_w_down, mix_norm, w_in, pool_w, pool_scale,
           w_pool_out, q_norm, k_norm, sinks, w_attn_out, gate_bias, w_out, ffn2_norm, ffn2_w_gate,
           ffn2_w_up, ffn2_w_down):
    b, s, d = x.shape
    h = x.reshape(b * s, d)
    h = _ffn(h, ffn1_norm, ffn1_w_gate, ffn1_w_up, ffn1_w_down, name="ffn1")
    h = _mixer(h, s, mix_norm, w_in, pool_w, pool_scale, w_pool_out, q_norm, k_norm, sinks,
               w_attn_out, gate_bias, w_out)
    h = _ffn(h, ffn2_norm, ffn2_w_gate, ffn2_w_up, ffn2_w_down, name="ffn2")
    return h.reshape(b, s, d)
```

```python
import functools
import math

import jax
import jax.numpy as jnp
from jax import lax
from jax.experimental import pallas as pl
from jax.experimental.pallas import tpu as pltpu

F32 = jnp.float32
BF16 = jnp.bfloat16

RMS_EPS = 1e-6
HEAD_DIM = 64
N_HEADS = 16
N_KV_HEADS = 2
PAIRS_PER_KV = N_HEADS // N_KV_HEADS // 2
BLOCK = 128
POOL_WINDOWS = (2, 4, 8, 16)
POOL_GROUP = 128
POOL_HALO = 16
LANES = 128
V7X_VMEM_BYTES = 64 * 1024 * 1024
VMEM_LIMIT = V7X_VMEM_BYTES - 8 * 1024 * 1024
LOG2E = math.log2(math.e)

FFN_TILE = 1024
FFN_CHUNK = 256
MIX_TILE = 1024
MIX_COL_CHUNK = 256


def _dot(a, b):
    return jnp.dot(a, b, preferred_element_type=F32)


def _dot_rows_split(a, b):
    half = a.shape[0] // 2
    return jnp.concatenate([_dot(a[:half], b), _dot(a[half:], b)], axis=0)


def _dot_nt(a, b):
    return lax.dot_general(a, b, (((1,), (1,)), ((), ())), preferred_element_type=F32)


def _rmsnorm_rows(x, g):
    ms = jnp.mean(x * x, axis=-1, keepdims=True)
    return x * lax.rsqrt(ms + RMS_EPS) * g


def _ffn_kernel(x_ref, g_ref, wg_ref, wu_ref, wd_ref, o_ref, acc_ref, *, d_ff, chunk):
    x = x_ref[...]
    xn = _rmsnorm_rows(x, g_ref[...]).astype(BF16)
    for c in range(d_ff // chunk):
        cols = slice(c * chunk, (c + 1) * chunk)
        g = _dot(xn, wg_ref[:, cols])
        u = _dot(xn, wu_ref[:, cols])
        a = (g * jax.nn.sigmoid(g) * u).astype(BF16)
        d = _dot(a, wd_ref[cols, :])
        if c == 0:
            acc_ref[...] = d
        else:
            acc_ref[...] += d
    o_ref[...] = x + 0.5 * acc_ref[...]


def _resident(shape):
    return pl.BlockSpec(shape, lambda i: (0,) * len(shape), pipeline_mode=pl.Buffered(1))


def _ffn(x, norm_g, w_gate, w_up, w_down, *, name):
    n, d = x.shape
    d_ff = w_gate.shape[1]
    assert n % FFN_TILE == 0 and d_ff % FFN_CHUNK == 0
    row_spec = pl.BlockSpec((FFN_TILE, d), lambda i: (i, 0))
    return pl.pallas_call(
        functools.partial(_ffn_kernel, d_ff=d_ff, chunk=FFN_CHUNK),
        grid=(n // FFN_TILE,),
        in_specs=[row_spec, _resident((1, d)), _resident((d, d_ff)), _resident((d, d_ff)),
                  _resident((d_ff, d))],
        out_specs=row_spec,
        out_shape=jax.ShapeDtypeStruct((n, d), F32),
        scratch_shapes=[pltpu.VMEM((FFN_TILE, d), F32)],
        compiler_params=pltpu.CompilerParams(dimension_semantics=("arbitrary",),
                                             vmem_limit_bytes=VMEM_LIMIT),
        name=name,
    )(x, norm_g.reshape(1, d), w_gate.astype(BF16), w_up.astype(BF16), w_down.astype(BF16))


def _head_mean_sq(x, blockdiag):
    x2 = x * x
    hi = x2.astype(BF16)
    lo = (x2 - hi.astype(F32)).astype(BF16)
    return (_dot(hi, blockdiag) + _dot(lo, blockdiag)) * (1.0 / HEAD_DIM)


def _split_kv_heads(x):
    swapped = pltpu.roll(x, HEAD_DIM, axis=1)
    low = lax.broadcasted_iota(jnp.int32, x.shape, 1) < HEAD_DIM
    zero = jnp.zeros_like(x)
    parts = (jnp.where(low, x, zero), jnp.where(low, zero, swapped),
             jnp.where(low, swapped, zero), jnp.where(low, zero, x))
    return tuple(p.astype(BF16) for p in parts)


def _mixer_kernel(h_ref, mixg_ref, w_in_ref, gbias_ref, poolw_ref, pscale_ref, wpo_ref,
                  qg_ref, kg_ref, sinks_ref, wao_ref, wout_ref, o_ref,
                  u_scr, xp_scr, q32_scr, q_scr, ka_scr, kb_scr, ve_scr, vo_scr, attn_scr, mixed_scr,
                  merged_scr, *, tiles_per_seq, d_model, pool_width, attn_width, kv_width):
    tm = h_ref.shape[0]
    i = pl.program_id(0)
    tile_in_seq = i % tiles_per_seq
    seq_start = tile_in_seq == 0
    lane128 = lax.broadcasted_iota(jnp.int32, (1, LANES), 1)
    low_half = lane128 < HEAD_DIM

    off_q = pool_width
    off_k = off_q + attn_width
    off_gp = off_k + 2 * kv_width
    off_ga = off_gp + d_model

    @pl.when(seq_start)
    def _():
        xp_scr[0:POOL_HALO, :] = jnp.zeros((POOL_HALO, pool_width), F32)
        zk = jnp.zeros((BLOCK, LANES), BF16)
        ones_e = jnp.broadcast_to(jnp.where(low_half, 1.0, 0.0).astype(BF16), (BLOCK + tm, LANES))
        ones_o = jnp.broadcast_to(jnp.where(low_half, 0.0, 1.0).astype(BF16), (BLOCK + tm, LANES))
        for kv in range(N_KV_HEADS):
            ka_scr[kv, 0:BLOCK, :] = zk
            kb_scr[kv, 0:BLOCK, :] = zk
            ve_scr[kv, 0:BLOCK, 0:LANES] = zk
            vo_scr[kv, 0:BLOCK, 0:LANES] = zk
            ve_scr[kv, :, LANES:2 * LANES] = ones_e
            vo_scr[kv, :, LANES:2 * LANES] = ones_o

    @pl.when(jnp.logical_not(seq_start))
    def _():
        xp_scr[0:POOL_HALO, :] = xp_scr[tm:tm + POOL_HALO, :]
        for kv in range(N_KV_HEADS):
            ka_scr[kv, 0:BLOCK, :] = ka_scr[kv, tm:tm + BLOCK, :]
            kb_scr[kv, 0:BLOCK, :] = kb_scr[kv, tm:tm + BLOCK, :]
            ve_scr[kv, 0:BLOCK, 0:LANES] = ve_scr[kv, tm:tm + BLOCK, 0:LANES]
            vo_scr[kv, 0:BLOCK, 0:LANES] = vo_scr[kv, tm:tm + BLOCK, 0:LANES]

    h = h_ref[...]
    u_scr[...] =_rmsnorm_rows(h, mixg_ref[...]).astype(BF16)

    xp_scr[POOL_HALO:POOL_HALO + tm, :] = _dot(u_scr[...], w_in_ref[:, 0:pool_width])

    r256 = lax.broadcasted_iota(jnp.int32, (2 * LANES, 2 * LANES), 0) // HEAD_DIM
    c256 = lax.broadcasted_iota(jnp.int32, (2 * LANES, 2 * LANES), 1) // HEAD_DIM
    blockdiag = jnp.where(r256 == c256, 1.0, 0.0).astype(BF16)

    q_gain = qg_ref[...] * (HEAD_DIM ** -0.5 * LOG2E)
    q32_scr[...] = _dot(u_scr[...], w_in_ref[:, off_q:off_q + attn_width])
    kv_proj = _dot_rows_split(u_scr[...], w_in_ref[:, off_k:off_k + 2 * kv_width])
    for c in range(attn_width // (2 * LANES)):
        cols = slice(c * 2 * LANES, (c + 1) * 2 * LANES)
        q = q32_scr[:, cols]
        ms = _head_mean_sq(q, blockdiag)
        q_scr[:, cols] = (q * lax.rsqrt(ms + RMS_EPS) * q_gain).astype(BF16)

    k = kv_proj[:, 0:kv_width]
    ms = _head_mean_sq(k, blockdiag[0:LANES, 0:LANES])
    k = k * lax.rsqrt(ms + RMS_EPS) * kg_ref[...]
    ka0, kb0, ka1, kb1 = _split_kv_heads(k)
    ka_scr[0, BLOCK:BLOCK + tm, :] = ka0
    kb_scr[0, BLOCK:BLOCK + tm, :] = kb0
    ka_scr[1, BLOCK:BLOCK + tm, :] = ka1
    kb_scr[1, BLOCK:BLOCK + tm, :] = kb1
    va0, vb0, va1, vb1 = _split_kv_heads(kv_proj[:, kv_width:2 * kv_width])
    ve_scr[0, BLOCK:BLOCK + tm, 0:LANES] = va0
    vo_scr[0, BLOCK:BLOCK + tm, 0:LANES] = vb0
    ve_scr[1, BLOCK:BLOCK + tm, 0:LANES] = va1
    vo_scr[1, BLOCK:BLOCK + tm, 0:LANES] = vb1

    t = tile_in_seq * tm + lax.broadcasted_iota(jnp.int32, (tm, 1), 0)
    pooled = []
    for gi, w in enumerate(POOL_WINDOWS):
        cols = slice(gi * POOL_GROUP, (gi + 1) * POOL_GROUP)
        ext = xp_scr[:, cols]
        wsum = ext
        shift = 1
        while shift < w:
            wsum = wsum + pltpu.roll(wsum, shift, axis=0)
            shift *= 2
        count = jnp.minimum(t + 1, w).astype(F32)
        pooled.append((wsum[POOL_HALO:] / count - ext[POOL_HALO:]).astype(BF16))
    for c in range(len(POOL_WINDOWS) // 2):
        cols = slice(c * 2 * POOL_GROUP, (c + 1) * 2 * POOL_GROUP)
        pair = jnp.concatenate(pooled[2 * c:2 * c + 2], axis=1)
        mixed_scr[:, cols] = (_dot(pair, poolw_ref[c]) * pscale_ref[:, cols]).astype(BF16)

    row = lax.broadcasted_iota(jnp.int32, (BLOCK, 2 * BLOCK), 0)
    col = lax.broadcasted_iota(jnp.int32, (BLOCK, 2 * BLOCK), 1)
    band = (col > row) & (col <= row + BLOCK)
    first_key = jnp.where(seq_start, BLOCK, 0)
    band_first = band & (col >= first_key)
    neg = jnp.finfo(F32).min
    sinks2 = sinks_ref[...] * LOG2E

    for b in range(tm // BLOCK):
        rows = slice(b * BLOCK, (b + 1) * BLOCK)
        keys = slice(b * BLOCK, (b + 2) * BLOCK)
        mask = band_first if b == 0 else band
        for kv in range(N_KV_HEADS):
            base = kv * PAIRS_PER_KV
            qs = jnp.concatenate([q_scr[rows, (base + j) * LANES:(base + j + 1) * LANES]
                                  for j in range(PAIRS_PER_KV)], axis=0)
            scores = (_dot_nt(qs, ka_scr[kv, keys, :]),
                      _dot_nt(qs, kb_scr[kv, keys, :]))
            probs = ([], [])
            sink_terms = []
            for j in range(PAIRS_PER_KV):
                m = []
                for par in range(2):
                    s = jnp.where(mask, scores[par][j * BLOCK:(j + 1) * BLOCK], neg)
                    m.append(jnp.max(s, axis=-1, keepdims=True))
                    probs[par].append(jnp.exp2(s - m[par]).astype(BF16))
                sink_terms.append(jnp.exp2(sinks2[base + j:base + j + 1, :]
                                           - jnp.where(low_half, m[0], m[1])))
            p = jnp.concatenate([jnp.concatenate(probs[0], axis=0),
                                 jnp.concatenate(probs[1], axis=0)], axis=1)
            v_stack = jnp.concatenate([ve_scr[kv, keys, :], vo_scr[kv, keys, :]], axis=0)
            acc = _dot_rows_split(p, v_stack)
            den = acc[:, LANES:2 * LANES] + jnp.concatenate(sink_terms, axis=0)
            out = (acc[:, 0:LANES] / den).astype(BF16)
            for j in range(PAIRS_PER_KV):
                attn_scr[rows, (base + j) * LANES:(base + j + 1) * LANES] = (
                    out[j * BLOCK:(j + 1) * BLOCK])

    for c in range(d_model // MIX_COL_CHUNK):
        cols = slice(c * MIX_COL_CHUNK, (c + 1) * MIX_COL_CHUNK)
        gp = jax.nn.sigmoid(
            _dot(u_scr[...], w_in_ref[:, off_gp + c * MIX_COL_CHUNK: off_gp + (c + 1) * MIX_COL_CHUNK])
            + gbias_ref[:, cols])
        ga = jax.nn.sigmoid(
            _dot(u_scr[...], w_in_ref[:, off_ga + c * MIX_COL_CHUNK: off_ga + (c + 1) * MIX_COL_CHUNK])
            + gbias_ref[:, d_model + c * MIX_COL_CHUNK: d_model + (c + 1) * MIX_COL_CHUNK])
        bp = _dot(mixed_scr[...], wpo_ref[:, cols])
        ba = _dot(attn_scr[...], wao_ref[:, cols])
        merged_scr[:, cols] = (gp * bp + ga * ba).astype(BF16)
    o_ref[...] = h + _dot(merged_scr[...], wout_ref[...])


def _pair_blockdiag(pool_w):
    g, n, _ = pool_w.shape
    z = jnp.zeros((n, n), pool_w.dtype)
    return jnp.stack([jnp.block([[pool_w[2 * c], z], [z, pool_w[2 * c + 1]]])
                      for c in range(g // 2)])


def _mixer(h, seq_len, mix_norm, w_in, pool_w, pool_scale, w_pool_out, q_norm, k_norm, sinks,
           w_attn_out, gate_bias, w_out):
    n, d = h.shape
    tm = MIX_TILE
    pool_width = w_pool_out.shape[0]
    attn_width = w_attn_out.shape[0]
    kv_width = N_KV_HEADS * HEAD_DIM
    in_width = w_in.shape[1]
    assert seq_len % tm == 0 and tm % BLOCK == 0
    assert pool_width == len(POOL_WINDOWS) * POOL_GROUP and attn_width == N_HEADS * HEAD_DIM
    assert in_width == pool_width + attn_width + 2 * kv_width + 2 * d and kv_width == LANES
    row_spec = pl.BlockSpec((tm, d), lambda i: (i, 0))
    kern = functools.partial(_mixer_kernel, tiles_per_seq=seq_len // tm, d_model=d,
                             pool_width=pool_width, attn_width=attn_width, kv_width=kv_width)
    pool_w2 = _pair_blockdiag(pool_w).astype(BF16)
    sink_rows = jnp.repeat(sinks.reshape(N_HEADS // 2, 2), HEAD_DIM, axis=1)
    return pl.pallas_call(
        kern,
        grid=(n // tm,),
        in_specs=[row_spec, _resident((1, d)), _resident((d, in_width)), _resident((1, 2 * d)),
                  _resident(pool_w2.shape), _resident((1, pool_width)), _resident((pool_width, d)),
                  _resident((1, 2 * LANES)), _resident((1, LANES)), _resident(sink_rows.shape),
                  _resident((attn_width, d)), _resident((d, d))],
        out_specs=row_spec,
        out_shape=jax.ShapeDtypeStruct((n, d), F32),
        scratch_shapes=[
            pltpu.VMEM((tm, d), BF16),
            pltpu.VMEM((POOL_HALO + tm, pool_width), F32),
            pltpu.VMEM((tm, attn_width), F32),
            pltpu.VMEM((tm, attn_width), BF16),
            pltpu.VMEM((N_KV_HEADS, BLOCK + tm, LANES), BF16),
            pltpu.VMEM((N_KV_HEADS, BLOCK + tm, LANES), BF16),
            pltpu.VMEM((N_KV_HEADS, BLOCK + tm, 2 * LANES), BF16),
            pltpu.VMEM((N_KV_HEADS, BLOCK + tm, 2 * LANES), BF16),
            pltpu.VMEM((tm, attn_width), BF16),
            pltpu.VMEM((tm, pool_width), BF16),
            pltpu.VMEM((tm, d), BF16),
        ],
        compiler_params=pltpu.CompilerParams(dimension_semantics=("arbitrary",),
                                             vmem_limit_bytes=VMEM_LIMIT),
        name="mixer",
    )(h, mix_norm.reshape(1, d), w_in.astype(BF16), gate_bias.reshape(1, 2 * d),
      pool_w2, pool_scale.reshape(1, pool_width), w_pool_out.astype(BF16),
      jnp.tile(q_norm, 2 * LANES // HEAD_DIM).reshape(1, 2 * LANES),
      jnp.tile(k_norm, LANES // HEAD_DIM).reshape(1, LANES), sink_rows,
      w_attn_out.astype(BF16), w_out.astype(BF16))


def kernel(x, ffn1_norm, ffn1_w_gate, ffn1_w_up, ffn1_w_down, mix_norm, w_in, pool_w, pool_scale,
           w_pool_out, q_norm, k_norm, sinks, w_attn_out, gate_bias, w_out, ffn2_norm, ffn2_w_gate,
           ffn2_w_up, ffn2_w_down):
    b, s, d = x.shape
    h = x.reshape(b * s, d)
    h = _ffn(h, ffn1_norm, ffn1_w_gate, ffn1_w_up, ffn1_w_down, name="ffn1")
    h = _mixer(h, s, mix_norm, w_in, pool_w, pool_scale, w_pool_out, q_norm, k_norm, sinks,
               w_attn_out, gate_bias, w_out)
    h = _ffn(h, ffn2_norm, ffn2_w_gate, ffn2_w_up, ffn2_w_down, name="ffn2")
    return h.reshape(b, s, d)
```

```python
import functools
import math

import jax
import jax.numpy as jnp
from jax import lax
from jax.experimental import pallas as pl
from jax.experimental.pallas import tpu as pltpu

F32 = jnp.float32
BF16 = jnp.bfloat16

RMS_EPS = 1e-6
HEAD_DIM = 64
N_HEADS = 16
N_KV_HEADS = 2
PAIRS_PER_KV = N_HEADS // N_KV_HEADS // 2
BLOCK = 128
POOL_WINDOWS = (2, 4, 8, 16)
POOL_GROUP = 128
POOL_HALO = 16
LANES = 128
V7X_VMEM_BYTES = 64 * 1024 * 1024
VMEM_LIMIT = V7X_VMEM_BYTES - 8 * 1024 * 1024
LOG2E = math.log2(math.e)
MASK_BIAS = -3.0e38

FFN_TILE = 1024
FFN_CHUNK = 256
MIX_TILE = 1024
MIX_COL_CHUNK = 256


def _dot(a, b):
    return jnp.dot(a, b, preferred_element_type=F32)


def _dot_rows_split(a, b):
    half = a.shape[0] // 2
    return jnp.concatenate([_dot(a[:half], b), _dot(a[half:], b)], axis=0)


def _dot_nt(a, b):
    return lax.dot_general(a, b, (((1,), (1,)), ((), ())), preferred_element_type=F32)


def _rmsnorm_rows(x, g):
    ms = jnp.mean(x * x, axis=-1, keepdims=True)
    return x * lax.rsqrt(ms + RMS_EPS) * g


def _ffn_kernel(x_ref, g_ref, wg_ref, wu_ref, wd_ref, o_ref, acc_ref, *, d_ff, chunk):
    x = x_ref[...]
    xn = _rmsnorm_rows(x, g_ref[...]).astype(BF16)
    for c in range(d_ff // chunk):
        cols = slice(c * chunk, (c + 1) * chunk)
        g = _dot(xn, wg_ref[:, cols])
        u = _dot(xn, wu_ref[:, cols])
        a = (g * jax.nn.sigmoid(g) * u).astype(BF16)
        d = _dot(a, wd_ref[cols, :])
        if c == 0:
            acc_ref[...] = d
        else:
            acc_ref[...] += d
    o_ref[...] = x + 0.5 * acc_ref[...]


def _resident(shape):
    return pl.BlockSpec(shape, lambda i: (0,) * len(shape), pipeline_mode=pl.Buffered(1))


def _ffn(x, norm_g, w_gate, w_up, w_down, *, name):
    n, d = x.shape
    d_ff = w_gate.shape[1]
    assert n % FFN_TILE == 0 and d_ff % FFN_CHUNK == 0
    row_spec = pl.BlockSpec((FFN_TILE, d), lambda i: (i, 0))
    return pl.pallas_call(
        functools.partial(_ffn_kernel, d_ff=d_ff, chunk=FFN_CHUNK),
        grid=(n // FFN_TILE,),
        in_specs=[row_spec, _resident((1, d)), _resident((d, d_ff)), _resident((d, d_ff)),
                  _resident((d_ff, d))],
        out_specs=row_spec,
        out_shape=jax.ShapeDtypeStruct((n, d), F32),
        scratch_shapes=[pltpu.VMEM((FFN_TILE, d), F32)],
        compiler_params=pltpu.CompilerParams(dimension_semantics=("arbitrary",),
                                             vmem_limit_bytes=VMEM_LIMIT),
        name=name,
    )(x, norm_g.reshape(1, d), w_gate.astype(BF16), w_up.astype(BF16), w_down.astype(BF16))


def _head_mean_sq(x, low_half):
    x2 = x * x
    s_low = jnp.sum(jnp.where(low_half, x2, 0.0), axis=-1, keepdims=True)
    s_all = jnp.sum(x2, axis=-1, keepdims=True)
    return jnp.where(low_half, s_low, s_all - s_low) * (1.0 / HEAD_DIM)


def _split_kv_heads(x):
    swapped = pltpu.roll(x, HEAD_DIM, axis=1)
    low = lax.broadcasted_iota(jnp.int32, x.shape, 1) < HEAD_DIM
    zero = jnp.zeros_like(x)
    parts = (jnp.where(low, x, zero), jnp.where(low, zero, swapped),
             jnp.where(low, swapped, zero), jnp.where(low, zero, x))
    return tuple(p.astype(BF16) for p in parts)


def _mixer_kernel(h_ref, mixg_ref, w_in_ref, gbias_ref, poolw_ref, pscale_ref, wpo_ref,
                  qg_ref, kg_ref, sinks_ref, wao_ref, wout_ref, o_ref,
                  u_scr, xp_scr, q32_scr, q_scr, ka_scr, kb_scr, ve_scr, vo_scr, attn_scr, mixed_scr,
                  merged_scr, *, tiles_per_seq, d_model, pool_width, attn_width, kv_width):
    tm = h_ref.shape[0]
    i = pl.program_id(0)
    tile_in_seq = i % tiles_per_seq
    seq_start = tile_in_seq == 0
    lane128 = lax.broadcasted_iota(jnp.int32, (1, LANES), 1)
    low_half = lane128 < HEAD_DIM

    off_q = pool_width
    off_k = off_q + attn_width
    off_gp = off_k + 2 * kv_width
    off_ga = off_gp + d_model

    @pl.when(seq_start)
    def _():
        xp_scr[0:POOL_HALO, :] = jnp.zeros((POOL_HALO, pool_width), F32)
        zk = jnp.zeros((BLOCK, LANES), BF16)
        ones_e = jnp.broadcast_to(jnp.where(low_half, 1.0, 0.0).astype(BF16), (BLOCK + tm, LANES))
        ones_o = jnp.broadcast_to(jnp.where(low_half, 0.0, 1.0).astype(BF16), (BLOCK + tm, LANES))
        for kv in range(N_KV_HEADS):
            ka_scr[kv, 0:BLOCK, :] = zk
            kb_scr[kv, 0:BLOCK, :] = zk
            ve_scr[kv, 0:BLOCK, 0:LANES] = zk
            vo_scr[kv, 0:BLOCK, 0:LANES] = zk
            ve_scr[kv, :, LANES:2 * LANES] = ones_e
            vo_scr[kv, :, LANES:2 * LANES] = ones_o

    @pl.when(jnp.logical_not(seq_start))
    def _():
        xp_scr[0:POOL_HALO, :] = xp_scr[tm:tm + POOL_HALO, :]
        for kv in range(N_KV_HEADS):
            ka_scr[kv, 0:BLOCK, :] = ka_scr[kv, tm:tm + BLOCK, :]
            kb_scr[kv, 0:BLOCK, :] = kb_scr[kv, tm:tm + BLOCK, :]
            ve_scr[kv, 0:BLOCK, 0:LANES] = ve_scr[kv, tm:tm + BLOCK, 0:LANES]
            vo_scr[kv, 0:BLOCK, 0:LANES] = vo_scr[kv, tm:tm + BLOCK, 0:LANES]

    h = h_ref[...]
    u_scr[...] = _rmsnorm_rows(h, mixg_ref[...]).astype(BF16)

    xp_scr[POOL_HALO:POOL_HALO + tm, :] = _dot(u_scr[...], w_in_ref[:, 0:pool_width])

    q_gain = qg_ref[...] * (HEAD_DIM ** -0.5 * LOG2E)
    q32_scr[...] = _dot(u_scr[...], w_in_ref[:, off_q:off_q + attn_width])
    kv_proj = _dot_rows_split(u_scr[...], w_in_ref[:, off_k:off_k + 2 * kv_width])
    for c in range(attn_width // LANES):
        cols = slice(c * LANES, (c + 1) * LANES)
        q = q32_scr[:, cols]
        ms = _head_mean_sq(q, low_half)
        q_scr[:, cols] = (q * lax.rsqrt(ms + RMS_EPS) * q_gain).astype(BF16)

    k = kv_proj[:, 0:kv_width]
    k = k * lax.rsqrt(_head_mean_sq(k, low_half) + RMS_EPS) * kg_ref[...]
    ka0, kb0, ka1, kb1 = _split_kv_heads(k)
    ka_scr[0, BLOCK:BLOCK + tm, :] = ka0
    kb_scr[0, BLOCK:BLOCK + tm, :] = kb0
    ka_scr[1, BLOCK:BLOCK + tm, :] = ka1
    kb_scr[1, BLOCK:BLOCK + tm, :] = kb1
    va0, vb0, va1, vb1 = _split_kv_heads(kv_proj[:, kv_width:2 * kv_width])
    ve_scr[0, BLOCK:BLOCK + tm, 0:LANES] = va0
    vo_scr[0, BLOCK:BLOCK + tm, 0:LANES] = vb0
    ve_scr[1, BLOCK:BLOCK + tm, 0:LANES] = va1
    vo_scr[1, BLOCK:BLOCK + tm, 0:LANES] = vb1

    t = tile_in_seq * tm + lax.broadcasted_iota(jnp.int32, (tm, 1), 0)
    pooled = []
    for gi, w in enumerate(POOL_WINDOWS):
        cols = slice(gi * POOL_GROUP, (gi + 1) * POOL_GROUP)
        ext = xp_scr[:, cols]
        wsum = ext
        shift = 1
        while shift < w:
            wsum = wsum + pltpu.roll(wsum, shift, axis=0)
            shift *= 2
        count = jnp.minimum(t + 1, w).astype(F32)
        pooled.append((wsum[POOL_HALO:] / count - ext[POOL_HALO:]).astype(BF16))
    for c in range(len(POOL_WINDOWS) // 2):
        cols = slice(c * 2 * POOL_GROUP, (c + 1) * 2 * POOL_GROUP)
        pair = jnp.concatenate(pooled[2 * c:2 * c + 2], axis=1)
        mixed_scr[:, cols] = (_dot(pair, poolw_ref[c]) * pscale_ref[:, cols]).astype(BF16)

    key = lax.broadcasted_iota(jnp.int32, (2 * BLOCK, BLOCK), 0)
    qry = lax.broadcasted_iota(jnp.int32, (2 * BLOCK, BLOCK), 1)
    band = (key > qry) & (key <= qry + BLOCK)
    first_key = jnp.where(seq_start, BLOCK, 0)
    bias = jnp.where(band, 0.0, MASK_BIAS).astype(BF16)
    bias_first = jnp.where(band & (key >= first_key), 0.0, MASK_BIAS).astype(BF16)
    r4 = lax.broadcasted_iota(jnp.int32, (PAIRS_PER_KV * BLOCK, BLOCK), 0) % BLOCK
    c4 = lax.broadcasted_iota(jnp.int32, (PAIRS_PER_KV * BLOCK, BLOCK), 1)
    onehot = jnp.where(r4 == c4, 1.0, 0.0).astype(BF16)
    sinks2 = sinks_ref[...] * LOG2E

    for b in range(tm // BLOCK):
        rows = slice(b * BLOCK, (b + 1) * BLOCK)
        keys = slice(b * BLOCK, (b + 2) * BLOCK)
        key_bias = bias_first if b == 0 else bias
        for kv in range(N_KV_HEADS):
            base = kv * PAIRS_PER_KV
            qs = jnp.concatenate([q_scr[rows, (base + j) * LANES:(base + j + 1) * LANES]
                                  for j in range(PAIRS_PER_KV)], axis=0)
            qs = jnp.concatenate([qs, onehot], axis=1)
            k_even = jnp.concatenate([ka_scr[kv, keys, :], key_bias], axis=1)
            k_odd = jnp.concatenate([kb_scr[kv, keys, :], key_bias], axis=1)
            scores = (_dot_nt(qs, k_even), _dot_nt(qs, k_odd))
            probs = ([], [])
            sink_terms = []
            for j in range(PAIRS_PER_KV):
                m = []
                for par in range(2):
                    s = scores[par][j * BLOCK:(j + 1) * BLOCK]
                    m.append(jnp.max(s, axis=-1, keepdims=True))
                    probs[par].append(jnp.exp2(s - m[par]).astype(BF16))
                sink_terms.append(jnp.exp2(sinks2[base + j:base + j + 1, :]
                                           - jnp.where(low_half, m[0], m[1])))
            p = jnp.concatenate([jnp.concatenate(probs[0], axis=0),
                                 jnp.concatenate(probs[1], axis=0)], axis=1)
            v_stack = jnp.concatenate([ve_scr[kv, keys, :], vo_scr[kv, keys, :]], axis=0)
            acc = _dot_rows_split(p, v_stack)
            den = acc[:, LANES:2 * LANES] + jnp.concatenate(sink_terms, axis=0)
            out = (acc[:, 0:LANES] / den).astype(BF16)
            for j in range(PAIRS_PER_KV):
                attn_scr[rows, (base + j) * LANES:(base + j + 1) * LANES] = (
                    out[j * BLOCK:(j + 1) * BLOCK])

    for c in range(d_model // MIX_COL_CHUNK):
        cols = slice(c * MIX_COL_CHUNK, (c + 1) * MIX_COL_CHUNK)
        gp = jax.nn.sigmoid(
            _dot(u_scr[...], w_in_ref[:, off_gp + c * MIX_COL_CHUNK: off_gp + (c + 1) * MIX_COL_CHUNK])
            + gbias_ref[:, cols])
        ga = jax.nn.sigmoid(
            _dot(u_scr[...], w_in_ref[:, off_ga + c * MIX_COL_CHUNK: off_ga + (c + 1) * MIX_COL_CHUNK])
            + gbias_ref[:, d_model + c * MIX_COL_CHUNK: d_model + (c + 1) * MIX_COL_CHUNK])
        bp = _dot(mixed_scr[...], wpo_ref[:, cols])
        ba = _dot(attn_scr[...], wao_ref[:, cols])
        merged_scr[:, cols] = (gp * bp + ga * ba).astype(BF16)
    o_ref[...] = h + _dot(merged_scr[...], wout_ref[...])


def _pair_blockdiag(pool_w):
    g, n, _ = pool_w.shape
    z = jnp.zeros((n, n), pool_w.dtype)
    return jnp.stack([jnp.block([[pool_w[2 * c], z], [z, pool_w[2 * c + 1]]])
                      for c in range(g // 2)])


def _mixer(h, seq_len, mix_norm, w_in, pool_w, pool_scale, w_pool_out, q_norm, k_norm, sinks,
           w_attn_out, gate_bias, w_out):
    n, d = h.shape
    tm = MIX_TILE
    pool_width = w_pool_out.shape[0]
    attn_width = w_attn_out.shape[0]
    kv_width = N_KV_HEADS * HEAD_DIM
    in_width = w_in.shape[1]
    assert seq_len % tm == 0 and tm % BLOCK == 0
    assert pool_width == len(POOL_WINDOWS) * POOL_GROUP and attn_width == N_HEADS * HEAD_DIM
    assert in_width == pool_width + attn_width + 2 * kv_width + 2 * d and kv_width == LANES
    row_spec = pl.BlockSpec((tm, d), lambda i: (i, 0))
    kern = functools.partial(_mixer_kernel, tiles_per_seq=seq_len // tm, d_model=d,
                             pool_width=pool_width, attn_width=attn_width, kv_width=kv_width)
    pool_w2 = _pair_blockdiag(pool_w).astype(BF16)
    sink_rows = jnp.repeat(sinks.reshape(N_HEADS // 2, 2), HEAD_DIM, axis=1)
    return pl.pallas_call(
        kern,
        grid=(n // tm,),
        in_specs=[row_spec, _resident((1, d)), _resident((d, in_width)), _resident((1, 2 * d)),
                  _resident(pool_w2.shape), _resident((1, pool_width)), _resident((pool_width, d)),
                  _resident((1, LANES)), _resident((1, LANES)), _resident(sink_rows.shape),
                  _resident((attn_width, d)), _resident((d, d))],
        out_specs=row_spec,
        out_shape=jax.ShapeDtypeStruct((n, d), F32),
        scratch_shapes=[
            pltpu.VMEM((tm, d), BF16),
            pltpu.VMEM((POOL_HALO + tm, pool_width), F32),
            pltpu.VMEM((tm, attn_width), F32),
            pltpu.VMEM((tm, attn_width), BF16),
            pltpu.VMEM((N_KV_HEADS, BLOCK + tm, LANES), BF16),
            pltpu.VMEM((N_KV_HEADS, BLOCK + tm, LANES), BF16),
            pltpu.VMEM((N_KV_HEADS, BLOCK + tm, 2 * LANES), BF16),
            pltpu.VMEM((N_KV_HEADS, BLOCK + tm, 2 * LANES), BF16),
            pltpu.VMEM((tm, attn_width), BF16),
            pltpu.VMEM((tm, pool_width), BF16),
            pltpu.VMEM((tm, d), BF16),
        ],
        compiler_params=pltpu.CompilerParams(dimension_semantics=("arbitrary",),
                                             vmem_limit_bytes=VMEM_LIMIT),
        name="mixer",
    )(h, mix_norm.reshape(1, d), w_in.astype(BF16), gate_bias.reshape(1, 2 * d),
      pool_w2, pool_scale.reshape(1, pool_width), w_pool_out.astype(BF16),
      jnp.tile(q_norm, LANES // HEAD_DIM).reshape(1, LANES),
      jnp.tile(k_norm, LANES // HEAD_DIM).reshape(1, LANES), sink_rows,
      w_attn_out.astype(BF16), w_out.astype(BF16))


def kernel(x, ffn1_norm, ffn1_w_gate, ffn1_w_up, ffn1_w_down, mix_norm, w_in, pool_w, pool_scale,
           w_pool_out, q_norm, k_norm, sinks, w_attn_out, gate_bias, w_out, ffn2_norm, ffn2_w_gate,
           ffn2_w_up, ffn2_w_down):
    b, s, d = x.shape
    h = x.reshape(b * s, d)
    h = _ffn(h, ffn1_norm, ffn1_w_gate, ffn1_w_up, ffn1_w_down, name="ffn1")
    h = _mixer(h, s, mix_norm, w_in, pool_w, pool_scale, w_pool_out, q_norm, k_norm, sinks,
               w_attn_out, gate_bias, w_out)
    h = _ffn(h, ffn2_norm, ffn2_w_gate, ffn2_w_up, ffn2_w_down, name="ffn2")
    return h.reshape(b, s, d)
```

```python
import functools
import math

import jax
import jax.numpy as jnp
from jax import lax
from jax.experimental import pallas as pl
from jax.experimental.pallas import tpu as pltpu

F32 = jnp.float32
BF16 = jnp.bfloat16

RMS_EPS = 1e-6
HEAD_DIM = 64
N_HEADS = 16
N_KV_HEADS = 2
PAIRS_PER_KV = N_HEADS // N_KV_HEADS // 2
BLOCK = 128
POOL_WINDOWS = (2, 4, 8, 16)
POOL_GROUP = 128
POOL_HALO = 16
LANES = 128
BF16_SUBLANES = 16
V7X_VMEM_BYTES = 64 * 1024 * 1024
VMEM_LIMIT = V7X_VMEM_BYTES - 8 * 1024 * 1024
LOG2E = math.log2(math.e)
MASK_BIAS = -3.0e38

FFN_TILE = 1024
FFN_CHUNK = 256
MIX_TILE = 1024
MIX_COL_CHUNK = 256


def _dot(a, b):
    return jnp.dot(a, b, preferred_element_type=F32)


def _dot_rows_split(a, b):
    half = a.shape[0] // 2
    return jnp.concatenate([_dot(a[:half], b), _dot(a[half:], b)], axis=0)


def _dot_nt(a, b):
    return lax.dot_general(a, b, (((1,), (1,)), ((), ())), preferred_element_type=F32)


def _rmsnorm_rows(x, g):
    ms = jnp.mean(x * x, axis=-1, keepdims=True)
    return x * lax.rsqrt(ms + RMS_EPS) * g


def _cast_slabs(f32_refs, bf16_refs):
    for src, dst in zip(f32_refs, bf16_refs, strict=True):
        dst[...] = src[...].astype(BF16)


def _ffn_kernel(x_ref, g_ref, wg_ref, wu_ref, wd_ref, *rest, d_ff, chunk, n_cast):
    cast_in, o_ref, cast_out, acc_ref = (rest[:n_cast], rest[n_cast],
                                         rest[n_cast + 1:2 * n_cast + 1], rest[2 * n_cast + 1])
    x = x_ref[...]
    xn = _rmsnorm_rows(x, g_ref[...]).astype(BF16)
    for c in range(d_ff // chunk):
        cols = slice(c * chunk, (c + 1) * chunk)
        g = _dot(xn, wg_ref[:, cols])
        u = _dot(xn, wu_ref[:, cols])
        a = (g * jax.nn.sigmoid(g) * u).astype(BF16)
        d = _dot(a, wd_ref[cols, :])
        if c == 0:
            acc_ref[...] = d
        else:
            acc_ref[...] += d
    o_ref[...] = x + 0.5 * acc_ref[...]
    _cast_slabs(cast_in, cast_out)


def _resident(shape):
    return pl.BlockSpec(shape, lambda i: (0,) * len(shape), pipeline_mode=pl.Buffered(1))


def _slab_specs(weights, steps):
    in_specs, out_specs, out_shapes = [], [], []
    for w in weights:
        rows, cols = w.shape
        assert rows % (steps * BF16_SUBLANES) == 0
        in_specs.append(pl.BlockSpec((rows // steps, cols), lambda i: (i, 0)))
        out_specs.append(pl.BlockSpec((rows // steps, cols), lambda i: (i, 0)))
        out_shapes.append(jax.ShapeDtypeStruct(w.shape, BF16))
    return in_specs, out_specs, out_shapes


def _ffn(x, norm_g, w_gate, w_up, w_down, *, name, cast_for_later=()):
    n, d = x.shape
    d_ff = w_gate.shape[1]
    assert n % FFN_TILE == 0 and d_ff % FFN_CHUNK == 0
    steps = n // FFN_TILE
    row_spec = pl.BlockSpec((FFN_TILE, d), lambda i: (i, 0))
    cast_in, cast_out, cast_shapes = _slab_specs(cast_for_later, steps)
    return pl.pallas_call(
        functools.partial(_ffn_kernel, d_ff=d_ff, chunk=FFN_CHUNK, n_cast=len(cast_for_later)),
        grid=(steps,),
        in_specs=[row_spec, _resident((1, d)), _resident((d, d_ff)), _resident((d, d_ff)),
                  _resident((d_ff, d))] + cast_in,
        out_specs=[row_spec] + cast_out,
        out_shape=[jax.ShapeDtypeStruct((n, d), F32)] + cast_shapes,
        scratch_shapes=[pltpu.VMEM((FFN_TILE, d), F32)],
        compiler_params=pltpu.CompilerParams(dimension_semantics=("arbitrary",),
                                             vmem_limit_bytes=VMEM_LIMIT),
        name=name,
    )(x, norm_g.reshape(1, d), w_gate, w_up, w_down, *cast_for_later)


def _head_mean_sq(x, low_half):
    x2 = x * x
    s_low = jnp.sum(jnp.where(low_half, x2, 0.0), axis=-1, keepdims=True)
    s_all = jnp.sum(x2, axis=-1, keepdims=True)
    return jnp.where(low_half, s_low, s_all - s_low) * (1.0 / HEAD_DIM)


def _split_kv_heads(x):
    swapped = pltpu.roll(x, HEAD_DIM, axis=1)
    low = lax.broadcasted_iota(jnp.int32, x.shape, 1) < HEAD_DIM
    zero = jnp.zeros_like(x)
    parts = (jnp.where(low, x, zero), jnp.where(low, zero, swapped),
             jnp.where(low, swapped, zero), jnp.where(low, zero, x))
    return tuple(p.astype(BF16) for p in parts)


def _mixer_kernel(h_ref, mixg_ref, w_in_ref, gbias_ref, poolw_ref, pscale_ref, wpo_ref,
                  qg_ref, kg_ref, sinks_ref, wao_ref, wout_ref, *rest,
                  tiles_per_seq, d_model, pool_width, attn_width, kv_width, n_cast):
    cast_in, o_ref, cast_out = rest[:n_cast], rest[n_cast], rest[n_cast + 1:2 * n_cast + 1]
    (u_scr, xp_scr, q32_scr, q_scr, ka_scr, kb_scr, ve_scr, vo_scr, attn_scr, mixed_scr,
     merged_scr) = rest[2 * n_cast + 1:]
    tm = h_ref.shape[0]
    i = pl.program_id(0)
    tile_in_seq = i % tiles_per_seq
    seq_start = tile_in_seq == 0
    lane128 = lax.broadcasted_iota(jnp.int32, (1, LANES), 1)
    low_half = lane128 < HEAD_DIM

    off_q = pool_width
    off_k = off_q + attn_width
    off_gp = off_k + 2 * kv_width
    off_ga = off_gp + d_model

    @pl.when(seq_start)
    def _():
        xp_scr[0:POOL_HALO, :] = jnp.zeros((POOL_HALO, pool_width), F32)
        zk = jnp.zeros((BLOCK, LANES), BF16)
        ones_e = jnp.broadcast_to(jnp.where(low_half, 1.0, 0.0).astype(BF16), (BLOCK + tm, LANES))
        ones_o = jnp.broadcast_to(jnp.where(low_half, 0.0, 1.0).astype(BF16), (BLOCK + tm, LANES))
        for kv in range(N_KV_HEADS):
            ka_scr[kv, 0:BLOCK, :] = zk
            kb_scr[kv, 0:BLOCK, :] = zk
            ve_scr[kv, 0:BLOCK, 0:LANES] = zk
            vo_scr[kv, 0:BLOCK, 0:LANES] = zk
            ve_scr[kv, :, LANES:2 * LANES] = ones_e
            vo_scr[kv, :, LANES:2 * LANES] = ones_o

    @pl.when(jnp.logical_not(seq_start))
    def _():
        xp_scr[0:POOL_HALO, :] = xp_scr[tm:tm + POOL_HALO, :]
        for kv in range(N_KV_HEADS):
            ka_scr[kv, 0:BLOCK, :] = ka_scr[kv, tm:tm + BLOCK, :]
            kb_scr[kv, 0:BLOCK, :] = kb_scr[kv, tm:tm + BLOCK, :]
            ve_scr[kv, 0:BLOCK, 0:LANES] = ve_scr[kv, tm:tm + BLOCK, 0:LANES]
            vo_scr[kv, 0:BLOCK, 0:LANES] = vo_scr[kv, tm:tm + BLOCK, 0:LANES]

    h = h_ref[...]
    u_scr[...] = _rmsnorm_rows(h, mixg_ref[...]).astype(BF16)

    xp_scr[POOL_HALO:POOL_HALO + tm, :] = _dot(u_scr[...], w_in_ref[:, 0:pool_width])

    q_gain = qg_ref[...] * (HEAD_DIM ** -0.5 * LOG2E)
    q32_scr[...] = _dot(u_scr[...], w_in_ref[:, off_q:off_q + attn_width])
    kv_proj = _dot_rows_split(u_scr[...], w_in_ref[:, off_k:off_k + 2 * kv_width])
    for c in range(attn_width // LANES):
        cols = slice(c * LANES, (c + 1) * LANES)
        q = q32_scr[:, cols]
        ms = _head_mean_sq(q, low_half)
        q_scr[:, cols] = (q * lax.rsqrt(ms + RMS_EPS) * q_gain).astype(BF16)

    k = kv_proj[:, 0:kv_width]
    k = k * lax.rsqrt(_head_mean_sq(k, low_half) + RMS_EPS) * kg_ref[...]
    ka0, kb0, ka1, kb1 = _split_kv_heads(k)
    ka_scr[0, BLOCK:BLOCK + tm, :] = ka0
    kb_scr[0, BLOCK:BLOCK + tm, :] = kb0
    ka_scr[1, BLOCK:BLOCK + tm, :] = ka1
    kb_scr[1, BLOCK:BLOCK + tm, :] = kb1
    va0, vb0, va1, vb1 = _split_kv_heads(kv_proj[:, kv_width:2 * kv_width])
    ve_scr[0, BLOCK:BLOCK + tm, 0:LANES] = va0
    vo_scr[0, BLOCK:BLOCK + tm, 0:LANES] = vb0
    ve_scr[1, BLOCK:BLOCK + tm, 0:LANES] = va1
    vo_scr[1, BLOCK:BLOCK + tm, 0:LANES] = vb1

    t = tile_in_seq * tm + lax.broadcasted_iota(jnp.int32, (tm, 1), 0)
    pooled = []
    for gi, w in enumerate(POOL_WINDOWS):
        cols = slice(gi * POOL_GROUP, (gi + 1) * POOL_GROUP)
        ext = xp_scr[:, cols]
        wsum = ext
        shift = 1
        while shift < w:
            wsum = wsum + pltpu.roll(wsum, shift, axis=0)
            shift *= 2
        count = jnp.minimum(t + 1, w).astype(F32)
        pooled.append((wsum[POOL_HALO:] / count - ext[POOL_HALO:]).astype(BF16))
    for c in range(len(POOL_WINDOWS) // 2):
        cols = slice(c * 2 * POOL_GROUP, (c + 1) * 2 * POOL_GROUP)
        pair = jnp.concatenate(pooled[2 * c:2 * c + 2], axis=1)
        mixed_scr[:, cols] = (_dot(pair, poolw_ref[c]) * pscale_ref[:, cols]).astype(BF16)

    key = lax.broadcasted_iota(jnp.int32, (2 * BLOCK, BLOCK), 0)
    qry = lax.broadcasted_iota(jnp.int32, (2 * BLOCK, BLOCK), 1)
    band = (key > qry) & (key <= qry + BLOCK)
    first_key = jnp.where(seq_start, BLOCK, 0)
    bias = jnp.where(band, 0.0, MASK_BIAS).astype(BF16)
    bias_first = jnp.where(band & (key >= first_key), 0.0, MASK_BIAS).astype(BF16)
    r4 = lax.broadcasted_iota(jnp.int32, (PAIRS_PER_KV * BLOCK, BLOCK), 0) % BLOCK
    c4 = lax.broadcasted_iota(jnp.int32, (PAIRS_PER_KV * BLOCK, BLOCK), 1)
    onehot = jnp.where(r4 == c4, 1.0, 0.0).astype(BF16)
    sinks2 = sinks_ref[...] * LOG2E

    for b in range(tm // BLOCK):
        rows = slice(b * BLOCK, (b + 1) * BLOCK)
        keys = slice(b * BLOCK, (b + 2) * BLOCK)
        key_bias = bias_first if b == 0 else bias
        for kv in range(N_KV_HEADS):
            base = kv * PAIRS_PER_KV
            qs = jnp.concatenate([q_scr[rows, (base + j) * LANES:(base + j + 1) * LANES]
                                  for j in range(PAIRS_PER_KV)], axis=0)
            qs = jnp.concatenate([qs, onehot], axis=1)
            k_even = jnp.concatenate([ka_scr[kv, keys, :], key_bias], axis=1)
            k_odd = jnp.concatenate([kb_scr[kv, keys, :], key_bias], axis=1)
            scores = (_dot_nt(qs, k_even), _dot_nt(qs, k_odd))
            probs = ([], [])
            sink_terms = []
            for j in range(PAIRS_PER_KV):
                m = []
                for par in range(2):
                    s = scores[par][j * BLOCK:(j + 1) * BLOCK]
                    m.append(jnp.max(s, axis=-1, keepdims=True))
                    probs[par].append(jnp.exp2(s - m[par]).astype(BF16))
                sink_terms.append(jnp.exp2(sinks2[base + j:base + j + 1, :]
                                           - jnp.where(low_half, m[0], m[1])))
            p = jnp.concatenate([jnp.concatenate(probs[0], axis=0),
                                 jnp.concatenate(probs[1], axis=0)], axis=1)
            v_stack = jnp.concatenate([ve_scr[kv, keys, :], vo_scr[kv, keys, :]], axis=0)
            acc = _dot_rows_split(p, v_stack)
            den = acc[:, LANES:2 * LANES] + jnp.concatenate(sink_terms, axis=0)
            out = (acc[:, 0:LANES] / den).astype(BF16)
            for j in range(PAIRS_PER_KV):
                attn_scr[rows, (base + j) * LANES:(base + j + 1) * LANES] = (
                    out[j * BLOCK:(j + 1) * BLOCK])

    for c in range(d_model // MIX_COL_CHUNK):
        cols = slice(c * MIX_COL_CHUNK, (c + 1) * MIX_COL_CHUNK)
        gp = jax.nn.sigmoid(
            _dot(u_scr[...], w_in_ref[:, off_gp + c * MIX_COL_CHUNK: off_gp + (c + 1) * MIX_COL_CHUNK])
            + gbias_ref[:, cols])
        ga = jax.nn.sigmoid(
            _dot(u_scr[...], w_in_ref[:, off_ga + c * MIX_COL_CHUNK: off_ga + (c + 1) * MIX_COL_CHUNK])
            + gbias_ref[:, d_model + c * MIX_COL_CHUNK: d_model + (c + 1) * MIX_COL_CHUNK])
        bp = _dot(mixed_scr[...], wpo_ref[:, cols])
        ba = _dot(attn_scr[...], wao_ref[:, cols])
        merged_scr[:, cols] = (gp * bp + ga * ba).astype(BF16)
    o_ref[...] = h + _dot(merged_scr[...], wout_ref[...])
    _cast_slabs(cast_in, cast_out)


def _pair_blockdiag(pool_w):
    g, n, _ = pool_w.shape
    z = jnp.zeros((n, n), pool_w.dtype)
    return jnp.stack([jnp.block([[pool_w[2 * c], z], [z, pool_w[2 * c + 1]]])
                      for c in range(g // 2)])


def _mixer(h, seq_len, mix_norm, w_in, pool_w, pool_scale, w_pool_out, q_norm, k_norm, sinks,
           w_attn_out, gate_bias, w_out, *, cast_for_later=()):
    n, d = h.shape
    tm = MIX_TILE
    pool_width = w_pool_out.shape[0]
    attn_width = w_attn_out.shape[0]
    kv_width = N_KV_HEADS * HEAD_DIM
    in_width = w_in.shape[1]
    assert seq_len % tm == 0 and tm % BLOCK == 0
    assert pool_width == len(POOL_WINDOWS) * POOL_GROUP and attn_width == N_HEADS * HEAD_DIM
    assert in_width == pool_width + attn_width + 2 * kv_width + 2 * d and kv_width == LANES
    row_spec = pl.BlockSpec((tm, d), lambda i: (i, 0))
    cast_in, cast_out, cast_shapes = _slab_specs(cast_for_later, n // tm)
    kern = functools.partial(_mixer_kernel, tiles_per_seq=seq_len // tm, d_model=d,
                             pool_width=pool_width, attn_width=attn_width, kv_width=kv_width,
                             n_cast=len(cast_for_later))
    pool_w2 = _pair_blockdiag(pool_w).astype(BF16)
    sink_rows = jnp.repeat(sinks.reshape(N_HEADS // 2, 2), HEAD_DIM, axis=1)
    return pl.pallas_call(
        kern,
        grid=(n // tm,),
        in_specs=[row_spec, _resident((1, d)), _resident((d, in_width)), _resident((1, 2 * d)),
                  _resident(pool_w2.shape), _resident((1, pool_width)), _resident((pool_width, d)),
                  _resident((1, LANES)), _resident((1, LANES)), _resident(sink_rows.shape),
                  _resident((attn_width, d)), _resident((d, d))] + cast_in,
        out_specs=[row_spec] + cast_out,
        out_shape=[jax.ShapeDtypeStruct((n, d), F32)] + cast_shapes,
        scratch_shapes=[
            pltpu.VMEM((tm, d), BF16),
            pltpu.VMEM((POOL_HALO + tm, pool_width), F32),
            pltpu.VMEM((tm, attn_width), F32),
            pltpu.VMEM((tm, attn_width), BF16),
            pltpu.VMEM((N_KV_HEADS, BLOCK + tm, LANES), BF16),
            pltpu.VMEM((N_KV_HEADS, BLOCK + tm, LANES), BF16),
            pltpu.VMEM((N_KV_HEADS, BLOCK + tm, 2 * LANES), BF16),
            pltpu.VMEM((N_KV_HEADS, BLOCK + tm, 2 * LANES), BF16),
            pltpu.VMEM((tm, attn_width), BF16),
            pltpu.VMEM((tm, pool_width), BF16),
            pltpu.VMEM((tm, d), BF16),
        ],
        compiler_params=pltpu.CompilerParams(dimension_semantics=("arbitrary",),
                                             vmem_limit_bytes=VMEM_LIMIT),
        name="mixer",
    )(h, mix_norm.reshape(1, d), w_in, gate_bias.reshape(1, 2 * d),
      pool_w2, pool_scale.reshape(1, pool_width), w_pool_out,
      jnp.tile(q_norm, LANES // HEAD_DIM).reshape(1, LANES),
      jnp.tile(k_norm, LANES // HEAD_DIM).reshape(1, LANES), sink_rows,
      w_attn_out, w_out, *cast_for_later)


def kernel(x, ffn1_norm, ffn1_w_gate, ffn1_w_up, ffn1_w_down, mix_norm, w_in, pool_w, pool_scale,
           w_pool_out, q_norm, k_norm, sinks, w_attn_out, gate_bias, w_out, ffn2_norm, ffn2_w_gate,
           ffn2_w_up, ffn2_w_down):
    b, s, d = x.shape
    h = x.reshape(b * s, d)
    h, w_in_b, w_pool_out_b, w_attn_out_b, w_out_b = _ffn(
        h, ffn1_norm, ffn1_w_gate.astype(BF16), ffn1_w_up.astype(BF16), ffn1_w_down.astype(BF16),
        name="ffn1", cast_for_later=(w_in, w_pool_out, w_attn_out, w_out))
    h, w_gate_b, w_up_b, w_down_b = _mixer(
        h, s, mix_norm, w_in_b, pool_w, pool_scale, w_pool_out_b, q_norm, k_norm, sinks,
        w_attn_out_b, gate_bias, w_out_b, cast_for_later=(ffn2_w_gate, ffn2_w_up, ffn2_w_down))
    (h,) = _ffn(h, ffn2_norm, w_gate_b, w_up_b, w_down_b, name="ffn2")
    return h.reshape(b, s, d)
```

```python
import functools
import math

import jax
import jax.numpy as jnp
from jax import lax
from jax.experimental import pallas as pl
from jax.experimental.pallas import tpu as pltpu

F32 = jnp.float32
BF16 = jnp.bfloat16

RMS_EPS = 1e-6
HEAD_DIM = 64
N_HEADS = 16
N_KV_HEADS = 2
PAIRS_PER_KV = N_HEADS // N_KV_HEADS // 2
BLOCK = 128
POOL_WINDOWS = (2, 4, 8, 16)
POOL_GROUP = 128
POOL_HALO = 16
LANES = 128
BF16_SUBLANES = 16
V7X_VMEM_BYTES = 64 * 1024 * 1024
VMEM_LIMIT = V7X_VMEM_BYTES - 8 * 1024 * 1024
LOG2E = math.log2(math.e)
MASK_BIAS = -3.0e38

FFN_TILE = 1024
FFN_CHUNK = 256
MIX_TILE = 1024
MIX_COL_CHUNK = 256


def _dot(a, b):
    return jnp.dot(a, b, preferred_element_type=F32)


def _dot_rows_split(a, b):
    half = a.shape[0] // 2
    return jnp.concatenate([_dot(a[:half], b), _dot(a[half:], b)], axis=0)


def _dot_nt(a, b):
    return lax.dot_general(a, b, (((1,), (1,)), ((), ())), preferred_element_type=F32)


def _rmsnorm_rows(x, g):
    ms = jnp.mean(x * x, axis=-1, keepdims=True)
    return x * lax.rsqrt(ms + RMS_EPS) * g


def _cast_slabs(f32_refs, bf16_refs):
    for src, dst in zip(f32_refs, bf16_refs, strict=True):
        dst[...] = src[...].astype(BF16)


def _ffn_kernel(x_ref, g_ref, wg_hbm, wu_hbm, wd_hbm, *rest, d_ff, chunk, n_cast):
    cast_in, o_ref, cast_out = rest[:n_cast], rest[n_cast], rest[n_cast + 1:2 * n_cast + 1]
    wg_v, wu_v, wd_v, stage_g, stage_u, stage_d, sems, acc_ref = rest[2 * n_cast + 1:]
    n_chunks = d_ff // chunk

    def chunk_copies(c):
        slot = c % 2
        cols = pl.ds(c * chunk, chunk)
        return (pltpu.make_async_copy(wg_hbm.at[:, cols], stage_g.at[slot], sems.at[0, slot]),
                pltpu.make_async_copy(wu_hbm.at[:, cols], stage_u.at[slot], sems.at[1, slot]),
                pltpu.make_async_copy(wd_hbm.at[cols, :], stage_d.at[slot], sems.at[2, slot]))

    def tile(stream_weights):
        if stream_weights:
            for cp in chunk_copies(0):
                cp.start()
        x = x_ref[...]
        xn = _rmsnorm_rows(x, g_ref[...]).astype(BF16)
        for c in range(n_chunks):
            cols = slice(c * chunk, (c + 1) * chunk)
            if stream_weights:
                if c + 1 < n_chunks:
                    for cp in chunk_copies(c + 1):
                        cp.start()
                for cp in chunk_copies(c):
                    cp.wait()
                wg_v[:, cols] = stage_g[c % 2].astype(BF16)
                wu_v[:, cols] = stage_u[c % 2].astype(BF16)
                wd_v[cols, :] = stage_d[c % 2].astype(BF16)
            g = _dot(xn, wg_v[:, cols])
            u = _dot(xn, wu_v[:, cols])
            a = (g * jax.nn.sigmoid(g) * u).astype(BF16)
            d = _dot(a, wd_v[cols, :])
            if c == 0:
                acc_ref[...] = d
            else:
                acc_ref[...] += d
        o_ref[...] = x + 0.5 * acc_ref[...]

    first = pl.program_id(0) == 0
    pl.when(first)(functools.partial(tile, True))
    pl.when(jnp.logical_not(first))(functools.partial(tile, False))
    _cast_slabs(cast_in, cast_out)


def _resident(shape):
    return pl.BlockSpec(shape, lambda i: (0,) * len(shape), pipeline_mode=pl.Buffered(1))


def _slab_specs(weights, steps):
    in_specs, out_specs, out_shapes = [], [], []
    for w in weights:
        rows, cols = w.shape
        assert rows % (steps * BF16_SUBLANES) == 0
        in_specs.append(pl.BlockSpec((rows // steps, cols), lambda i: (i, 0)))
        out_specs.append(pl.BlockSpec((rows // steps, cols), lambda i: (i, 0)))
        out_shapes.append(jax.ShapeDtypeStruct(w.shape, BF16))
    return in_specs, out_specs, out_shapes


def _ffn(x, norm_g, w_gate, w_up, w_down, *, name, cast_for_later=()):
    n, d = x.shape
    d_ff = w_gate.shape[1]
    assert n % FFN_TILE == 0 and d_ff % FFN_CHUNK == 0
    steps = n // FFN_TILE
    row_spec = pl.BlockSpec((FFN_TILE, d), lambda i: (i, 0))
    in_hbm = pl.BlockSpec(memory_space=pl.ANY)
    cast_in, cast_out, cast_shapes = _slab_specs(cast_for_later, steps)
    return pl.pallas_call(
        functools.partial(_ffn_kernel, d_ff=d_ff, chunk=FFN_CHUNK, n_cast=len(cast_for_later)),
        grid=(steps,),
        in_specs=[row_spec, _resident((1, d)), in_hbm, in_hbm, in_hbm] + cast_in,
        out_specs=[row_spec] + cast_out,
        out_shape=[jax.ShapeDtypeStruct((n, d), F32)] + cast_shapes,
        scratch_shapes=[
            pltpu.VMEM((d, d_ff), BF16),
            pltpu.VMEM((d, d_ff), BF16),
            pltpu.VMEM((d_ff, d), BF16),
            pltpu.VMEM((2, d, FFN_CHUNK), F32),
            pltpu.VMEM((2, d, FFN_CHUNK), F32),
            pltpu.VMEM((2, FFN_CHUNK, d), F32),
            pltpu.SemaphoreType.DMA((3, 2)),
            pltpu.VMEM((FFN_TILE, d), F32),
        ],
        compiler_params=pltpu.CompilerParams(dimension_semantics=("arbitrary",),
                                             vmem_limit_bytes=VMEM_LIMIT),
        name=name,
    )(x, norm_g.reshape(1, d), w_gate, w_up, w_down, *cast_for_later)


def _head_mean_sq(x, low_half):
    x2 = x * x
    s_low = jnp.sum(jnp.where(low_half, x2, 0.0), axis=-1, keepdims=True)
    s_all = jnp.sum(x2, axis=-1, keepdims=True)
    return jnp.where(low_half, s_low, s_all - s_low) * (1.0 / HEAD_DIM)


def _split_kv_heads(x):
    swapped = pltpu.roll(x, HEAD_DIM, axis=1)
    low = lax.broadcasted_iota(jnp.int32, x.shape, 1) < HEAD_DIM
    zero = jnp.zeros_like(x)
    parts = (jnp.where(low, x, zero), jnp.where(low, zero, swapped),
             jnp.where(low, swapped, zero), jnp.where(low, zero, x))
    return tuple(p.astype(BF16) for p in parts)


def _mixer_kernel(h_ref, mixg_ref, w_in_ref, gbias_ref, poolw_ref, pscale_ref, wpo_ref,
                  qg_ref, kg_ref, sinks_ref, wao_ref, wout_ref, *rest,
                  tiles_per_seq, d_model, pool_width, attn_width, kv_width, n_cast):
    cast_in, o_ref, cast_out = rest[:n_cast], rest[n_cast], rest[n_cast + 1:2 * n_cast + 1]
    (u_scr, xp_scr, q32_scr, q_scr, ka_scr, kb_scr, ve_scr, vo_scr, attn_scr, mixed_scr,
     merged_scr) = rest[2 * n_cast + 1:]
    tm = h_ref.shape[0]
    i = pl.program_id(0)
    tile_in_seq = i % tiles_per_seq
    seq_start = tile_in_seq == 0
    lane128 = lax.broadcasted_iota(jnp.int32, (1, LANES), 1)
    low_half = lane128 < HEAD_DIM

    off_q = pool_width
    off_k = off_q + attn_width
    off_gp = off_k + 2 * kv_width
    off_ga = off_gp + d_model

    @pl.when(seq_start)
    def _():
        xp_scr[0:POOL_HALO, :] = jnp.zeros((POOL_HALO, pool_width), F32)
        zk = jnp.zeros((BLOCK, LANES), BF16)
        ones_e = jnp.broadcast_to(jnp.where(low_half, 1.0, 0.0).astype(BF16), (BLOCK + tm, LANES))
        ones_o = jnp.broadcast_to(jnp.where(low_half, 0.0, 1.0).astype(BF16), (BLOCK + tm, LANES))
        for kv in range(N_KV_HEADS):
            ka_scr[kv, 0:BLOCK, :] = zk
            kb_scr[kv, 0:BLOCK, :] = zk
            ve_scr[kv, 0:BLOCK, 0:LANES] = zk
            vo_scr[kv, 0:BLOCK, 0:LANES] = zk
            ve_scr[kv, :, LANES:2 * LANES] = ones_e
            vo_scr[kv, :, LANES:2 * LANES] = ones_o

    @pl.when(jnp.logical_not(seq_start))
    def _():
        xp_scr[0:POOL_HALO, :] = xp_scr[tm:tm + POOL_HALO, :]
        for kv in range(N_KV_HEADS):
            ka_scr[kv, 0:BLOCK, :] = ka_scr[kv, tm:tm + BLOCK, :]
            kb_scr[kv, 0:BLOCK, :] = kb_scr[kv, tm:tm + BLOCK, :]
            ve_scr[kv, 0:BLOCK, 0:LANES] = ve_scr[kv, tm:tm + BLOCK, 0:LANES]
            vo_scr[kv, 0:BLOCK, 0:LANES] = vo_scr[kv, tm:tm + BLOCK, 0:LANES]

    h = h_ref[...]
    u_scr[...] = _rmsnorm_rows(h, mixg_ref[...]).astype(BF16)

    xp_scr[POOL_HALO:POOL_HALO + tm, :] = _dot(u_scr[...], w_in_ref[:, 0:pool_width])

    q_gain = qg_ref[...] * (HEAD_DIM ** -0.5 * LOG2E)
    q32_scr[...] = _dot(u_scr[...], w_in_ref[:, off_q:off_q + attn_width])
    kv_proj = _dot_rows_split(u_scr[...], w_in_ref[:, off_k:off_k + 2 * kv_width])
    for c in range(attn_width // LANES):
        cols = slice(c * LANES, (c + 1) * LANES)
        q = q32_scr[:, cols]
        ms = _head_mean_sq(q, low_half)
        q_scr[:, cols] = (q * lax.rsqrt(ms + RMS_EPS) * q_gain).astype(BF16)

    k = kv_proj[:, 0:kv_width]
    k = k * lax.rsqrt(_head_mean_sq(k, low_half) + RMS_EPS) * kg_ref[...]
    ka0, kb0, ka1, kb1 = _split_kv_heads(k)
    ka_scr[0, BLOCK:BLOCK + tm, :] = ka0
    kb_scr[0, BLOCK:BLOCK + tm, :] = kb0
    ka_scr[1, BLOCK:BLOCK + tm, :] = ka1
    kb_scr[1, BLOCK:BLOCK + tm, :] = kb1
    va0, vb0, va1, vb1 = _split_kv_heads(kv_proj[:, kv_width:2 * kv_width])
    ve_scr[0, BLOCK:BLOCK + tm, 0:LANES] = va0
    vo_scr[0, BLOCK:BLOCK + tm, 0:LANES] = vb0
    ve_scr[1, BLOCK:BLOCK + tm, 0:LANES] = va1
    vo_scr[1, BLOCK:BLOCK + tm, 0:LANES] = vb1

    t = tile_in_seq * tm + lax.broadcasted_iota(jnp.int32, (tm, 1), 0)
    pooled = []
    for gi, w in enumerate(POOL_WINDOWS):
        cols = slice(gi * POOL_GROUP, (gi + 1) * POOL_GROUP)
        ext = xp_scr[:, cols]
        wsum = ext
        shift = 1
        while shift < w:
            wsum = wsum + pltpu.roll(wsum, shift, axis=0)
            shift *= 2
        count = jnp.minimum(t + 1, w).astype(F32)
        pooled.append((wsum[POOL_HALO:] / count - ext[POOL_HALO:]).astype(BF16))
    for c in range(len(POOL_WINDOWS) // 2):
        cols = slice(c * 2 * POOL_GROUP, (c + 1) * 2 * POOL_GROUP)
        pair = jnp.concatenate(pooled[2 * c:2 * c + 2], axis=1)
        mixed_scr[:, cols] = (_dot(pair, poolw_ref[c]) * pscale_ref[:, cols]).astype(BF16)

    key = lax.broadcasted_iota(jnp.int32, (2 * BLOCK, BLOCK), 0)
    qry = lax.broadcasted_iota(jnp.int32, (2 * BLOCK, BLOCK), 1)
    band = (key > qry) & (key <= qry + BLOCK)
    first_key = jnp.where(seq_start, BLOCK, 0)
    bias = jnp.where(band, 0.0, MASK_BIAS).astype(BF16)
    bias_first = jnp.where(band & (key >= first_key), 0.0, MASK_BIAS).astype(BF16)
    r4 = lax.broadcasted_iota(jnp.int32, (PAIRS_PER_KV * BLOCK, BLOCK), 0) % BLOCK
    c4 = lax.broadcasted_iota(jnp.int32, (PAIRS_PER_KV * BLOCK, BLOCK), 1)
    onehot = jnp.where(r4 == c4, 1.0, 0.0).astype(BF16)
    sinks2 = sinks_ref[...] * LOG2E

    for b in range(tm // BLOCK):
        rows = slice(b * BLOCK, (b + 1) * BLOCK)
        keys = slice(b * BLOCK, (b + 2) * BLOCK)
        key_bias = bias_first if b == 0 else bias
        for kv in range(N_KV_HEADS):
            base = kv * PAIRS_PER_KV
            qs = jnp.concatenate([q_scr[rows, (base + j) * LANES:(base + j + 1) * LANES]
                                  for j in range(PAIRS_PER_KV)], axis=0)
            qs = jnp.concatenate([qs, onehot], axis=1)
            k_even = jnp.concatenate([ka_scr[kv, keys, :], key_bias], axis=1)
            k_odd = jnp.concatenate([kb_scr[kv, keys, :], key_bias], axis=1)
            scores = (_dot_nt(qs, k_even), _dot_nt(qs, k_odd))
            probs = ([], [])
            sink_terms = []
            for j in range(PAIRS_PER_KV):
                m = []
                for par in range(2):
                    s = scores[par][j * BLOCK:(j + 1) * BLOCK]
                    m.append(jnp.max(s, axis=-1, keepdims=True))
                    probs[par].append(jnp.exp2(s - m[par]).astype(BF16))
                sink_terms.append(jnp.exp2(sinks2[base + j:base + j + 1, :]
                                           - jnp.where(low_half, m[0], m[1])))
            p = jnp.concatenate([jnp.concatenate(probs[0], axis=0),
                                 jnp.concatenate(probs[1], axis=0)], axis=1)
            v_stack = jnp.concatenate([ve_scr[kv, keys, :], vo_scr[kv, keys, :]], axis=0)
            acc = _dot_rows_split(p, v_stack)
            den = acc[:, LANES:2 * LANES] + jnp.concatenate(sink_terms, axis=0)
            out = (acc[:, 0:LANES] / den).astype(BF16)
            for j in range(PAIRS_PER_KV):
                attn_scr[rows, (base + j) * LANES:(base + j + 1) * LANES] = (
                    out[j * BLOCK:(j + 1) * BLOCK])

    for c in range(d_model // MIX_COL_CHUNK):
        cols = slice(c * MIX_COL_CHUNK, (c + 1) * MIX_COL_CHUNK)
        gp = jax.nn.sigmoid(
            _dot(u_scr[...], w_in_ref[:, off_gp + c * MIX_COL_CHUNK: off_gp + (c + 1) * MIX_COL_CHUNK])
            + gbias_ref[:, cols])
        ga = jax.nn.sigmoid(
            _dot(u_scr[...], w_in_ref[:, off_ga + c * MIX_COL_CHUNK: off_ga + (c + 1) * MIX_COL_CHUNK])
            + gbias_ref[:, d_model + c * MIX_COL_CHUNK: d_model + (c + 1) * MIX_COL_CHUNK])
        bp = _dot(mixed_scr[...], wpo_ref[:, cols])
        ba = _dot(attn_scr[...], wao_ref[:, cols])
        merged_scr[:, cols] = (gp * bp + ga * ba).astype(BF16)
    o_ref[...] = h + _dot(merged_scr[...], wout_ref[...])
    _cast_slabs(cast_in, cast_out)


def _pair_blockdiag(pool_w):
    g, n, _ = pool_w.shape
    z = jnp.zeros((n, n), pool_w.dtype)
    return jnp.stack([jnp.block([[pool_w[2 * c], z], [z, pool_w[2 * c + 1]]])
                      for c in range(g // 2)])


def _mixer(h, seq_len, mix_norm, w_in, pool_w, pool_scale, w_pool_out, q_norm, k_norm, sinks,
           w_attn_out, gate_bias, w_out, *, cast_for_later=()):
    n, d = h.shape
    tm = MIX_TILE
    pool_width = w_pool_out.shape[0]
    attn_width = w_attn_out.shape[0]
    kv_width = N_KV_HEADS * HEAD_DIM
    in_width = w_in.shape[1]
    assert seq_len % tm == 0 and tm % BLOCK == 0
    assert pool_width == len(POOL_WINDOWS) * POOL_GROUP and attn_width == N_HEADS * HEAD_DIM
    assert in_width == pool_width + attn_width + 2 * kv_width + 2 * d and kv_width == LANES
    row_spec = pl.BlockSpec((tm, d), lambda i: (i, 0))
    cast_in, cast_out, cast_shapes = _slab_specs(cast_for_later, n // tm)
    kern = functools.partial(_mixer_kernel, tiles_per_seq=seq_len // tm, d_model=d,
                             pool_width=pool_width, attn_width=attn_width, kv_width=kv_width,
                             n_cast=len(cast_for_later))
    pool_w2 = _pair_blockdiag(pool_w).astype(BF16)
    sink_rows = jnp.repeat(sinks.reshape(N_HEADS // 2, 2), HEAD_DIM, axis=1)
    return pl.pallas_call(
        kern,
        grid=(n // tm,),
        in_specs=[row_spec, _resident((1, d)), _resident((d, in_width)), _resident((1, 2 * d)),
                  _resident(pool_w2.shape), _resident((1, pool_width)), _resident((pool_width, d)),
                  _resident((1, LANES)), _resident((1, LANES)), _resident(sink_rows.shape),
                  _resident((attn_width, d)), _resident((d, d))] + cast_in,
        out_specs=[row_spec] + cast_out,
        out_shape=[jax.ShapeDtypeStruct((n, d), F32)] + cast_shapes,
        scratch_shapes=[
            pltpu.VMEM((tm, d), BF16),
            pltpu.VMEM((POOL_HALO + tm, pool_width), F32),
            pltpu.VMEM((tm, attn_width), F32),
            pltpu.VMEM((tm, attn_width), BF16),
            pltpu.VMEM((N_KV_HEADS, BLOCK + tm, LANES), BF16),
            pltpu.VMEM((N_KV_HEADS, BLOCK + tm, LANES), BF16),
            pltpu.VMEM((N_KV_HEADS, BLOCK + tm, 2 * LANES), BF16),
            pltpu.VMEM((N_KV_HEADS, BLOCK + tm, 2 * LANES), BF16),
            pltpu.VMEM((tm, attn_width), BF16),
            pltpu.VMEM((tm, pool_width), BF16),
            pltpu.VMEM((tm, d), BF16),
        ],
        compiler_params=pltpu.CompilerParams(dimension_semantics=("arbitrary",),
                                             vmem_limit_bytes=VMEM_LIMIT),
        name="mixer",
    )(h, mix_norm.reshape(1, d), w_in, gate_bias.reshape(1, 2 * d),
      pool_w2, pool_scale.reshape(1, pool_width), w_pool_out,
      jnp.tile(q_norm, LANES // HEAD_DIM).reshape(1, LANES),
      jnp.tile(k_norm, LANES // HEAD_DIM).reshape(1, LANES), sink_rows,
      w_attn_out, w_out, *cast_for_later)


def kernel(x, ffn1_norm, ffn1_w_gate, ffn1_w_up, ffn1_w_down, mix_norm, w_in, pool_w, pool_scale,
           w_pool_out, q_norm, k_norm, sinks, w_attn_out, gate_bias, w_out, ffn2_norm, ffn2_w_gate,
           ffn2_w_up, ffn2_w_down):
    b, s, d = x.shape
    h = x.reshape(b * s, d)
    h, w_in_b, w_pool_out_b, w_attn_out_b, w_out_b = _ffn(
        h, ffn1_norm, ffn1_w_gate, ffn1_w_up, ffn1_w_down, name="ffn1",
        cast_for_later=(w_in, w_pool_out, w_attn_out, w_out))
    (h,) = _mixer(h, s, mix_norm, w_in_b, pool_w, pool_scale, w_pool_out_b, q_norm, k_norm, sinks,
                  w_attn_out_b, gate_bias, w_out_b)
    (h,) = _ffn(h, ffn2_norm, ffn2_w_gate, ffn2_w_up, ffn2_w_down, name="ffn2")
    return h.reshape(b, s, d)
```

```python
import functools
import math

import jax
import jax.numpy as jnp
from jax import lax
from jax.experimental import pallas as pl
from jax.experimental.pallas import tpu as pltpu

F32 = jnp.float32
BF16 = jnp.bfloat16

RMS_EPS = 1e-6
HEAD_DIM = 64
N_HEADS = 16
N_KV_HEADS = 2
PAIRS_PER_KV = N_HEADS // N_KV_HEADS // 2
BLOCK = 128
POOL_WINDOWS = (2, 4, 8, 16)
POOL_GROUP = 128
POOL_HALO = 16
LANES = 128
BF16_SUBLANES = 16
V7X_VMEM_BYTES = 64 * 1024 * 1024
VMEM_LIMIT = V7X_VMEM_BYTES - 8 * 1024 * 1024
LOG2E = math.log2(math.e)
MASK_BIAS = -3.0e38

FFN_TILE = 1024
FFN_CHUNK = 256
FFN_WEIGHT_SLABS = 16
FFN_STAGE_SLOTS = 3
MIX_TILE = 1024
MIX_COL_CHUNK = 256


def _dot(a, b):
    return jnp.dot(a, b, preferred_element_type=F32)


def _dot_rows_split(a, b):
    half = a.shape[0] // 2
    return jnp.concatenate([_dot(a[:half], b), _dot(a[half:], b)], axis=0)


def _dot_nt(a, b):
    return lax.dot_general(a, b, (((1,), (1,)), ((), ())), preferred_element_type=F32)


def _rmsnorm_rows(x, g):
    ms = jnp.mean(x * x, axis=-1, keepdims=True)
    return x * lax.rsqrt(ms + RMS_EPS) * g


def _cast_slabs(f32_refs, bf16_refs):
    for src, dst in zip(f32_refs, bf16_refs, strict=True):
        dst[...] = src[...].astype(BF16)


def _ffn_kernel(x_ref, g_ref, wg_hbm, wu_hbm, wd_hbm, *rest, d_ff, chunk, n_cast, n_slabs):
    cast_in, o_ref, cast_out = rest[:n_cast], rest[n_cast], rest[n_cast + 1:2 * n_cast + 1]
    wg_v, wu_v, wd_v, stage_g, stage_u, stage_d, sems, acc_ref = rest[2 * n_cast + 1:]
    n_slots = stage_g.shape[0]
    streams = ((wg_hbm, stage_g, wg_v), (wu_hbm, stage_u, wu_v), (wd_hbm, stage_d, wd_v))

    def slab_copies(s):
        slot = s % n_slots
        return [pltpu.make_async_copy(hbm.at[pl.ds(s * stage.shape[1], stage.shape[1]), :],
                                      stage.at[slot], sems.at[k, slot])
                for k, (hbm, stage, _) in enumerate(streams)]

    @pl.when(pl.program_id(0) == 0)
    def _():
        for s in range(min(n_slots, n_slabs)):
            for cp in slab_copies(s):
                cp.start()
        for s in range(n_slabs):
            for cp in slab_copies(s):
                cp.wait()
            for _, stage, dst in streams:
                rows = stage.shape[1]
                dst[s * rows:(s + 1) * rows, :] = stage[s % n_slots].astype(BF16)
            if s + n_slots < n_slabs:
                for cp in slab_copies(s + n_slots):
                    cp.start()

    x = x_ref[...]
    xn = _rmsnorm_rows(x, g_ref[...]).astype(BF16)
    for c in range(d_ff // chunk):
        cols = slice(c * chunk, (c + 1) * chunk)
        g = _dot(xn, wg_v[:, cols])
        u = _dot(xn, wu_v[:, cols])
        a = (g * jax.nn.sigmoid(g) * u).astype(BF16)
        d = _dot(a, wd_v[cols, :])
        if c == 0:
            acc_ref[...] = d
        else:
            acc_ref[...] += d
    o_ref[...] = x + 0.5 * acc_ref[...]
    _cast_slabs(cast_in, cast_out)


def _resident(shape):
    return pl.BlockSpec(shape, lambda i: (0,) * len(shape), pipeline_mode=pl.Buffered(1))


def _slab_specs(weights, steps):
    in_specs, out_specs, out_shapes = [], [], []
    for w in weights:
        rows, cols = w.shape
        assert rows % (steps * BF16_SUBLANES) == 0
        in_specs.append(pl.BlockSpec((rows // steps, cols), lambda i: (i, 0)))
        out_specs.append(pl.BlockSpec((rows // steps, cols), lambda i: (i, 0)))
        out_shapes.append(jax.ShapeDtypeStruct(w.shape, BF16))
    return in_specs, out_specs, out_shapes


def _ffn(x, norm_g, w_gate, w_up, w_down, *, name, cast_for_later=()):
    n, d = x.shape
    d_ff = w_gate.shape[1]
    assert n % FFN_TILE == 0 and d_ff % FFN_CHUNK == 0
    assert d % (FFN_WEIGHT_SLABS * BF16_SUBLANES) == 0 and d_ff % (FFN_WEIGHT_SLABS * BF16_SUBLANES) == 0
    steps = n // FFN_TILE
    row_spec = pl.BlockSpec((FFN_TILE, d), lambda i: (i, 0))
    in_hbm = pl.BlockSpec(memory_space=pl.ANY)
    cast_in, cast_out, cast_shapes = _slab_specs(cast_for_later, steps)
    return pl.pallas_call(
        functools.partial(_ffn_kernel, d_ff=d_ff, chunk=FFN_CHUNK, n_cast=len(cast_for_later),
                          n_slabs=FFN_WEIGHT_SLABS),
        grid=(steps,),
        in_specs=[row_spec, _resident((1, d)), in_hbm, in_hbm, in_hbm] + cast_in,
        out_specs=[row_spec] + cast_out,
        out_shape=[jax.ShapeDtypeStruct((n, d), F32)] + cast_shapes,
        scratch_shapes=[
            pltpu.VMEM((d, d_ff), BF16),
            pltpu.VMEM((d, d_ff), BF16),
            pltpu.VMEM((d_ff, d), BF16),
            pltpu.VMEM((FFN_STAGE_SLOTS, d // FFN_WEIGHT_SLABS, d_ff), F32),
            pltpu.VMEM((FFN_STAGE_SLOTS, d // FFN_WEIGHT_SLABS, d_ff), F32),
            pltpu.VMEM((FFN_STAGE_SLOTS, d_ff // FFN_WEIGHT_SLABS, d), F32),
            pltpu.SemaphoreType.DMA((3, FFN_STAGE_SLOTS)),
            pltpu.VMEM((FFN_TILE, d), F32),
        ],
        compiler_params=pltpu.CompilerParams(dimension_semantics=("arbitrary",),
                                             vmem_limit_bytes=VMEM_LIMIT),
        name=name,
    )(x, norm_g.reshape(1, d), w_gate, w_up, w_down, *cast_for_later)


def _head_mean_sq(x, low_half):
    x2 = x * x
    s_low = jnp.sum(jnp.where(low_half, x2, 0.0), axis=-1, keepdims=True)
    s_all = jnp.sum(x2, axis=-1, keepdims=True)
    return jnp.where(low_half, s_low, s_all - s_low) * (1.0 / HEAD_DIM)


def _split_kv_heads(x):
    swapped = pltpu.roll(x, HEAD_DIM, axis=1)
    low = lax.broadcasted_iota(jnp.int32, x.shape, 1) < HEAD_DIM
    zero = jnp.zeros_like(x)
    parts = (jnp.where(low, x, zero), jnp.where(low, zero, swapped),
             jnp.where(low, swapped, zero), jnp.where(low, zero, x))
    return tuple(p.astype(BF16) for p in parts)


def _mixer_kernel(h_ref, mixg_ref, w_in_ref, gbias_ref, poolw_ref, pscale_ref, wpo_ref,
                  qg_ref, kg_ref, sinks_ref, wao_ref, wout_ref, *rest,
                  tiles_per_seq, d_model, pool_width, attn_width, kv_width, n_cast):
    cast_in, o_ref, cast_out = rest[:n_cast], rest[n_cast], rest[n_cast + 1:2 * n_cast + 1]
    (u_scr, xp_scr, q32_scr, q_scr, ka_scr, kb_scr, ve_scr, vo_scr, attn_scr, mixed_scr,
     merged_scr) = rest[2 * n_cast + 1:]
    tm = h_ref.shape[0]
    i = pl.program_id(0)
    tile_in_seq = i % tiles_per_seq
    seq_start = tile_in_seq == 0
    lane128 = lax.broadcasted_iota(jnp.int32, (1, LANES), 1)
    low_half = lane128 < HEAD_DIM

    off_q = pool_width
    off_k = off_q + attn_width
    off_gp = off_k + 2 * kv_width
    off_ga = off_gp + d_model

    @pl.when(seq_start)
    def _():
        xp_scr[0:POOL_HALO, :] = jnp.zeros((POOL_HALO, pool_width), F32)
        zk = jnp.zeros((BLOCK, LANES), BF16)
        ones_e = jnp.broadcast_to(jnp.where(low_half, 1.0, 0.0).astype(BF16), (BLOCK + tm, LANES))
        ones_o = jnp.broadcast_to(jnp.where(low_half, 0.0, 1.0).astype(BF16), (BLOCK + tm, LANES))
        for kv in range(N_KV_HEADS):
            ka_scr[kv, 0:BLOCK, :] = zk
            kb_scr[kv, 0:BLOCK, :] = zk
            ve_scr[kv, 0:BLOCK, 0:LANES] = zk
            vo_scr[kv, 0:BLOCK, 0:LANES] = zk
            ve_scr[kv, :, LANES:2 * LANES] = ones_e
            vo_scr[kv, :, LANES:2 * LANES] = ones_o

    @pl.when(jnp.logical_not(seq_start))
    def _():
        xp_scr[0:POOL_HALO, :] = xp_scr[tm:tm + POOL_HALO, :]
        for kv in range(N_KV_HEADS):
            ka_scr[kv, 0:BLOCK, :] = ka_scr[kv, tm:tm + BLOCK, :]
            kb_scr[kv, 0:BLOCK, :] = kb_scr[kv, tm:tm + BLOCK, :]
            ve_scr[kv, 0:BLOCK, 0:LANES] = ve_scr[kv, tm:tm + BLOCK, 0:LANES]
            vo_scr[kv, 0:BLOCK, 0:LANES] = vo_scr[kv, tm:tm + BLOCK, 0:LANES]

    h = h_ref[...]
    u_scr[...] = _rmsnorm_rows(h, mixg_ref[...]).astype(BF16)

    xp_scr[POOL_HALO:POOL_HALO + tm, :] = _dot(u_scr[...], w_in_ref[:, 0:pool_width])

    q_gain = qg_ref[...] * (HEAD_DIM ** -0.5 * LOG2E)
    q32_scr[...] = _dot(u_scr[...], w_in_ref[:, off_q:off_q + attn_width])
    kv_proj = _dot_rows_split(u_scr[...], w_in_ref[:, off_k:off_k + 2 * kv_width])
    for c in range(attn_width // LANES):
        cols = slice(c * LANES, (c + 1) * LANES)
        q = q32_scr[:, cols]
        ms = _head_mean_sq(q, low_half)
        q_scr[:, cols] = (q * lax.rsqrt(ms + RMS_EPS) * q_gain).astype(BF16)

    k = kv_proj[:, 0:kv_width]
    k = k * lax.rsqrt(_head_mean_sq(k, low_half) + RMS_EPS) * kg_ref[...]
    ka0, kb0, ka1, kb1 = _split_kv_heads(k)
    ka_scr[0, BLOCK:BLOCK + tm, :] = ka0
    kb_scr[0, BLOCK:BLOCK + tm, :] = kb0
    ka_scr[1, BLOCK:BLOCK + tm, :] = ka1
    kb_scr[1, BLOCK:BLOCK + tm, :] = kb1
    va0, vb0, va1, vb1 = _split_kv_heads(kv_proj[:, kv_width:2 * kv_width])
    ve_scr[0, BLOCK:BLOCK + tm, 0:LANES] = va0
    vo_scr[0, BLOCK:BLOCK + tm, 0:LANES] = vb0
    ve_scr[1, BLOCK:BLOCK + tm, 0:LANES] = va1
    vo_scr[1, BLOCK:BLOCK + tm, 0:LANES] = vb1

    t = tile_in_seq * tm + lax.broadcasted_iota(jnp.int32, (tm, 1), 0)
    pooled = []
    for gi, w in enumerate(POOL_WINDOWS):
        cols = slice(gi * POOL_GROUP, (gi + 1) * POOL_GROUP)
        ext = xp_scr[:, cols]
        wsum = ext
        shift = 1
        while shift < w:
            wsum = wsum + pltpu.roll(wsum, shift, axis=0)
            shift *= 2
        count = jnp.minimum(t + 1, w).astype(F32)
        pooled.append((wsum[POOL_HALO:] / count - ext[POOL_HALO:]).astype(BF16))
    for c in range(len(POOL_WINDOWS) // 2):
        cols = slice(c * 2 * POOL_GROUP, (c + 1) * 2 * POOL_GROUP)
        pair = jnp.concatenate(pooled[2 * c:2 * c + 2], axis=1)
        mixed_scr[:, cols] = (_dot(pair, poolw_ref[c]) * pscale_ref[:, cols]).astype(BF16)

    key = lax.broadcasted_iota(jnp.int32, (2 * BLOCK, BLOCK), 0)
    qry = lax.broadcasted_iota(jnp.int32, (2 * BLOCK, BLOCK), 1)
    band = (key > qry) & (key <= qry + BLOCK)
    first_key = jnp.where(seq_start, BLOCK, 0)
    bias = jnp.where(band, 0.0, MASK_BIAS).astype(BF16)
    bias_first = jnp.where(band & (key >= first_key), 0.0, MASK_BIAS).astype(BF16)
    r4 = lax.broadcasted_iota(jnp.int32, (PAIRS_PER_KV * BLOCK, BLOCK), 0) % BLOCK
    c4 = lax.broadcasted_iota(jnp.int32, (PAIRS_PER_KV * BLOCK, BLOCK), 1)
    onehot = jnp.where(r4 == c4, 1.0, 0.0).astype(BF16)
    sinks2 = sinks_ref[...] * LOG2E

    for b in range(tm // BLOCK):
        rows = slice(b * BLOCK, (b + 1) * BLOCK)
        keys = slice(b * BLOCK, (b + 2) * BLOCK)
        key_bias = bias_first if b == 0 else bias
        for kv in range(N_KV_HEADS):
            base = kv * PAIRS_PER_KV
            qs = jnp.concatenate([q_scr[rows, (base + j) * LANES:(base + j + 1) * LANES]
                                  for j in range(PAIRS_PER_KV)], axis=0)
            qs = jnp.concatenate([qs, onehot], axis=1)
            k_even = jnp.concatenate([ka_scr[kv, keys, :], key_bias], axis=1)
            k_odd = jnp.concatenate([kb_scr[kv, keys, :], key_bias], axis=1)
            scores = (_dot_nt(qs, k_even), _dot_nt(qs, k_odd))
            probs = ([], [])
            sink_terms = []
            for j in range(PAIRS_PER_KV):
                m = []
                for par in range(2):
                    s = scores[par][j * BLOCK:(j + 1) * BLOCK]
                    m.append(jnp.max(s, axis=-1, keepdims=True))
                    probs[par].append(jnp.exp2(s - m[par]).astype(BF16))
                sink_terms.append(jnp.exp2(sinks2[base + j:base + j + 1, :]
                                           - jnp.where(low_half, m[0], m[1])))
            p = jnp.concatenate([jnp.concatenate(probs[0], axis=0),
                                 jnp.concatenate(probs[1], axis=0)], axis=1)
            v_stack = jnp.concatenate([ve_scr[kv, keys, :], vo_scr[kv, keys, :]], axis=0)
            acc = _dot_rows_split(p, v_stack)
            den = acc[:, LANES:2 * LANES] + jnp.concatenate(sink_terms, axis=0)
            out = (acc[:, 0:LANES] / den).astype(BF16)
            for j in range(PAIRS_PER_KV):
                attn_scr[rows, (base + j) * LANES:(base + j + 1) * LANES] = (
                    out[j * BLOCK:(j + 1) * BLOCK])

    for c in range(d_model // MIX_COL_CHUNK):
        cols = slice(c * MIX_COL_CHUNK, (c + 1) * MIX_COL_CHUNK)
        gp = jax.nn.sigmoid(
            _dot(u_scr[...], w_in_ref[:, off_gp + c * MIX_COL_CHUNK: off_gp + (c + 1) * MIX_COL_CHUNK])
            + gbias_ref[:, cols])
        ga = jax.nn.sigmoid(
            _dot(u_scr[...], w_in_ref[:, off_ga + c * MIX_COL_CHUNK: off_ga + (c + 1) * MIX_COL_CHUNK])
            + gbias_ref[:, d_model + c * MIX_COL_CHUNK: d_model + (c + 1) * MIX_COL_CHUNK])
        bp = _dot(mixed_scr[...], wpo_ref[:, cols])
        ba = _dot(attn_scr[...], wao_ref[:, cols])
        merged_scr[:, cols] = (gp * bp + ga * ba).astype(BF16)
    o_ref[...] = h + _dot(merged_scr[...], wout_ref[...])
    _cast_slabs(cast_in, cast_out)


def _pair_blockdiag(pool_w):
    g, n, _ = pool_w.shape
    z = jnp.zeros((n, n), pool_w.dtype)
    return jnp.stack([jnp.block([[pool_w[2 * c], z], [z, pool_w[2 * c + 1]]])
                      for c in range(g // 2)])


def _mixer(h, seq_len, mix_norm, w_in, pool_w, pool_scale, w_pool_out, q_norm, k_norm, sinks,
           w_attn_out, gate_bias, w_out, *, cast_for_later=()):
    n, d = h.shape
    tm = MIX_TILE
    pool_width = w_pool_out.shape[0]
    attn_width = w_attn_out.shape[0]
    kv_width = N_KV_HEADS * HEAD_DIM
    in_width = w_in.shape[1]
    assert seq_len % tm == 0 and tm % BLOCK == 0
    assert pool_width == len(POOL_WINDOWS) * POOL_GROUP and attn_width == N_HEADS * HEAD_DIM
    assert in_width == pool_width + attn_width + 2 * kv_width + 2 * d and kv_width == LANES
    row_spec = pl.BlockSpec((tm, d), lambda i: (i, 0))
    cast_in, cast_out, cast_shapes = _slab_specs(cast_for_later, n // tm)
    kern = functools.partial(_mixer_kernel, tiles_per_seq=seq_len // tm, d_model=d,
                             pool_width=pool_width, attn_width=attn_width, kv_width=kv_width,
                             n_cast=len(cast_for_later))
    pool_w2 = _pair_blockdiag(pool_w).astype(BF16)
    sink_rows = jnp.repeat(sinks.reshape(N_HEADS // 2, 2), HEAD_DIM, axis=1)
    return pl.pallas_call(
        kern,
        grid=(n // tm,),
        in_specs=[row_spec, _resident((1, d)), _resident((d, in_width)), _resident((1, 2 * d)),
                  _resident(pool_w2.shape), _resident((1, pool_width)), _resident((pool_width, d)),
                  _resident((1, LANES)), _resident((1, LANES)), _resident(sink_rows.shape),
                  _resident((attn_width, d)), _resident((d, d))] + cast_in,
        out_specs=[row_spec] + cast_out,
        out_shape=[jax.ShapeDtypeStruct((n, d), F32)] + cast_shapes,
        scratch_shapes=[
            pltpu.VMEM((tm, d), BF16),
            pltpu.VMEM((POOL_HALO + tm, pool_width), F32),
            pltpu.VMEM((tm, attn_width), F32),
            pltpu.VMEM((tm, attn_width), BF16),
            pltpu.VMEM((N_KV_HEADS, BLOCK + tm, LANES), BF16),
            pltpu.VMEM((N_KV_HEADS, BLOCK + tm, LANES), BF16),
            pltpu.VMEM((N_KV_HEADS, BLOCK + tm, 2 * LANES), BF16),
            pltpu.VMEM((N_KV_HEADS, BLOCK + tm, 2 * LANES), BF16),
            pltpu.VMEM((tm, attn_width), BF16),
            pltpu.VMEM((tm, pool_width), BF16),
            pltpu.VMEM((tm, d), BF16),
        ],
        compiler_params=pltpu.CompilerParams(dimension_semantics=("arbitrary",),
                                             vmem_limit_bytes=VMEM_LIMIT),
        name="mixer",
    )(h, mix_norm.reshape(1, d), w_in, gate_bias.reshape(1, 2 * d),
      pool_w2, pool_scale.reshape(1, pool_width), w_pool_out,
      jnp.tile(q_norm, LANES // HEAD_DIM).reshape(1, LANES),
      jnp.tile(k_norm, LANES // HEAD_DIM).reshape(1, LANES), sink_rows,
      w_attn_out, w_out, *cast_for_later)


def kernel(x, ffn1_norm, ffn1_w_gate, ffn1_w_up, ffn1_w_down, mix_norm, w_in, pool_w, pool_scale,
           w_pool_out, q_norm, k_norm, sinks, w_attn_out, gate_bias, w_out, ffn2_norm, ffn2_w_gate,
           ffn2_w_up, ffn2_w_down):
    b, s, d = x.shape
    h = x.reshape(b * s, d)
    h, w_in_b, w_pool_out_b, w_attn_out_b, w_out_b = _ffn(
        h, ffn1_norm, ffn1_w_gate, ffn1_w_up, ffn1_w_down, name="ffn1",
        cast_for_later=(w_in, w_pool_out, w_attn_out, w_out))
    (h,) = _mixer(h, s, mix_norm, w_in_b, pool_w, pool_scale, w_pool_out_b, q_norm, k_norm, sinks,
                  w_attn_out_b, gate_bias, w_out_b)
    (h,) = _ffn(h, ffn2_norm, ffn2_w_gate, ffn2_w_up, ffn2_w_down, name="ffn2")
    return h.reshape(b, s, d)
```

```python
import functools
import math

import jax
import jax.numpy as jnp
from jax import lax
from jax.experimental import pallas as pl
from jax.experimental.pallas import tpu as pltpu

F32 = jnp.float32
BF16 = jnp.bfloat16

RMS_EPS = 1e-6
HEAD_DIM = 64
N_HEADS = 16
N_KV_HEADS = 2
PAIRS_PER_KV = N_HEADS // N_KV_HEADS // 2
BLOCK = 128
POOL_WINDOWS = (2, 4, 8, 16)
POOL_GROUP = 128
POOL_HALO = 16
LANES = 128
BF16_SUBLANES = 16
V7X_VMEM_BYTES = 64 * 1024 * 1024
VMEM_LIMIT = V7X_VMEM_BYTES - 4 * 1024 * 1024
LOG2E = math.log2(math.e)
MASK_BIAS = -3.0e38

FFN_TILE = 1024
FFN_CHUNK = 256
FFN_WEIGHT_SLABS = 16
FFN_STAGE_SLOTS = 3
MIX_TILE = 1024
MIX_COL_CHUNK = 256


def _dot(a, b):
    return jnp.dot(a, b, preferred_element_type=F32)


def _dot_rows_split(a, b):
    half = a.shape[0] // 2
    return jnp.concatenate([_dot(a[:half], b), _dot(a[half:], b)], axis=0)


def _dot_nt(a, b):
    return lax.dot_general(a, b, (((1,), (1,)), ((), ())), preferred_element_type=F32)


def _rmsnorm_rows(x, g):
    ms = jnp.mean(x * x, axis=-1, keepdims=True)
    return x * lax.rsqrt(ms + RMS_EPS) * g


def _cast_slabs(f32_refs, bf16_refs):
    for src, dst in zip(f32_refs, bf16_refs, strict=True):
        dst[...] = src[...].astype(BF16)


def _ffn_kernel(x_ref, g_ref, wg_hbm, wu_hbm, wd_hbm, *rest, d_ff, chunk, n_cast, n_slabs):
    cast_in, o_ref, cast_out = rest[:n_cast], rest[n_cast], rest[n_cast + 1:2 * n_cast + 1]
    wg_v, wu_v, wd_v, stage_g, stage_u, stage_d, sems, acc_ref = rest[2 * n_cast + 1:]
    n_slots = stage_g.shape[0]
    streams = ((wg_hbm, stage_g, wg_v), (wu_hbm, stage_u, wu_v), (wd_hbm, stage_d, wd_v))

    def slab_copies(s):
        slot = s % n_slots
        return [pltpu.make_async_copy(hbm.at[pl.ds(s * stage.shape[1], stage.shape[1]), :],
                                      stage.at[slot], sems.at[k, slot])
                for k, (hbm, stage, _) in enumerate(streams)]

    @pl.when(pl.program_id(0) == 0)
    def _():
        for s in range(min(n_slots, n_slabs)):
            for cp in slab_copies(s):
                cp.start()
        for s in range(n_slabs):
            for cp in slab_copies(s):
                cp.wait()
            for _, stage, dst in streams:
                rows = stage.shape[1]
                dst[s * rows:(s + 1) * rows, :] = stage[s % n_slots].astype(BF16)
            if s + n_slots < n_slabs:
                for cp in slab_copies(s + n_slots):
                    cp.start()

    x = x_ref[...]
    xn = _rmsnorm_rows(x, g_ref[...]).astype(BF16)
    for c in range(d_ff // chunk):
        cols = slice(c * chunk, (c + 1) * chunk)
        g = _dot(xn, wg_v[:, cols])
        u = _dot(xn, wu_v[:, cols])
        a = (g * jax.nn.sigmoid(g) * u).astype(BF16)
        d = _dot(a, wd_v[cols, :])
        if c == 0:
            acc_ref[...] = d
        else:
            acc_ref[...] += d
    o_ref[...] = x + 0.5 * acc_ref[...]
    _cast_slabs(cast_in, cast_out)


def _resident(shape):
    return pl.BlockSpec(shape, lambda i: (0,) * len(shape), pipeline_mode=pl.Buffered(1))


def _slab_specs(weights, steps):
    in_specs, out_specs, out_shapes = [], [], []
    for w in weights:
        rows, cols = w.shape
        assert rows % (steps * BF16_SUBLANES) == 0
        in_specs.append(pl.BlockSpec((rows // steps, cols), lambda i: (i, 0)))
        out_specs.append(pl.BlockSpec((rows // steps, cols), lambda i: (i, 0)))
        out_shapes.append(jax.ShapeDtypeStruct(w.shape, BF16))
    return in_specs, out_specs, out_shapes


def _ffn(x, norm_g, w_gate, w_up, w_down, *, name, cast_for_later=()):
    n, d = x.shape
    d_ff = w_gate.shape[1]
    assert n % FFN_TILE == 0 and d_ff % FFN_CHUNK == 0
    assert d % (FFN_WEIGHT_SLABS * BF16_SUBLANES) == 0 and d_ff % (FFN_WEIGHT_SLABS * BF16_SUBLANES) == 0
    steps = n // FFN_TILE
    row_spec = pl.BlockSpec((FFN_TILE, d), lambda i: (i, 0))
    in_hbm = pl.BlockSpec(memory_space=pl.ANY)
    cast_in, cast_out, cast_shapes = _slab_specs(cast_for_later, steps)
    return pl.pallas_call(
        functools.partial(_ffn_kernel, d_ff=d_ff, chunk=FFN_CHUNK, n_cast=len(cast_for_later),
                          n_slabs=FFN_WEIGHT_SLABS),
        grid=(steps,),
        in_specs=[row_spec, _resident((1, d)), in_hbm, in_hbm, in_hbm] + cast_in,
        out_specs=[row_spec] + cast_out,
        out_shape=[jax.ShapeDtypeStruct((n, d), F32)] + cast_shapes,
        scratch_shapes=[
            pltpu.VMEM((d, d_ff), BF16),
            pltpu.VMEM((d, d_ff), BF16),
            pltpu.VMEM((d_ff, d), BF16),
            pltpu.VMEM((FFN_STAGE_SLOTS, d // FFN_WEIGHT_SLABS, d_ff), F32),
            pltpu.VMEM((FFN_STAGE_SLOTS, d // FFN_WEIGHT_SLABS, d_ff), F32),
            pltpu.VMEM((FFN_STAGE_SLOTS, d_ff // FFN_WEIGHT_SLABS, d), F32),
            pltpu.SemaphoreType.DMA((3, FFN_STAGE_SLOTS)),
            pltpu.VMEM((FFN_TILE, d), F32),
        ],
        compiler_params=pltpu.CompilerParams(dimension_semantics=("arbitrary",),
                                             vmem_limit_bytes=VMEM_LIMIT),
        name=name,
    )(x, norm_g.reshape(1, d), w_gate, w_up, w_down, *cast_for_later)


def _head_mean_sq(x, low_half):
    x2 = x * x
    s_low = jnp.sum(jnp.where(low_half, x2, 0.0), axis=-1, keepdims=True)
    s_all = jnp.sum(x2, axis=-1, keepdims=True)
    return jnp.where(low_half, s_low, s_all - s_low) * (1.0 / HEAD_DIM)


def _split_kv_heads(x):
    swapped = pltpu.roll(x, HEAD_DIM, axis=1)
    low = lax.broadcasted_iota(jnp.int32, x.shape, 1) < HEAD_DIM
    zero = jnp.zeros_like(x)
    parts = (jnp.where(low, x, zero), jnp.where(low, zero, swapped),
             jnp.where(low, swapped, zero), jnp.where(low, zero, x))
    return tuple(p.astype(BF16) for p in parts)


def _mixer_kernel(h_ref, mixg_ref, w_in_ref, gbias_ref, poolw_ref, pscale_ref, wpo_ref,
                  qg_ref, kg_ref, sinks_ref, wao_ref, wout_ref, *rest,
                  tiles_per_seq, d_model, pool_width, attn_width, kv_width, n_cast):
    cast_in, o_ref, cast_out = rest[:n_cast], rest[n_cast], rest[n_cast + 1:2 * n_cast + 1]
    (u_scr, xp_scr, q32_scr, q_scr, ka_scr, kb_scr, ve_scr, vo_scr, attn_scr, mixed_scr,
     gate_scr, merged_scr) = rest[2 * n_cast + 1:]
    tm = h_ref.shape[0]
    i = pl.program_id(0)
    tile_in_seq = i % tiles_per_seq
    seq_start = tile_in_seq == 0
    lane128 = lax.broadcasted_iota(jnp.int32, (1, LANES), 1)
    low_half = lane128 < HEAD_DIM

    off_q = pool_width
    off_k = off_q + attn_width
    off_gp = off_k + 2 * kv_width
    off_ga = off_gp + d_model

    @pl.when(seq_start)
    def _():
        xp_scr[0:POOL_HALO, :] = jnp.zeros((POOL_HALO, pool_width), F32)
        zk = jnp.zeros((BLOCK, LANES), BF16)
        ones_e = jnp.broadcast_to(jnp.where(low_half, 1.0, 0.0).astype(BF16), (BLOCK + tm, LANES))
        ones_o = jnp.broadcast_to(jnp.where(low_half, 0.0, 1.0).astype(BF16), (BLOCK + tm, LANES))
        for kv in range(N_KV_HEADS):
            ka_scr[kv, 0:BLOCK, :] = zk
            kb_scr[kv, 0:BLOCK, :] = zk
            ve_scr[kv, 0:BLOCK, 0:LANES] = zk
            vo_scr[kv, 0:BLOCK, 0:LANES] = zk
            ve_scr[kv, :, LANES:2 * LANES] = ones_e
            vo_scr[kv, :, LANES:2 * LANES] = ones_o

    @pl.when(jnp.logical_not(seq_start))
    def _():
        xp_scr[0:POOL_HALO, :] = xp_scr[tm:tm + POOL_HALO, :]
        for kv in range(N_KV_HEADS):
            ka_scr[kv, 0:BLOCK, :] = ka_scr[kv, tm:tm + BLOCK, :]
            kb_scr[kv, 0:BLOCK, :] = kb_scr[kv, tm:tm + BLOCK, :]
            ve_scr[kv, 0:BLOCK, 0:LANES] = ve_scr[kv, tm:tm + BLOCK, 0:LANES]
            vo_scr[kv, 0:BLOCK, 0:LANES] = vo_scr[kv, tm:tm + BLOCK, 0:LANES]

    h = h_ref[...]
    u_scr[...] = _rmsnorm_rows(h, mixg_ref[...]).astype(BF16)

    xp_scr[POOL_HALO:POOL_HALO + tm, :] = _dot(u_scr[...], w_in_ref[:, 0:pool_width])
    q32_scr[...] = _dot(u_scr[...], w_in_ref[:, off_q:off_q + attn_width])
    kv_proj = _dot_rows_split(u_scr[...], w_in_ref[:, off_k:off_k + 2 * kv_width])

    t = tile_in_seq * tm + lax.broadcasted_iota(jnp.int32, (tm, 1), 0)
    pooled = []
    for gi, w in enumerate(POOL_WINDOWS):
        cols = slice(gi * POOL_GROUP, (gi + 1) * POOL_GROUP)
        ext = xp_scr[:, cols]
        wsum = ext
        shift = 1
        while shift < w:
            wsum = wsum + pltpu.roll(wsum, shift, axis=0)
            shift *= 2
        count = jnp.minimum(t + 1, w).astype(F32)
        pooled.append((wsum[POOL_HALO:] / count - ext[POOL_HALO:]).astype(BF16))
    for c in range(len(POOL_WINDOWS) // 2):
        cols = slice(c * 2 * POOL_GROUP, (c + 1) * 2 * POOL_GROUP)
        pair = jnp.concatenate(pooled[2 * c:2 * c + 2], axis=1)
        mixed_scr[:, cols] = (_dot(pair, poolw_ref[c]) * pscale_ref[:, cols]).astype(BF16)

    q_gain = qg_ref[...] * (HEAD_DIM ** -0.5 * LOG2E)
    n_gate_chunks = d_model // MIX_COL_CHUNK
    q_chunks_per_gate_chunk = attn_width // LANES // n_gate_chunks
    for c in range(n_gate_chunks):
        cols = slice(c * MIX_COL_CHUNK, (c + 1) * MIX_COL_CHUNK)
        cols_a = slice(d_model + c * MIX_COL_CHUNK, d_model + (c + 1) * MIX_COL_CHUNK)
        gp = jax.nn.sigmoid(
            _dot(u_scr[...], w_in_ref[:, off_gp + c * MIX_COL_CHUNK: off_gp + (c + 1) * MIX_COL_CHUNK])
            + gbias_ref[:, cols])
        gate_scr[:, cols] = gp * _dot(mixed_scr[...], wpo_ref[:, cols])
        gate_scr[:, cols_a] = jax.nn.sigmoid(
            _dot(u_scr[...], w_in_ref[:, off_ga + c * MIX_COL_CHUNK: off_ga + (c + 1) * MIX_COL_CHUNK])
            + gbias_ref[:, cols_a])
        for qc in range(c * q_chunks_per_gate_chunk, (c + 1) * q_chunks_per_gate_chunk):
            qcols = slice(qc * LANES, (qc + 1) * LANES)
            q = q32_scr[:, qcols]
            ms = _head_mean_sq(q, low_half)
            q_scr[:, qcols] = (q * lax.rsqrt(ms + RMS_EPS) * q_gain).astype(BF16)

    k = kv_proj[:, 0:kv_width]
    k = k * lax.rsqrt(_head_mean_sq(k, low_half) + RMS_EPS) * kg_ref[...]
    ka0, kb0, ka1, kb1 = _split_kv_heads(k)
    ka_scr[0, BLOCK:BLOCK + tm, :] = ka0
    kb_scr[0, BLOCK:BLOCK + tm, :] = kb0
    ka_scr[1, BLOCK:BLOCK + tm, :] = ka1
    kb_scr[1, BLOCK:BLOCK + tm, :] = kb1
    va0, vb0, va1, vb1 = _split_kv_heads(kv_proj[:, kv_width:2 * kv_width])
    ve_scr[0, BLOCK:BLOCK + tm, 0:LANES] = va0
    vo_scr[0, BLOCK:BLOCK + tm, 0:LANES] = vb0
    ve_scr[1, BLOCK:BLOCK + tm, 0:LANES] = va1
    vo_scr[1, BLOCK:BLOCK + tm, 0:LANES] = vb1

    key = lax.broadcasted_iota(jnp.int32, (2 * BLOCK, BLOCK), 0)
    qry = lax.broadcasted_iota(jnp.int32, (2 * BLOCK, BLOCK), 1)
    band = (key > qry) & (key <= qry + BLOCK)
    first_key = jnp.where(seq_start, BLOCK, 0)
    bias = jnp.where(band, 0.0, MASK_BIAS).astype(BF16)
    bias_first = jnp.where(band & (key >= first_key), 0.0, MASK_BIAS).astype(BF16)
    r4 = lax.broadcasted_iota(jnp.int32, (PAIRS_PER_KV * BLOCK, BLOCK), 0) % BLOCK
    c4 = lax.broadcasted_iota(jnp.int32, (PAIRS_PER_KV * BLOCK, BLOCK), 1)
    onehot = jnp.where(r4 == c4, 1.0, 0.0).astype(BF16)
    sinks2 = sinks_ref[...] * LOG2E

    for b in range(tm // BLOCK):
        rows = slice(b * BLOCK, (b + 1) * BLOCK)
        keys = slice(b * BLOCK, (b + 2) * BLOCK)
        key_bias = bias_first if b == 0 else bias
        for kv in range(N_KV_HEADS):
            base = kv * PAIRS_PER_KV
            qs = jnp.concatenate([q_scr[rows, (base + j) * LANES:(base + j + 1) * LANES]
                                  for j in range(PAIRS_PER_KV)], axis=0)
            qs = jnp.concatenate([qs, onehot], axis=1)
            k_even = jnp.concatenate([ka_scr[kv, keys, :], key_bias], axis=1)
            k_odd = jnp.concatenate([kb_scr[kv, keys, :], key_bias], axis=1)
            scores = (_dot_nt(qs, k_even), _dot_nt(qs, k_odd))
            probs = ([], [])
            sink_terms = []
            for j in range(PAIRS_PER_KV):
                m = []
                for par in range(2):
                    s = scores[par][j * BLOCK:(j + 1) * BLOCK]
                    m.append(jnp.max(s, axis=-1, keepdims=True))
                    probs[par].append(jnp.exp2(s - m[par]).astype(BF16))
                sink_terms.append(jnp.exp2(sinks2[base + j:base + j + 1, :]
                                           - jnp.where(low_half, m[0], m[1])))
            p = jnp.concatenate([jnp.concatenate(probs[0], axis=0),
                                 jnp.concatenate(probs[1], axis=0)], axis=1)
            v_stack = jnp.concatenate([ve_scr[kv, keys, :], vo_scr[kv, keys, :]], axis=0)
            acc = _dot_rows_split(p, v_stack)
            den = acc[:, LANES:2 * LANES] + jnp.concatenate(sink_terms, axis=0)
            out = (acc[:, 0:LANES] / den).astype(BF16)
            for j in range(PAIRS_PER_KV):
                attn_scr[rows, (base + j) * LANES:(base + j + 1) * LANES] = (
                    out[j * BLOCK:(j + 1) * BLOCK])

    for c in range(n_gate_chunks):
        cols = slice(c * MIX_COL_CHUNK, (c + 1) * MIX_COL_CHUNK)
        cols_a = slice(d_model + c * MIX_COL_CHUNK, d_model + (c + 1) * MIX_COL_CHUNK)
        ba = _dot(attn_scr[...], wao_ref[:, cols])
        merged_scr[:, cols] = (gate_scr[:, cols] + gate_scr[:, cols_a] * ba).astype(BF16)
    o_ref[...] = h + _dot(merged_scr[...], wout_ref[...])
    _cast_slabs(cast_in, cast_out)


def _pair_blockdiag(pool_w):
    g, n, _ = pool_w.shape
    z = jnp.zeros((n, n), pool_w.dtype)
    return jnp.stack([jnp.block([[pool_w[2 * c], z], [z, pool_w[2 * c + 1]]])
                      for c in range(g // 2)])


def _mixer(h, seq_len, mix_norm, w_in, pool_w, pool_scale, w_pool_out, q_norm, k_norm, sinks,
           w_attn_out, gate_bias, w_out, *, cast_for_later=()):
    n, d = h.shape
    tm = MIX_TILE
    pool_width = w_pool_out.shape[0]
    attn_width = w_attn_out.shape[0]
    kv_width = N_KV_HEADS * HEAD_DIM
    in_width = w_in.shape[1]
    assert seq_len % tm == 0 and tm % BLOCK == 0
    assert pool_width == len(POOL_WINDOWS) * POOL_GROUP and attn_width == N_HEADS * HEAD_DIM
    assert in_width == pool_width + attn_width + 2 * kv_width + 2 * d and kv_width == LANES
    row_spec = pl.BlockSpec((tm, d), lambda i: (i, 0))
    cast_in, cast_out, cast_shapes = _slab_specs(cast_for_later, n // tm)
    kern = functools.partial(_mixer_kernel, tiles_per_seq=seq_len // tm, d_model=d,
                             pool_width=pool_width, attn_width=attn_width, kv_width=kv_width,
                             n_cast=len(cast_for_later))
    pool_w2 = _pair_blockdiag(pool_w).astype(BF16)
    sink_rows = jnp.repeat(sinks.reshape(N_HEADS // 2, 2), HEAD_DIM, axis=1)
    return pl.pallas_call(
        kern,
        grid=(n // tm,),
        in_specs=[row_spec, _resident((1, d)), _resident((d, in_width)), _resident((1, 2 * d)),
                  _resident(pool_w2.shape), _resident((1, pool_width)), _resident((pool_width, d)),
                  _resident((1, LANES)), _resident((1, LANES)), _resident(sink_rows.shape),
                  _resident((attn_width, d)), _resident((d, d))] + cast_in,
        out_specs=[row_spec] + cast_out,
        out_shape=[jax.ShapeDtypeStruct((n, d), F32)] + cast_shapes,
        scratch_shapes=[
            pltpu.VMEM((tm, d), BF16),
            pltpu.VMEM((POOL_HALO + tm, pool_width), F32),
            pltpu.VMEM((tm, attn_width), F32),
            pltpu.VMEM((tm, attn_width), BF16),
            pltpu.VMEM((N_KV_HEADS, BLOCK + tm, LANES), BF16),
            pltpu.VMEM((N_KV_HEADS, BLOCK + tm, LANES), BF16),
            pltpu.VMEM((N_KV_HEADS, BLOCK + tm, 2 * LANES), BF16),
            pltpu.VMEM((N_KV_HEADS, BLOCK + tm, 2 * LANES), BF16),
            pltpu.VMEM((tm, attn_width), BF16),
            pltpu.VMEM((tm, pool_width), BF16),
            pltpu.VMEM((tm, 2 * d), F32),
            pltpu.VMEM((tm, d), BF16),
        ],
        compiler_params=pltpu.CompilerParams(dimension_semantics=("arbitrary",),
                                             vmem_limit_bytes=VMEM_LIMIT),
        name="mixer",
    )(h, mix_norm.reshape(1, d), w_in, gate_bias.reshape(1, 2 * d),
      pool_w2, pool_scale.reshape(1, pool_width), w_pool_out,
      jnp.tile(q_norm, LANES // HEAD_DIM).reshape(1, LANES),
      jnp.tile(k_norm, LANES // HEAD_DIM).reshape(1, LANES), sink_rows,
      w_attn_out, w_out, *cast_for_later)


def kernel(x, ffn1_norm, ffn1_w_gate, ffn1_w_up, ffn1_w_down, mix_norm, w_in, pool_w, pool_scale,
           w_pool_out, q_norm, k_norm, sinks, w_attn_out, gate_bias, w_out, ffn2_norm, ffn2_w_gate,
           ffn2_w_up, ffn2_w_down):
    b, s, d = x.shape
    h = x.reshape(b * s, d)
    h, w_in_b, w_pool_out_b, w_attn_out_b, w_out_b = _ffn(
        h, ffn1_norm, ffn1_w_gate, ffn1_w_up, ffn1_w_down, name="ffn1",
        cast_for_later=(w_in, w_pool_out, w_attn_out, w_out))
    (h,) = _mixer(h, s, mix_norm, w_in_b, pool_w, pool_scale, w_pool_out_b, q_norm, k_norm, sinks,
                  w_attn_out_b, gate_bias, w_out_b)
    (h,) = _ffn(h, ffn2_norm, ffn2_w_gate, ffn2_w_up, ffn2_w_down, name="ffn2")
    return h.reshape(b, s, d)
```

```python
import functools
import math

import jax
import jax.numpy as jnp
from jax import lax
from jax.experimental import pallas as pl
from jax.experimental.pallas import tpu as pltpu

F32 = jnp.float32
BF16 = jnp.bfloat16

RMS_EPS = 1e-6
HEAD_DIM = 64
N_HEADS = 16
N_KV_HEADS = 2
PAIRS_PER_KV = N_HEADS // N_KV_HEADS // 2
BLOCK = 128
POOL_WINDOWS = (2, 4, 8, 16)
POOL_GROUP = 128
POOL_HALO = 16
LANES = 128
BF16_SUBLANES = 16
V7X_VMEM_BYTES = 64 * 1024 * 1024
VMEM_LIMIT = V7X_VMEM_BYTES - 4 * 1024 * 1024
LOG2E = math.log2(math.e)
MASK_BIAS = -3.0e38

FFN_TILE = 1024
FFN_CHUNK = 256
FFN_WEIGHT_SLABS = 16
FFN_STAGE_SLOTS = 3
MIX_TILE = 1024
MIX_COL_CHUNK = 256


def _dot(a, b):
    return jnp.dot(a, b, preferred_element_type=F32)


def _dot_rows_split(a, b):
    half = a.shape[0] // 2
    return jnp.concatenate([_dot(a[:half], b), _dot(a[half:], b)], axis=0)


def _dot_nt(a, b):
    return lax.dot_general(a, b, (((1,), (1,)), ((), ())), preferred_element_type=F32)


def _rmsnorm_rows(x, g):
    ms = jnp.mean(x * x, axis=-1, keepdims=True)
    return x * lax.rsqrt(ms + RMS_EPS) * g


def _cast_slabs(f32_refs, bf16_refs):
    for src, dst in zip(f32_refs, bf16_refs, strict=True):
        dst[...] = src[...].astype(BF16)


def _ffn_kernel(x_ref, g_ref, wg_hbm, wu_hbm, wd_hbm, *rest, d_ff, chunk, n_cast, n_slabs):
    cast_in, o_ref, cast_out = rest[:n_cast], rest[n_cast], rest[n_cast + 1:2 * n_cast + 1]
    wg_v, wu_v, wd_v, stage_g, stage_u, stage_d, sems, acc_ref = rest[2 * n_cast + 1:]
    n_slots = stage_g.shape[0]
    streams = ((wg_hbm, stage_g, wg_v), (wu_hbm, stage_u, wu_v), (wd_hbm, stage_d, wd_v))

    def slab_copies(s):
        slot = s % n_slots
        return [pltpu.make_async_copy(hbm.at[pl.ds(s * stage.shape[1], stage.shape[1]), :],
                                      stage.at[slot], sems.at[k, slot])
                for k, (hbm, stage, _) in enumerate(streams)]

    @pl.when(pl.program_id(0) == 0)
    def _():
        for s in range(min(n_slots, n_slabs)):
            for cp in slab_copies(s):
                cp.start()
        for s in range(n_slabs):
            for cp in slab_copies(s):
                cp.wait()
            for _, stage, dst in streams:
                rows = stage.shape[1]
                dst[s * rows:(s + 1) * rows, :] = stage[s % n_slots].astype(BF16)
            if s + n_slots < n_slabs:
                for cp in slab_copies(s + n_slots):
                    cp.start()

    x = x_ref[...]
    xn = _rmsnorm_rows(x, g_ref[...]).astype(BF16)
    for c in range(d_ff // chunk):
        cols = slice(c * chunk, (c + 1) * chunk)
        g = _dot(xn, wg_v[:, cols])
        u = _dot(xn, wu_v[:, cols])
        a = (g * jax.nn.sigmoid(g) * u).astype(BF16)
        d = _dot(a, wd_v[cols, :])
        if c == 0:
            acc_ref[...] = d
        else:
            acc_ref[...] += d
    o_ref[...] = x + 0.5 * acc_ref[...]
    _cast_slabs(cast_in, cast_out)


def _resident(shape):
    return pl.BlockSpec(shape, lambda i: (0,) * len(shape), pipeline_mode=pl.Buffered(1))


def _slab_specs(weights, steps):
    in_specs, out_specs, out_shapes = [], [], []
    for w in weights:
        rows, cols = w.shape
        assert rows % (steps * BF16_SUBLANES) == 0
        in_specs.append(pl.BlockSpec((rows // steps, cols), lambda i: (i, 0)))
        out_specs.append(pl.BlockSpec((rows // steps, cols), lambda i: (i, 0)))
        out_shapes.append(jax.ShapeDtypeStruct(w.shape, BF16))
    return in_specs, out_specs, out_shapes


def _ffn(x, norm_g, w_gate, w_up, w_down, *, name, cast_for_later=()):
    n, d = x.shape
    d_ff = w_gate.shape[1]
    assert n % FFN_TILE == 0 and d_ff % FFN_CHUNK == 0
    assert d % (FFN_WEIGHT_SLABS * BF16_SUBLANES) == 0 and d_ff % (FFN_WEIGHT_SLABS * BF16_SUBLANES) == 0
    steps = n // FFN_TILE
    row_spec = pl.BlockSpec((FFN_TILE, d), lambda i: (i, 0))
    in_hbm = pl.BlockSpec(memory_space=pl.ANY)
    cast_in, cast_out, cast_shapes = _slab_specs(cast_for_later, steps)
    return pl.pallas_call(
        functools.partial(_ffn_kernel, d_ff=d_ff, chunk=FFN_CHUNK, n_cast=len(cast_for_later),
                          n_slabs=FFN_WEIGHT_SLABS),
        grid=(steps,),
        in_specs=[row_spec, _resident((1, d)), in_hbm, in_hbm, in_hbm] + cast_in,
        out_specs=[row_spec] + cast_out,
        out_shape=[jax.ShapeDtypeStruct((n, d), F32)] + cast_shapes,
        scratch_shapes=[
            pltpu.VMEM((d, d_ff), BF16),
            pltpu.VMEM((d, d_ff), BF16),
            pltpu.VMEM((d_ff, d), BF16),
            pltpu.VMEM((FFN_STAGE_SLOTS, d // FFN_WEIGHT_SLABS, d_ff), F32),
            pltpu.VMEM((FFN_STAGE_SLOTS, d // FFN_WEIGHT_SLABS, d_ff), F32),
            pltpu.VMEM((FFN_STAGE_SLOTS, d_ff // FFN_WEIGHT_SLABS, d), F32),
            pltpu.SemaphoreType.DMA((3, FFN_STAGE_SLOTS)),
            pltpu.VMEM((FFN_TILE, d), F32),
        ],
        compiler_params=pltpu.CompilerParams(dimension_semantics=("arbitrary",),
                                             vmem_limit_bytes=VMEM_LIMIT),
        name=name,
    )(x, norm_g.reshape(1, d), w_gate, w_up, w_down, *cast_for_later)


def _head_mean_sq(x, low_half):
    x2 = x * x
    s_low = jnp.sum(jnp.where(low_half, x2, 0.0), axis=-1, keepdims=True)
    s_all = jnp.sum(x2, axis=-1, keepdims=True)
    return jnp.where(low_half, s_low, s_all - s_low) * (1.0 / HEAD_DIM)


def _split_kv_heads(x):
    swapped = pltpu.roll(x, HEAD_DIM, axis=1)
    low = lax.broadcasted_iota(jnp.int32, x.shape, 1) < HEAD_DIM
    zero = jnp.zeros_like(x)
    parts = (jnp.where(low, x, zero), jnp.where(low, zero, swapped),
             jnp.where(low, swapped, zero), jnp.where(low, zero, x))
    return tuple(p.astype(BF16) for p in parts)


def _mixer_kernel(h_ref, mixg_ref, w_in_ref, gbias_ref, poolw_ref, pscale_ref, wpo_ref,
                  qg_ref, kg_ref, sinks_ref, wao_ref, wout_ref, *rest,
                  tiles_per_seq, d_model, pool_width, attn_width, kv_width, n_cast):
    cast_in, o_ref, cast_out = rest[:n_cast], rest[n_cast], rest[n_cast + 1:2 * n_cast + 1]
    (u_scr, xp_scr, q32_scr, q_scr, ka_scr, kb_scr, ve_scr, vo_scr, attn_scr, mixed_scr,
     gate_scr, merged_scr) = rest[2 * n_cast + 1:]
    tm = h_ref.shape[0]
    i = pl.program_id(0)
    tile_in_seq = i % tiles_per_seq
    seq_start = tile_in_seq == 0
    lane128 = lax.broadcasted_iota(jnp.int32, (1, LANES), 1)
    low_half = lane128 < HEAD_DIM

    off_q = pool_width
    off_k = off_q + attn_width
    off_gp = off_k + 2 * kv_width
    off_ga = off_gp + d_model

    @pl.when(seq_start)
    def _():
        xp_scr[0:POOL_HALO, :] = jnp.zeros((POOL_HALO, pool_width), F32)
        zk = jnp.zeros((BLOCK, LANES), BF16)
        ones_e = jnp.broadcast_to(jnp.where(low_half, 1.0, 0.0).astype(BF16), (BLOCK + tm, LANES))
        ones_o = jnp.broadcast_to(jnp.where(low_half, 0.0, 1.0).astype(BF16), (BLOCK + tm, LANES))
        for kv in range(N_KV_HEADS):
            ka_scr[kv, 0:BLOCK, :] = zk
            kb_scr[kv, 0:BLOCK, :] = zk
            ve_scr[kv, 0:BLOCK, 0:LANES] = zk
            vo_scr[kv, 0:BLOCK, 0:LANES] = zk
            ve_scr[kv, :, LANES:2 * LANES] = ones_e
            vo_scr[kv, :, LANES:2 * LANES] = ones_o

    @pl.when(jnp.logical_not(seq_start))
    def _():
        xp_scr[0:POOL_HALO, :] = xp_scr[tm:tm + POOL_HALO, :]
        for kv in range(N_KV_HEADS):
            ka_scr[kv, 0:BLOCK, :] = ka_scr[kv, tm:tm + BLOCK, :]
            kb_scr[kv, 0:BLOCK, :] = kb_scr[kv, tm:tm + BLOCK, :]
            ve_scr[kv, 0:BLOCK, 0:LANES] = ve_scr[kv, tm:tm + BLOCK, 0:LANES]
            vo_scr[kv, 0:BLOCK, 0:LANES] = vo_scr[kv, tm:tm + BLOCK, 0:LANES]

    u_scr[...] = _rmsnorm_rows(h_ref[...], mixg_ref[...]).astype(BF16)

    xp_scr[POOL_HALO:POOL_HALO + tm, :] = _dot(u_scr[...], w_in_ref[:, 0:pool_width])
    q32_scr[...] = _dot(u_scr[...], w_in_ref[:, off_q:off_q + attn_width])
    kv_proj = _dot_rows_split(u_scr[...], w_in_ref[:, off_k:off_k + 2 * kv_width])

    t = tile_in_seq * tm + lax.broadcasted_iota(jnp.int32, (tm, 1), 0)
    pooled = []
    for gi, w in enumerate(POOL_WINDOWS):
        cols = slice(gi * POOL_GROUP, (gi + 1) * POOL_GROUP)
        ext = xp_scr[:, cols]
        wsum = ext
        shift = 1
        while shift < w:
            wsum = wsum + pltpu.roll(wsum, shift, axis=0)
            shift *= 2
        count = jnp.minimum(t + 1, w).astype(F32)
        pooled.append((wsum[POOL_HALO:] / count - ext[POOL_HALO:]).astype(BF16))
    for c in range(len(POOL_WINDOWS) // 2):
        cols = slice(c * 2 * POOL_GROUP, (c + 1) * 2 * POOL_GROUP)
        pair = jnp.concatenate(pooled[2 * c:2 * c + 2], axis=1)
        mixed_scr[:, cols] = (_dot(pair, poolw_ref[c]) * pscale_ref[:, cols]).astype(BF16)

    q_gain = qg_ref[...] * (HEAD_DIM ** -0.5 * LOG2E)
    n_gate_chunks = d_model // MIX_COL_CHUNK
    q_chunks_per_gate_chunk = attn_width // LANES // n_gate_chunks
    for c in range(n_gate_chunks):
        cols = slice(c * MIX_COL_CHUNK, (c + 1) * MIX_COL_CHUNK)
        cols_a = slice(d_model + c * MIX_COL_CHUNK, d_model + (c + 1) * MIX_COL_CHUNK)
        gp = jax.nn.sigmoid(
            _dot(u_scr[...], w_in_ref[:, off_gp + c * MIX_COL_CHUNK: off_gp + (c + 1) * MIX_COL_CHUNK])
            + gbias_ref[:, cols])
        gate_scr[:, cols] = gp
        gate_scr[:, cols_a] = jax.nn.sigmoid(
            _dot(u_scr[...], w_in_ref[:, off_ga + c * MIX_COL_CHUNK: off_ga + (c + 1) * MIX_COL_CHUNK])
            + gbias_ref[:, cols_a])
        for qc in range(c * q_chunks_per_gate_chunk, (c + 1) * q_chunks_per_gate_chunk):
            qcols = slice(qc * LANES, (qc + 1) * LANES)
            q = q32_scr[:, qcols]
            ms = _head_mean_sq(q, low_half)
            q_scr[:, qcols] = (q * lax.rsqrt(ms + RMS_EPS) * q_gain).astype(BF16)

    k = kv_proj[:, 0:kv_width]
    k = k * lax.rsqrt(_head_mean_sq(k, low_half) + RMS_EPS) * kg_ref[...]
    ka0, kb0, ka1, kb1 = _split_kv_heads(k)
    ka_scr[0, BLOCK:BLOCK + tm, :] = ka0
    kb_scr[0, BLOCK:BLOCK + tm, :] = kb0
    ka_scr[1, BLOCK:BLOCK + tm, :] = ka1
    kb_scr[1, BLOCK:BLOCK + tm, :] = kb1
    va0, vb0, va1, vb1 = _split_kv_heads(kv_proj[:, kv_width:2 * kv_width])
    ve_scr[0, BLOCK:BLOCK + tm, 0:LANES] = va0
    vo_scr[0, BLOCK:BLOCK + tm, 0:LANES] = vb0
    ve_scr[1, BLOCK:BLOCK + tm, 0:LANES] = va1
    vo_scr[1, BLOCK:BLOCK + tm, 0:LANES] = vb1

    key = lax.broadcasted_iota(jnp.int32, (2 * BLOCK, BLOCK), 0)
    qry = lax.broadcasted_iota(jnp.int32, (2 * BLOCK, BLOCK), 1)
    band = (key > qry) & (key <= qry + BLOCK)
    first_key = jnp.where(seq_start, BLOCK, 0)
    bias = jnp.where(band, 0.0, MASK_BIAS).astype(BF16)
    bias_first = jnp.where(band & (key >= first_key), 0.0, MASK_BIAS).astype(BF16)
    r4 = lax.broadcasted_iota(jnp.int32, (PAIRS_PER_KV * BLOCK, BLOCK), 0) % BLOCK
    c4 = lax.broadcasted_iota(jnp.int32, (PAIRS_PER_KV * BLOCK, BLOCK), 1)
    onehot = jnp.where(r4 == c4, 1.0, 0.0).astype(BF16)
    sinks2 = sinks_ref[...] * LOG2E

    def score_unit(unit):
        b, kv = divmod(unit, N_KV_HEADS)
        rows = slice(b * BLOCK, (b + 1) * BLOCK)
        keys = slice(b * BLOCK, (b + 2) * BLOCK)
        key_bias = bias_first if b == 0 else bias
        base = kv * PAIRS_PER_KV
        qs = jnp.concatenate([q_scr[rows, (base + j) * LANES:(base + j + 1) * LANES]
                              for j in range(PAIRS_PER_KV)], axis=0)
        qs = jnp.concatenate([qs, onehot], axis=1)
        k_even = jnp.concatenate([ka_scr[kv, keys, :], key_bias], axis=1)
        k_odd = jnp.concatenate([kb_scr[kv, keys, :], key_bias], axis=1)
        return _dot_nt(qs, k_even), _dot_nt(qs, k_odd)

    def finish_unit(unit, scores):
        b, kv = divmod(unit, N_KV_HEADS)
        rows = slice(b * BLOCK, (b + 1) * BLOCK)
        keys = slice(b * BLOCK, (b + 2) * BLOCK)
        base = kv * PAIRS_PER_KV
        probs = ([], [])
        sink_terms = []
        for j in range(PAIRS_PER_KV):
            m = []
            for par in range(2):
                s = scores[par][j * BLOCK:(j + 1) * BLOCK]
                m.append(jnp.max(s, axis=-1, keepdims=True))
                probs[par].append(jnp.exp2(s - m[par]).astype(BF16))
            sink_terms.append(jnp.exp2(sinks2[base + j:base + j + 1, :]
                                       - jnp.where(low_half, m[0], m[1])))
        p = jnp.concatenate([jnp.concatenate(probs[0], axis=0),
                             jnp.concatenate(probs[1], axis=0)], axis=1)
        v_stack = jnp.concatenate([ve_scr[kv, keys, :], vo_scr[kv, keys, :]], axis=0)
        acc = _dot_rows_split(p, v_stack)
        den = acc[:, LANES:2 * LANES] + jnp.concatenate(sink_terms, axis=0)
        out = (acc[:, 0:LANES] / den).astype(BF16)
        for j in range(PAIRS_PER_KV):
            attn_scr[rows, (base + j) * LANES:(base + j + 1) * LANES] = out[j * BLOCK:(j + 1) * BLOCK]

    half = tm // 2

    def pool_piece(r0, c):
        rows = slice(r0, r0 + half)
        cols = slice(c * MIX_COL_CHUNK, (c + 1) * MIX_COL_CHUNK)
        gate_scr[rows, cols] = gate_scr[rows, cols] * _dot(mixed_scr[rows, :], wpo_ref[:, cols])

    def merge_piece(r0, c):
        rows = slice(r0, r0 + half)
        cols = slice(c * MIX_COL_CHUNK, (c + 1) * MIX_COL_CHUNK)
        cols_a = slice(d_model + c * MIX_COL_CHUNK, d_model + (c + 1) * MIX_COL_CHUNK)
        ba = _dot(attn_scr[rows, :], wao_ref[:, cols])
        merged_scr[rows, cols] = (gate_scr[rows, cols] + gate_scr[rows, cols_a] * ba).astype(BF16)

    def out_piece(r0, c):
        rows = slice(r0, r0 + half)
        cols = slice(c * MIX_COL_CHUNK, (c + 1) * MIX_COL_CHUNK)
        o_ref[rows, cols] = h_ref[rows, cols] + _dot(merged_scr[rows, :], wout_ref[:, cols])

    def output_pieces(r0):
        return ([functools.partial(merge_piece, r0, c) for c in range(n_gate_chunks)]
                + [functools.partial(out_piece, r0, c) for c in range(n_gate_chunks)])

    n_units = (tm // BLOCK) * N_KV_HEADS
    dense = ([functools.partial(pool_piece, r0, c) for r0 in (0, half) for c in range(n_gate_chunks)]
             + output_pieces(0))
    assert len(dense) == n_units
    scores = score_unit(0)
    for unit in range(n_units):
        next_scores = score_unit(unit + 1) if unit + 1 < n_units else None
        dense[unit]()
        finish_unit(unit, scores)
        scores = next_scores
    for piece in output_pieces(half):
        piece()
    _cast_slabs(cast_in, cast_out)


def _pair_blockdiag(pool_w):
    g, n, _ = pool_w.shape
    z = jnp.zeros((n, n), pool_w.dtype)
    return jnp.stack([jnp.block([[pool_w[2 * c], z], [z, pool_w[2 * c + 1]]])
                      for c in range(g // 2)])


def _mixer(h, seq_len, mix_norm, w_in, pool_w, pool_scale, w_pool_out, q_norm, k_norm, sinks,
           w_attn_out, gate_bias, w_out, *, cast_for_later=()):
    n, d = h.shape
    tm = MIX_TILE
    pool_width = w_pool_out.shape[0]
    attn_width = w_attn_out.shape[0]
    kv_width = N_KV_HEADS * HEAD_DIM
    in_width = w_in.shape[1]
    assert seq_len % tm == 0 and tm % BLOCK == 0
    assert pool_width == len(POOL_WINDOWS) * POOL_GROUP and attn_width == N_HEADS * HEAD_DIM
    assert in_width == pool_width + attn_width + 2 * kv_width + 2 * d and kv_width == LANES
    row_spec = pl.BlockSpec((tm, d), lambda i: (i, 0))
    cast_in, cast_out, cast_shapes = _slab_specs(cast_for_later, n // tm)
    kern = functools.partial(_mixer_kernel, tiles_per_seq=seq_len // tm, d_model=d,
                             pool_width=pool_width, attn_width=attn_width, kv_width=kv_width,
                             n_cast=len(cast_for_later))
    pool_w2 = _pair_blockdiag(pool_w).astype(BF16)
    sink_rows = jnp.repeat(sinks.reshape(N_HEADS // 2, 2), HEAD_DIM, axis=1)
    return pl.pallas_call(
        kern,
        grid=(n // tm,),
        in_specs=[row_spec, _resident((1, d)), _resident((d, in_width)), _resident((1, 2 * d)),
                  _resident(pool_w2.shape), _resident((1, pool_width)), _resident((pool_width, d)),
                  _resident((1, LANES)), _resident((1, LANES)), _resident(sink_rows.shape),
                  _resident((attn_width, d)), _resident((d, d))] + cast_in,
        out_specs=[row_spec] + cast_out,
        out_shape=[jax.ShapeDtypeStruct((n, d), F32)] + cast_shapes,
        scratch_shapes=[
            pltpu.VMEM((tm, d), BF16),
            pltpu.VMEM((POOL_HALO + tm, pool_width), F32),
            pltpu.VMEM((tm, attn_width), F32),
            pltpu.VMEM((tm, attn_width), BF16),
            pltpu.VMEM((N_KV_HEADS, BLOCK + tm, LANES), BF16),
            pltpu.VMEM((N_KV_HEADS, BLOCK + tm, LANES), BF16),
            pltpu.VMEM((N_KV_HEADS, BLOCK + tm, 2 * LANES), BF16),
            pltpu.VMEM((N_KV_HEADS, BLOCK + tm, 2 * LANES), BF16),
            pltpu.VMEM((tm, attn_width), BF16),
            pltpu.VMEM((tm, pool_width), BF16),
            pltpu.VMEM((tm, 2 * d), F32),
            pltpu.VMEM((tm, d), BF16),
        ],
        compiler_params=pltpu.CompilerParams(dimension_semantics=("arbitrary",),
                                             vmem_limit_bytes=VMEM_LIMIT),
        name="mixer",
    )(h, mix_norm.reshape(1, d), w_in, gate_bias.reshape(1, 2 * d),
      pool_w2, pool_scale.reshape(1, pool_width), w_pool_out,
      jnp.tile(q_norm, LANES // HEAD_DIM).reshape(1, LANES),
      jnp.tile(k_norm, LANES // HEAD_DIM).reshape(1, LANES), sink_rows,
      w_attn_out, w_out, *cast_for_later)


def kernel(x, ffn1_norm, ffn1_w_gate, ffn1_w_up, ffn1_w_down, mix_norm, w_in, pool_w, pool_scale,
           w_pool_out, q_norm, k_norm, sinks, w_attn_out, gate_bias, w_out, ffn2_norm, ffn2_w_gate,
           ffn2_w_up, ffn2_w_down):
    b, s, d = x.shape
    h = x.reshape(b * s, d)
    h, w_in_b, w_pool_out_b, w_attn_out_b, w_out_b = _ffn(
        h, ffn1_norm, ffn1_w_gate, ffn1_w_up, ffn1_w_down, name="ffn1",
        cast_for_later=(w_in, w_pool_out, w_attn_out, w_out))
    (h,) = _mixer(h, s, mix_norm, w_in_b, pool_w, pool_scale, w_pool_out_b, q_norm, k_norm, sinks,
                  w_attn_out_b, gate_bias, w_out_b)
    (h,) = _ffn(h, ffn2_norm, ffn2_w_gate, ffn2_w_up, ffn2_w_down, name="ffn2")
    return h.reshape(b, s, d)
```

```python
import functools
import math

import jax
import jax.numpy as jnp
from jax import lax
from jax.experimental import pallas as pl
from jax.experimental.pallas import tpu as pltpu

F32 = jnp.float32
BF16 = jnp.bfloat16

RMS_EPS = 1e-6
HEAD_DIM = 64
N_HEADS = 16
N_KV_HEADS = 2
PAIRS_PER_KV = N_HEADS // N_KV_HEADS // 2
BLOCK = 128
POOL_WINDOWS = (2, 4, 8, 16)
POOL_GROUP = 128
POOL_HALO = 16
LANES = 128
BF16_SUBLANES = 16
V7X_VMEM_BYTES = 64 * 1024 * 1024
VMEM_LIMIT = V7X_VMEM_BYTES - 4 * 1024 * 1024
LOG2E = math.log2(math.e)
MASK_BIAS = -3.0e38

FFN_TILE = 1024
FFN_CHUNK = 256
FFN_DOWN_GROUP = 2
FFN_WEIGHT_SLABS = 16
FFN_STAGE_SLOTS = 3
MIX_TILE = 1024
MIX_COL_CHUNK = 256


def _dot(a, b):
    return jnp.dot(a, b, preferred_element_type=F32)


def _dot_rows_split(a, b):
    half = a.shape[0] // 2
    return jnp.concatenate([_dot(a[:half], b), _dot(a[half:], b)], axis=0)


def _dot_nt(a, b):
    return lax.dot_general(a, b, (((1,), (1,)), ((), ())), preferred_element_type=F32)


def _rmsnorm_rows(x, g):
    ms = jnp.mean(x * x, axis=-1, keepdims=True)
    return x * lax.rsqrt(ms + RMS_EPS) * g


def _cast_slabs(f32_refs, bf16_refs):
    for src, dst in zip(f32_refs, bf16_refs, strict=True):
        dst[...] = src[...].astype(BF16)


def _ffn_kernel(x_ref, g_ref, wg_hbm, wu_hbm, wd_hbm, *rest, d_ff, chunk, n_cast, n_slabs):
    cast_in, o_ref, cast_out = rest[:n_cast], rest[n_cast], rest[n_cast + 1:2 * n_cast + 1]
    wg_v, wu_v, wd_v, stage_g, stage_u, stage_d, sems, acc_ref = rest[2 * n_cast + 1:]
    n_slots = stage_g.shape[0]
    streams = ((wg_hbm, stage_g, wg_v), (wu_hbm, stage_u, wu_v), (wd_hbm, stage_d, wd_v))

    def slab_copies(s):
        slot = s % n_slots
        return [pltpu.make_async_copy(hbm.at[pl.ds(s * stage.shape[1], stage.shape[1]), :],
                                      stage.at[slot], sems.at[k, slot])
                for k, (hbm, stage, _) in enumerate(streams)]

    @pl.when(pl.program_id(0) == 0)
    def _():
        for s in range(min(n_slots, n_slabs)):
            for cp in slab_copies(s):
                cp.start()
        for s in range(n_slabs):
            for cp in slab_copies(s):
                cp.wait()
            for _, stage, dst in streams:
                rows = stage.shape[1]
                dst[s * rows:(s + 1) * rows, :] = stage[s % n_slots].astype(BF16)
            if s + n_slots < n_slabs:
                for cp in slab_copies(s + n_slots):
                    cp.start()

    x = x_ref[...]
    xn = _rmsnorm_rows(x, g_ref[...]).astype(BF16)
    n_chunks = d_ff // chunk
    for first in range(0, n_chunks, FFN_DOWN_GROUP):
        group = range(first, min(first + FFN_DOWN_GROUP, n_chunks))
        acts = []
        for c in group:
            cols = slice(c * chunk, (c + 1) * chunk)
            g = _dot(xn, wg_v[:, cols])
            u = _dot(xn, wu_v[:, cols])
            acts.append((g * jax.nn.sigmoid(g) * u).astype(BF16))
        d = _dot(jnp.concatenate(acts, axis=1), wd_v[group[0] * chunk:(group[-1] + 1) * chunk, :])
        if first == 0:
            acc_ref[...] = d
        else:
            acc_ref[...] += d
    o_ref[...] = x + 0.5 * acc_ref[...]
    _cast_slabs(cast_in, cast_out)


def _resident(shape):
    return pl.BlockSpec(shape, lambda i: (0,) * len(shape), pipeline_mode=pl.Buffered(1))


def _slab_specs(weights, steps):
    in_specs, out_specs, out_shapes = [], [], []
    for w in weights:
        rows, cols = w.shape
        assert rows % (steps * BF16_SUBLANES) == 0
        in_specs.append(pl.BlockSpec((rows // steps, cols), lambda i: (i, 0)))
        out_specs.append(pl.BlockSpec((rows // steps, cols), lambda i: (i, 0)))
        out_shapes.append(jax.ShapeDtypeStruct(w.shape, BF16))
    return in_specs, out_specs, out_shapes


def _ffn(x, norm_g, w_gate, w_up, w_down, *, name, cast_for_later=()):
    n, d = x.shape
    d_ff = w_gate.shape[1]
    assert n % FFN_TILE == 0 and d_ff % FFN_CHUNK == 0
    assert d % (FFN_WEIGHT_SLABS * BF16_SUBLANES) == 0 and d_ff % (FFN_WEIGHT_SLABS * BF16_SUBLANES) == 0
    steps = n // FFN_TILE
    row_spec = pl.BlockSpec((FFN_TILE, d), lambda i: (i, 0))
    in_hbm = pl.BlockSpec(memory_space=pl.ANY)
    cast_in, cast_out, cast_shapes = _slab_specs(cast_for_later, steps)
    return pl.pallas_call(
        functools.partial(_ffn_kernel, d_ff=d_ff, chunk=FFN_CHUNK, n_cast=len(cast_for_later),
                          n_slabs=FFN_WEIGHT_SLABS),
        grid=(steps,),
        in_specs=[row_spec, _resident((1, d)), in_hbm, in_hbm, in_hbm] + cast_in,
        out_specs=[row_spec] + cast_out,
        out_shape=[jax.ShapeDtypeStruct((n, d), F32)] + cast_shapes,
        scratch_shapes=[
            pltpu.VMEM((d, d_ff), BF16),
            pltpu.VMEM((d, d_ff), BF16),
            pltpu.VMEM((d_ff, d), BF16),
            pltpu.VMEM((FFN_STAGE_SLOTS, d // FFN_WEIGHT_SLABS, d_ff), F32),
            pltpu.VMEM((FFN_STAGE_SLOTS, d // FFN_WEIGHT_SLABS, d_ff), F32),
            pltpu.VMEM((FFN_STAGE_SLOTS, d_ff // FFN_WEIGHT_SLABS, d), F32),
            pltpu.SemaphoreType.DMA((3, FFN_STAGE_SLOTS)),
            pltpu.VMEM((FFN_TILE, d), F32),
        ],
        compiler_params=pltpu.CompilerParams(dimension_semantics=("arbitrary",),
                                             vmem_limit_bytes=VMEM_LIMIT),
        name=name,
    )(x, norm_g.reshape(1, d), w_gate, w_up, w_down, *cast_for_later)


def _head_mean_sq(x, low_half):
    x2 = x * x
    s_low = jnp.sum(jnp.where(low_half, x2, 0.0), axis=-1, keepdims=True)
    s_all = jnp.sum(x2, axis=-1, keepdims=True)
    return jnp.where(low_half, s_low, s_all - s_low) * (1.0 / HEAD_DIM)


def _split_kv_heads(x):
    swapped = pltpu.roll(x, HEAD_DIM, axis=1)
    low = lax.broadcasted_iota(jnp.int32, x.shape, 1) < HEAD_DIM
    zero = jnp.zeros_like(x)
    parts = (jnp.where(low, x, zero), jnp.where(low, zero, swapped),
             jnp.where(low, swapped, zero), jnp.where(low, zero, x))
    return tuple(p.astype(BF16) for p in parts)


def _mixer_kernel(h_ref, mixg_ref, w_in_ref, gbias_ref, poolw_ref, pscale_ref, wpo_ref,
                  qg_ref, kg_ref, sinks_ref, wao_ref, wout_ref, *rest,
                  tiles_per_seq, d_model, pool_width, attn_width, kv_width, n_cast):
    cast_in, o_ref, cast_out = rest[:n_cast], rest[n_cast], rest[n_cast + 1:2 * n_cast + 1]
    (u_scr, xp_scr, q32_scr, q_scr, ka_scr, kb_scr, ve_scr, vo_scr, attn_scr, mixed_scr,
     gate_scr, merged_scr) = rest[2 * n_cast + 1:]
    tm = h_ref.shape[0]
    i = pl.program_id(0)
    tile_in_seq = i % tiles_per_seq
    seq_start = tile_in_seq == 0
    lane128 = lax.broadcasted_iota(jnp.int32, (1, LANES), 1)
    low_half = lane128 < HEAD_DIM

    off_q = pool_width
    off_k = off_q + attn_width
    off_gp = off_k + 2 * kv_width
    off_ga = off_gp + d_model

    @pl.when(seq_start)
    def _():
        xp_scr[0:POOL_HALO, :] = jnp.zeros((POOL_HALO, pool_width), F32)
        zk = jnp.zeros((BLOCK, LANES), BF16)
        ones_e = jnp.broadcast_to(jnp.where(low_half, 1.0, 0.0).astype(BF16), (BLOCK + tm, LANES))
        ones_o = jnp.broadcast_to(jnp.where(low_half, 0.0, 1.0).astype(BF16), (BLOCK + tm, LANES))
        for kv in range(N_KV_HEADS):
            ka_scr[kv, 0:BLOCK, :] = zk
            kb_scr[kv, 0:BLOCK, :] = zk
            ve_scr[kv, 0:BLOCK, 0:LANES] = zk
            vo_scr[kv, 0:BLOCK, 0:LANES] = zk
            ve_scr[kv, :, LANES:2 * LANES] = ones_e
            vo_scr[kv, :, LANES:2 * LANES] = ones_o

    @pl.when(jnp.logical_not(seq_start))
    def _():
        xp_scr[0:POOL_HALO, :] = xp_scr[tm:tm + POOL_HALO, :]
        for kv in range(N_KV_HEADS):
            ka_scr[kv, 0:BLOCK, :] = ka_scr[kv, tm:tm + BLOCK, :]
            kb_scr[kv, 0:BLOCK, :] = kb_scr[kv, tm:tm + BLOCK, :]
            ve_scr[kv, 0:BLOCK, 0:LANES] = ve_scr[kv, tm:tm + BLOCK, 0:LANES]
            vo_scr[kv, 0:BLOCK, 0:LANES] = vo_scr[kv, tm:tm + BLOCK, 0:LANES]

    u_scr[...] = _rmsnorm_rows(h_ref[...], mixg_ref[...]).astype(BF16)

    xp_scr[POOL_HALO:POOL_HALO + tm, :] = _dot(u_scr[...], w_in_ref[:, 0:pool_width])
    q32_scr[...] = _dot(u_scr[...], w_in_ref[:, off_q:off_q + attn_width])
    kv_proj = _dot_rows_split(u_scr[...], w_in_ref[:, off_k:off_k + 2 * kv_width])

    t = tile_in_seq * tm + lax.broadcasted_iota(jnp.int32, (tm, 1), 0)
    pooled = []
    for gi, w in enumerate(POOL_WINDOWS):
        cols = slice(gi * POOL_GROUP, (gi + 1) * POOL_GROUP)
        ext = xp_scr[:, cols]
        wsum = ext
        shift = 1
        while shift < w:
            wsum = wsum + pltpu.roll(wsum, shift, axis=0)
            shift *= 2
        count = jnp.minimum(t + 1, w).astype(F32)
        pooled.append((wsum[POOL_HALO:] / count - ext[POOL_HALO:]).astype(BF16))
    for c in range(len(POOL_WINDOWS) // 2):
        cols = slice(c * 2 * POOL_GROUP, (c + 1) * 2 * POOL_GROUP)
        pair = jnp.concatenate(pooled[2 * c:2 * c + 2], axis=1)
        mixed_scr[:, cols] = (_dot(pair, poolw_ref[c]) * pscale_ref[:, cols]).astype(BF16)

    q_gain = qg_ref[...] * (HEAD_DIM ** -0.5 * LOG2E)
    n_gate_chunks = d_model // MIX_COL_CHUNK
    q_chunks_per_gate_chunk = attn_width // LANES // n_gate_chunks
    for c in range(n_gate_chunks):
        cols = slice(c * MIX_COL_CHUNK, (c + 1) * MIX_COL_CHUNK)
        cols_a = slice(d_model + c * MIX_COL_CHUNK, d_model + (c + 1) * MIX_COL_CHUNK)
        gp = jax.nn.sigmoid(
            _dot(u_scr[...], w_in_ref[:, off_gp + c * MIX_COL_CHUNK: off_gp + (c + 1) * MIX_COL_CHUNK])
            + gbias_ref[:, cols])
        gate_scr[:, cols] = gp
        gate_scr[:, cols_a] = jax.nn.sigmoid(
            _dot(u_scr[...], w_in_ref[:, off_ga + c * MIX_COL_CHUNK: off_ga + (c + 1) * MIX_COL_CHUNK])
            + gbias_ref[:, cols_a])
        for qc in range(c * q_chunks_per_gate_chunk, (c + 1) * q_chunks_per_gate_chunk):
            qcols = slice(qc * LANES, (qc + 1) * LANES)
            q = q32_scr[:, qcols]
            ms = _head_mean_sq(q, low_half)
            q_scr[:, qcols] = (q * lax.rsqrt(ms + RMS_EPS) * q_gain).astype(BF16)

    k = kv_proj[:, 0:kv_width]
    k = k * lax.rsqrt(_head_mean_sq(k, low_half) + RMS_EPS) * kg_ref[...]
    ka0, kb0, ka1, kb1 = _split_kv_heads(k)
    ka_scr[0, BLOCK:BLOCK + tm, :] = ka0
    kb_scr[0, BLOCK:BLOCK + tm, :] = kb0
    ka_scr[1, BLOCK:BLOCK + tm, :] = ka1
    kb_scr[1, BLOCK:BLOCK + tm, :] = kb1
    va0, vb0, va1, vb1 = _split_kv_heads(kv_proj[:, kv_width:2 * kv_width])
    ve_scr[0, BLOCK:BLOCK + tm, 0:LANES] = va0
    vo_scr[0, BLOCK:BLOCK + tm, 0:LANES] = vb0
    ve_scr[1, BLOCK:BLOCK + tm, 0:LANES] = va1
    vo_scr[1, BLOCK:BLOCK + tm, 0:LANES] = vb1

    key = lax.broadcasted_iota(jnp.int32, (2 * BLOCK, BLOCK), 0)
    qry = lax.broadcasted_iota(jnp.int32, (2 * BLOCK, BLOCK), 1)
    band = (key > qry) & (key <= qry + BLOCK)
    first_key = jnp.where(seq_start, BLOCK, 0)
    bias = jnp.where(band, 0.0, MASK_BIAS).astype(BF16)
    bias_first = jnp.where(band & (key >= first_key), 0.0, MASK_BIAS).astype(BF16)
    r4 = lax.broadcasted_iota(jnp.int32, (PAIRS_PER_KV * BLOCK, BLOCK), 0) % BLOCK
    c4 = lax.broadcasted_iota(jnp.int32, (PAIRS_PER_KV * BLOCK, BLOCK), 1)
    onehot = jnp.where(r4 == c4, 1.0, 0.0).astype(BF16)
    sinks2 = sinks_ref[...] * LOG2E

    def score_unit(unit):
        b, kv = divmod(unit, N_KV_HEADS)
        rows = slice(b * BLOCK, (b + 1) * BLOCK)
        keys = slice(b * BLOCK, (b + 2) * BLOCK)
        key_bias = bias_first if b == 0 else bias
        base = kv * PAIRS_PER_KV
        qs = jnp.concatenate([q_scr[rows, (base + j) * LANES:(base + j + 1) * LANES]
                              for j in range(PAIRS_PER_KV)], axis=0)
        qs = jnp.concatenate([qs, onehot], axis=1)
        k_even = jnp.concatenate([ka_scr[kv, keys, :], key_bias], axis=1)
        k_odd = jnp.concatenate([kb_scr[kv, keys, :], key_bias], axis=1)
        return _dot_nt(qs, k_even), _dot_nt(qs, k_odd)

    def finish_unit(unit, scores):
        b, kv = divmod(unit, N_KV_HEADS)
        rows = slice(b * BLOCK, (b + 1) * BLOCK)
        keys = slice(b * BLOCK, (b + 2) * BLOCK)
        base = kv * PAIRS_PER_KV
        probs = ([], [])
        sink_terms = []
        for j in range(PAIRS_PER_KV):
            m = []
            for par in range(2):
                s = scores[par][j * BLOCK:(j + 1) * BLOCK]
                m.append(jnp.max(s, axis=-1, keepdims=True))
                probs[par].append(jnp.exp2(s - m[par]).astype(BF16))
            sink_terms.append(jnp.exp2(sinks2[base + j:base + j + 1, :]
                                       - jnp.where(low_half, m[0], m[1])))
        p = jnp.concatenate([jnp.concatenate(probs[0], axis=0),
                             jnp.concatenate(probs[1], axis=0)], axis=1)
        v_stack = jnp.concatenate([ve_scr[kv, keys, :], vo_scr[kv, keys, :]], axis=0)
        acc = _dot_rows_split(p, v_stack)
        den = acc[:, LANES:2 * LANES] + jnp.concatenate(sink_terms, axis=0)
        out = (acc[:, 0:LANES] / den).astype(BF16)
        for j in range(PAIRS_PER_KV):
            attn_scr[rows, (base + j) * LANES:(base + j + 1) * LANES] = out[j * BLOCK:(j + 1) * BLOCK]

    half = tm // 2

    def pool_piece(r0, c):
        rows = slice(r0, r0 + half)
        cols = slice(c * MIX_COL_CHUNK, (c + 1) * MIX_COL_CHUNK)
        gate_scr[rows, cols] = gate_scr[rows, cols] * _dot(mixed_scr[rows, :], wpo_ref[:, cols])

    def merge_piece(r0, c):
        rows = slice(r0, r0 + half)
        cols = slice(c * MIX_COL_CHUNK, (c + 1) * MIX_COL_CHUNK)
        cols_a = slice(d_model + c * MIX_COL_CHUNK, d_model + (c + 1) * MIX_COL_CHUNK)
        ba = _dot(attn_scr[rows, :], wao_ref[:, cols])
        merged_scr[rows, cols] = (gate_scr[rows, cols] + gate_scr[rows, cols_a] * ba).astype(BF16)

    def out_piece(r0, c):
        rows = slice(r0, r0 + half)
        cols = slice(c * MIX_COL_CHUNK, (c + 1) * MIX_COL_CHUNK)
        o_ref[rows, cols] = h_ref[rows, cols] + _dot(merged_scr[rows, :], wout_ref[:, cols])

    def output_pieces(r0):
        return ([functools.partial(merge_piece, r0, c) for c in range(n_gate_chunks)]
                + [functools.partial(out_piece, r0, c) for c in range(n_gate_chunks)])

    n_units = (tm // BLOCK) * N_KV_HEADS
    dense = ([functools.partial(pool_piece, r0, c) for r0 in (0, half) for c in range(n_gate_chunks)]
             + output_pieces(0))
    assert len(dense) == n_units
    scores = score_unit(0)
    for unit in range(n_units):
        next_scores = score_unit(unit + 1) if unit + 1 < n_units else None
        dense[unit]()
        finish_unit(unit, scores)
        scores = next_scores
    for piece in output_pieces(half):
        piece()
    _cast_slabs(cast_in, cast_out)


def _pair_blockdiag(pool_w):
    g, n, _ = pool_w.shape
    z = jnp.zeros((n, n), pool_w.dtype)
    return jnp.stack([jnp.block([[pool_w[2 * c], z], [z, pool_w[2 * c + 1]]])
                      for c in range(g // 2)])


def _mixer(h, seq_len, mix_norm, w_in, pool_w, pool_scale, w_pool_out, q_norm, k_norm, sinks,
           w_attn_out, gate_bias, w_out, *, cast_for_later=()):
    n, d = h.shape
    tm = MIX_TILE
    pool_width = w_pool_out.shape[0]
    attn_width = w_attn_out.shape[0]
    kv_width = N_KV_HEADS * HEAD_DIM
    in_width = w_in.shape[1]
    assert seq_len % tm == 0 and tm % BLOCK == 0
    assert pool_width == len(POOL_WINDOWS) * POOL_GROUP and attn_width == N_HEADS * HEAD_DIM
    assert in_width == pool_width + attn_width + 2 * kv_width + 2 * d and kv_width == LANES
    row_spec = pl.BlockSpec((tm, d), lambda i: (i, 0))
    cast_in, cast_out, cast_shapes = _slab_specs(cast_for_later, n // tm)
    kern = functools.partial(_mixer_kernel, tiles_per_seq=seq_len // tm, d_model=d,
                             pool_width=pool_width, attn_width=attn_width, kv_width=kv_width,
                             n_cast=len(cast_for_later))
    pool_w2 = _pair_blockdiag(pool_w).astype(BF16)
    sink_rows = jnp.repeat(sinks.reshape(N_HEADS // 2, 2), HEAD_DIM, axis=1)
    return pl.pallas_call(
        kern,
        grid=(n // tm,),
        in_specs=[row_spec, _resident((1, d)), _resident((d, in_width)), _resident((1, 2 * d)),
                  _resident(pool_w2.shape), _resident((1, pool_width)), _resident((pool_width, d)),
                  _resident((1, LANES)), _resident((1, LANES)), _resident(sink_rows.shape),
                  _resident((attn_width, d)), _resident((d, d))] + cast_in,
        out_specs=[row_spec] + cast_out,
        out_shape=[jax.ShapeDtypeStruct((n, d), F32)] + cast_shapes,
        scratch_shapes=[
            pltpu.VMEM((tm, d), BF16),
            pltpu.VMEM((POOL_HALO + tm, pool_width), F32),
            pltpu.VMEM((tm, attn_width), F32),
            pltpu.VMEM((tm, attn_width), BF16),
            pltpu.VMEM((N_KV_HEADS, BLOCK + tm, LANES), BF16),
            pltpu.VMEM((N_KV_HEADS, BLOCK + tm, LANES), BF16),
            pltpu.VMEM((N_KV_HEADS, BLOCK + tm, 2 * LANES), BF16),
            pltpu.VMEM((N_KV_HEADS, BLOCK + tm, 2 * LANES), BF16),
            pltpu.VMEM((tm, attn_width), BF16),
            pltpu.VMEM((tm, pool_width), BF16),
            pltpu.VMEM((tm, 2 * d), F32),
            pltpu.VMEM((tm, d), BF16),
        ],
        compiler_params=pltpu.CompilerParams(dimension_semantics=("arbitrary",),
                                             vmem_limit_bytes=VMEM_LIMIT),
        name="mixer",
    )(h, mix_norm.reshape(1, d), w_in, gate_bias.reshape(1, 2 * d),
      pool_w2, pool_scale.reshape(1, pool_width), w_pool_out,
      jnp.tile(q_norm, LANES // HEAD_DIM).reshape(1, LANES),
      jnp.tile(k_norm, LANES // HEAD_DIM).reshape(1, LANES), sink_rows,
      w_attn_out, w_out, *cast_for_later)


def kernel(x, ffn1_norm, ffn1_w_gate, ffn1_w_up, ffn1_w_down, mix_norm, w_in, pool_w, pool_scale,
           w_pool_out, q_norm, k_norm, sinks, w_attn_out, gate_bias, w_out, ffn2_norm, ffn2_w_gate,
           ffn2_w_up, ffn2_w_down):
    b, s, d = x.shape
    h = x.reshape(b * s, d)
    h, w_in_b, w_pool_out_b, w_attn_out_b, w_out_b = _ffn(
        h, ffn1_norm, ffn1_w_gate, ffn1_w_up, ffn1_w_down, name="ffn1",
        cast_for_later=(w_in, w_pool_out, w_attn_out, w_out))
    (h,) = _mixer(h, s, mix_norm, w_in_b, pool_w, pool_scale, w_pool_out_b, q_norm, k_norm, sinks,
                  w_attn_out_b, gate_bias, w_out_b)
    (h,) = _ffn(h, ffn2_norm, ffn2_w_gate, ffn2_w_up, ffn2_w_down, name="ffn2")
    return h.reshape(b, s, d)
```

```python
import functools
import math

import jax
import jax.numpy as jnp
from jax import lax
from jax.experimental import pallas as pl
from jax.experimental.pallas import tpu as pltpu

F32 = jnp.float32
BF16 = jnp.bfloat16

RMS_EPS = 1e-6
HEAD_DIM = 64
N_HEADS = 16
N_KV_HEADS = 2
PAIRS_PER_KV = N_HEADS // N_KV_HEADS // 2
BLOCK = 128
POOL_WINDOWS = (2, 4, 8, 16)
POOL_GROUP = 128
POOL_HALO = 16
LANES = 128
BF16_SUBLANES = 16
V7X_VMEM_BYTES = 64 * 1024 * 1024
VMEM_LIMIT = V7X_VMEM_BYTES - 4 * 1024 * 1024
LOG2E = math.log2(math.e)
MASK_BIAS = -3.0e38

FFN_TILE = 1024
FFN_CHUNK = 256
FFN_DOWN_GROUP = 4
FFN_WEIGHT_SLABS = 16
FFN_STAGE_SLOTS = 3
MIX_TILE = 1024
MIX_COL_CHUNK = 256


def _dot(a, b):
    return jnp.dot(a, b, preferred_element_type=F32)


def _dot_rows_split(a, b):
    half = a.shape[0] // 2
    return jnp.concatenate([_dot(a[:half], b), _dot(a[half:], b)], axis=0)


def _dot_nt(a, b):
    return lax.dot_general(a, b, (((1,), (1,)), ((), ())), preferred_element_type=F32)


def _rmsnorm_rows(x, g):
    ms = jnp.mean(x * x, axis=-1, keepdims=True)
    return x * lax.rsqrt(ms + RMS_EPS) * g


def _cast_slabs(f32_refs, bf16_refs):
    for src, dst in zip(f32_refs, bf16_refs, strict=True):
        dst[...] = src[...].astype(BF16)


def _ffn_kernel(x_ref, g_ref, wg_hbm, wu_hbm, wd_hbm, *rest, d_ff, chunk, n_cast, n_slabs):
    cast_in, o_ref, cast_out = rest[:n_cast], rest[n_cast], rest[n_cast + 1:2 * n_cast + 1]
    wg_v, wu_v, wd_v, stage_g, stage_u, stage_d, sems, acc_ref = rest[2 * n_cast + 1:]
    n_slots = stage_g.shape[0]
    streams = ((wg_hbm, stage_g, wg_v), (wu_hbm, stage_u, wu_v), (wd_hbm, stage_d, wd_v))

    def slab_copies(s):
        slot = s % n_slots
        return [pltpu.make_async_copy(hbm.at[pl.ds(s * stage.shape[1], stage.shape[1]), :],
                                      stage.at[slot], sems.at[k, slot])
                for k, (hbm, stage, _) in enumerate(streams)]

    @pl.when(pl.program_id(0) == 0)
    def _():
        for s in range(min(n_slots, n_slabs)):
            for cp in slab_copies(s):
                cp.start()
        for s in range(n_slabs):
            for cp in slab_copies(s):
                cp.wait()
            for _, stage, dst in streams:
                rows = stage.shape[1]
                dst[s * rows:(s + 1) * rows, :] = stage[s % n_slots].astype(BF16)
            if s + n_slots < n_slabs:
                for cp in slab_copies(s + n_slots):
                    cp.start()

    x = x_ref[...]
    xn = _rmsnorm_rows(x, g_ref[...]).astype(BF16)
    n_chunks = d_ff // chunk
    for first in range(0, n_chunks, FFN_DOWN_GROUP):
        group = range(first, min(first + FFN_DOWN_GROUP, n_chunks))
        acts = []
        for c in group:
            cols = slice(c * chunk, (c + 1) * chunk)
            g = _dot(xn, wg_v[:, cols])
            u = _dot(xn, wu_v[:, cols])
            acts.append((g * jax.nn.sigmoid(g) * u).astype(BF16))
        d = _dot(jnp.concatenate(acts, axis=1), wd_v[group[0] * chunk:(group[-1] + 1) * chunk, :])
        if first == 0:
            acc_ref[...] = d
        else:
            acc_ref[...] += d
    o_ref[...] = x + 0.5 * acc_ref[...]
    _cast_slabs(cast_in, cast_out)


def _resident(shape):
    return pl.BlockSpec(shape, lambda i: (0,) * len(shape), pipeline_mode=pl.Buffered(1))


def _slab_specs(weights, steps):
    in_specs, out_specs, out_shapes = [], [], []
    for w in weights:
        rows, cols = w.shape
        assert rows % (steps * BF16_SUBLANES) == 0
        in_specs.append(pl.BlockSpec((rows // steps, cols), lambda i: (i, 0)))
        out_specs.append(pl.BlockSpec((rows // steps, cols), lambda i: (i, 0)))
        out_shapes.append(jax.ShapeDtypeStruct(w.shape, BF16))
    return in_specs, out_specs, out_shapes


def _ffn(x, norm_g, w_gate, w_up, w_down, *, name, cast_for_later=()):
    n, d = x.shape
    d_ff = w_gate.shape[1]
    assert n % FFN_TILE == 0 and d_ff % FFN_CHUNK == 0
    assert d % (FFN_WEIGHT_SLABS * BF16_SUBLANES) == 0 and d_ff % (FFN_WEIGHT_SLABS * BF16_SUBLANES) == 0
    steps = n // FFN_TILE
    row_spec = pl.BlockSpec((FFN_TILE, d), lambda i: (i, 0))
    in_hbm = pl.BlockSpec(memory_space=pl.ANY)
    cast_in, cast_out, cast_shapes = _slab_specs(cast_for_later, steps)
    return pl.pallas_call(
        functools.partial(_ffn_kernel, d_ff=d_ff, chunk=FFN_CHUNK, n_cast=len(cast_for_later),
                          n_slabs=FFN_WEIGHT_SLABS),
        grid=(steps,),
        in_specs=[row_spec, _resident((1, d)), in_hbm, in_hbm, in_hbm] + cast_in,
        out_specs=[row_spec] + cast_out,
        out_shape=[jax.ShapeDtypeStruct((n, d), F32)] + cast_shapes,
        scratch_shapes=[
            pltpu.VMEM((d, d_ff), BF16),
            pltpu.VMEM((d, d_ff), BF16),
            pltpu.VMEM((d_ff, d), BF16),
            pltpu.VMEM((FFN_STAGE_SLOTS, d // FFN_WEIGHT_SLABS, d_ff), F32),
            pltpu.VMEM((FFN_STAGE_SLOTS, d // FFN_WEIGHT_SLABS, d_ff), F32),
            pltpu.VMEM((FFN_STAGE_SLOTS, d_ff // FFN_WEIGHT_SLABS, d), F32),
            pltpu.SemaphoreType.DMA((3, FFN_STAGE_SLOTS)),
            pltpu.VMEM((FFN_TILE, d), F32),
        ],
        compiler_params=pltpu.CompilerParams(dimension_semantics=("arbitrary",),
                                             vmem_limit_bytes=VMEM_LIMIT),
        name=name,
    )(x, norm_g.reshape(1, d), w_gate, w_up, w_down, *cast_for_later)


def _head_mean_sq(x, low_half):
    x2 = x * x
    s_low = jnp.sum(jnp.where(low_half, x2, 0.0), axis=-1, keepdims=True)
    s_all = jnp.sum(x2, axis=-1, keepdims=True)
    return jnp.where(low_half, s_low, s_all - s_low) * (1.0 / HEAD_DIM)


def _split_kv_heads(x):
    swapped = pltpu.roll(x, HEAD_DIM, axis=1)
    low = lax.broadcasted_iota(jnp.int32, x.shape, 1) < HEAD_DIM
    zero = jnp.zeros_like(x)
    parts = (jnp.where(low, x, zero), jnp.where(low, zero, swapped),
             jnp.where(low, swapped, zero), jnp.where(low, zero, x))
    return tuple(p.astype(BF16) for p in parts)


def _mixer_kernel(h_ref, mixg_ref, w_in_ref, gbias_ref, poolw_ref, pscale_ref, wpo_ref,
                  lanes_ref, wao_ref, wout_ref, *rest,
                  tiles_per_seq, d_model, pool_width, attn_width, kv_width, n_cast):
    cast_in, o_ref, cast_out = rest[:n_cast], rest[n_cast], rest[n_cast + 1:2 * n_cast + 1]
    (u_scr, xp_scr, q32_scr, q_scr, ka_scr, kb_scr, ve_scr, vo_scr, attn_scr, mixed_scr,
     gate_scr, merged_scr) = rest[2 * n_cast + 1:]
    tm = h_ref.shape[0]
    i = pl.program_id(0)
    tile_in_seq = i % tiles_per_seq
    seq_start = tile_in_seq == 0
    lane128 = lax.broadcasted_iota(jnp.int32, (1, LANES), 1)
    low_half = lane128 < HEAD_DIM

    off_q = pool_width
    off_k = off_q + attn_width
    off_gp = off_k + 2 * kv_width
    off_ga = off_gp + d_model

    @pl.when(seq_start)
    def _():
        xp_scr[0:POOL_HALO, :] = jnp.zeros((POOL_HALO, pool_width), F32)
        zk = jnp.zeros((BLOCK, LANES), BF16)
        ones_e = jnp.broadcast_to(jnp.where(low_half, 1.0, 0.0).astype(BF16), (BLOCK + tm, LANES))
        ones_o = jnp.broadcast_to(jnp.where(low_half, 0.0, 1.0).astype(BF16), (BLOCK + tm, LANES))
        for kv in range(N_KV_HEADS):
            ka_scr[kv, 0:BLOCK, :] = zk
            kb_scr[kv, 0:BLOCK, :] = zk
            ve_scr[kv, 0:BLOCK, 0:LANES] = zk
            vo_scr[kv, 0:BLOCK, 0:LANES] = zk
            ve_scr[kv, :, LANES:2 * LANES] = ones_e
            vo_scr[kv, :, LANES:2 * LANES] = ones_o

    @pl.when(jnp.logical_not(seq_start))
    def _():
        xp_scr[0:POOL_HALO, :] = xp_scr[tm:tm + POOL_HALO, :]
        for kv in range(N_KV_HEADS):
            ka_scr[kv, 0:BLOCK, :] = ka_scr[kv, tm:tm + BLOCK, :]
            kb_scr[kv, 0:BLOCK, :] = kb_scr[kv, tm:tm + BLOCK, :]
            ve_scr[kv, 0:BLOCK, 0:LANES] = ve_scr[kv, tm:tm + BLOCK, 0:LANES]
            vo_scr[kv, 0:BLOCK, 0:LANES] = vo_scr[kv, tm:tm + BLOCK, 0:LANES]

    u_scr[...] = _rmsnorm_rows(h_ref[...], mixg_ref[...]).astype(BF16)

    xp_scr[POOL_HALO:POOL_HALO + tm, :] = _dot(u_scr[...], w_in_ref[:, 0:pool_width])
    q32_scr[...] = _dot(u_scr[...], w_in_ref[:, off_q:off_q + attn_width])
    kv_proj = _dot_rows_split(u_scr[...], w_in_ref[:, off_k:off_k + 2 * kv_width])

    t = tile_in_seq * tm + lax.broadcasted_iota(jnp.int32, (tm, 1), 0)
    pooled = []
    for gi, w in enumerate(POOL_WINDOWS):
        cols = slice(gi * POOL_GROUP, (gi + 1) * POOL_GROUP)
        ext = xp_scr[:, cols]
        wsum = ext
        shift = 1
        while shift < w:
            wsum = wsum + pltpu.roll(wsum, shift, axis=0)
            shift *= 2
        count = jnp.minimum(t + 1, w).astype(F32)
        pooled.append((wsum[POOL_HALO:] / count - ext[POOL_HALO:]).astype(BF16))
    for c in range(len(POOL_WINDOWS) // 2):
        cols = slice(c * 2 * POOL_GROUP, (c + 1) * 2 * POOL_GROUP)
        pair = jnp.concatenate(pooled[2 * c:2 * c + 2], axis=1)
        zeros = jnp.zeros((POOL_GROUP, POOL_GROUP), F32)
        w_pair = jnp.concatenate(
            [jnp.concatenate([poolw_ref[2 * c], zeros], axis=1),
             jnp.concatenate([zeros, poolw_ref[2 * c + 1]], axis=1)], axis=0).astype(BF16)
        mixed_scr[:, cols] = (_dot(pair, w_pair) * pscale_ref[:, cols]).astype(BF16)

    q_gain = lanes_ref[0:1, :] * (HEAD_DIM ** -0.5 * LOG2E)
    n_gate_chunks = d_model // MIX_COL_CHUNK
    q_chunks_per_gate_chunk = attn_width // LANES // n_gate_chunks
    for c in range(n_gate_chunks):
        cols = slice(c * MIX_COL_CHUNK, (c + 1) * MIX_COL_CHUNK)
        gp = jax.nn.sigmoid(
            _dot(u_scr[...], w_in_ref[:, off_gp + c * MIX_COL_CHUNK: off_gp + (c + 1) * MIX_COL_CHUNK])
            + gbias_ref[:, cols])
        gate_scr[:, cols] = gp
        for qc in range(c * q_chunks_per_gate_chunk, (c + 1) * q_chunks_per_gate_chunk):
            qcols = slice(qc * LANES, (qc + 1) * LANES)
            q = q32_scr[:, qcols]
            ms = _head_mean_sq(q, low_half)
            q_scr[:, qcols] = (q * lax.rsqrt(ms + RMS_EPS) * q_gain).astype(BF16)

    k = kv_proj[:, 0:kv_width]
    k = k * lax.rsqrt(_head_mean_sq(k, low_half) + RMS_EPS) * lanes_ref[1:2, :]
    ka0, kb0, ka1, kb1 = _split_kv_heads(k)
    ka_scr[0, BLOCK:BLOCK + tm, :] = ka0
    kb_scr[0, BLOCK:BLOCK + tm, :] = kb0
    ka_scr[1, BLOCK:BLOCK + tm, :] = ka1
    kb_scr[1, BLOCK:BLOCK + tm, :] = kb1
    va0, vb0, va1, vb1 = _split_kv_heads(kv_proj[:, kv_width:2 * kv_width])
    ve_scr[0, BLOCK:BLOCK + tm, 0:LANES] = va0
    vo_scr[0, BLOCK:BLOCK + tm, 0:LANES] = vb0
    ve_scr[1, BLOCK:BLOCK + tm, 0:LANES] = va1
    vo_scr[1, BLOCK:BLOCK + tm, 0:LANES] = vb1

    key = lax.broadcasted_iota(jnp.int32, (2 * BLOCK, BLOCK), 0)
    qry = lax.broadcasted_iota(jnp.int32, (2 * BLOCK, BLOCK), 1)
    band = (key > qry) & (key <= qry + BLOCK)
    first_key = jnp.where(seq_start, BLOCK, 0)
    bias = jnp.where(band, 0.0, MASK_BIAS).astype(BF16)
    bias_first = jnp.where(band & (key >= first_key), 0.0, MASK_BIAS).astype(BF16)
    r4 = lax.broadcasted_iota(jnp.int32, (PAIRS_PER_KV * BLOCK, BLOCK), 0) % BLOCK
    c4 = lax.broadcasted_iota(jnp.int32, (PAIRS_PER_KV * BLOCK, BLOCK), 1)
    onehot = jnp.where(r4 == c4, 1.0, 0.0).astype(BF16)
    sinks2 = lanes_ref[2:2 + N_HEADS // 2, :] * LOG2E

    def score_unit(unit):
        b, kv = divmod(unit, N_KV_HEADS)
        rows = slice(b * BLOCK, (b + 1) * BLOCK)
        keys = slice(b * BLOCK, (b + 2) * BLOCK)
        key_bias = bias_first if b == 0 else bias
        base = kv * PAIRS_PER_KV
        qs = jnp.concatenate([q_scr[rows, (base + j) * LANES:(base + j + 1) * LANES]
                              for j in range(PAIRS_PER_KV)], axis=0)
        qs = jnp.concatenate([qs, onehot], axis=1)
        k_even = jnp.concatenate([ka_scr[kv, keys, :], key_bias], axis=1)
        k_odd = jnp.concatenate([kb_scr[kv, keys, :], key_bias], axis=1)
        return _dot_nt(qs, k_even), _dot_nt(qs, k_odd)

    def finish_unit(unit, scores):
        b, kv = divmod(unit, N_KV_HEADS)
        rows = slice(b * BLOCK, (b + 1) * BLOCK)
        keys = slice(b * BLOCK, (b + 2) * BLOCK)
        base = kv * PAIRS_PER_KV
        probs = ([], [])
        sink_terms = []
        for j in range(PAIRS_PER_KV):
            m = []
            for par in range(2):
                s = scores[par][j * BLOCK:(j + 1) * BLOCK]
                m.append(jnp.max(s, axis=-1, keepdims=True))
                probs[par].append(jnp.exp2(s - m[par]).astype(BF16))
            sink_terms.append(jnp.exp2(sinks2[base + j:base + j + 1, :]
                                       - jnp.where(low_half, m[0], m[1])))
        p = jnp.concatenate([jnp.concatenate(probs[0], axis=0),
                             jnp.concatenate(probs[1], axis=0)], axis=1)
        v_stack = jnp.concatenate([ve_scr[kv, keys, :], vo_scr[kv, keys, :]], axis=0)
        acc = _dot_rows_split(p, v_stack)
        den = acc[:, LANES:2 * LANES] + jnp.concatenate(sink_terms, axis=0)
        out = (acc[:, 0:LANES] / den).astype(BF16)
        for j in range(PAIRS_PER_KV):
            attn_scr[rows, (base + j) * LANES:(base + j + 1) * LANES] = out[j * BLOCK:(j + 1) * BLOCK]

    half = tm // 2

    def pool_piece(r0, c):
        rows = slice(r0, r0 + half)
        cols = slice(c * MIX_COL_CHUNK, (c + 1) * MIX_COL_CHUNK)
        gate_scr[rows, cols] = gate_scr[rows, cols] * _dot(mixed_scr[rows, :], wpo_ref[:, cols])

    def merge_piece(r0, c):
        rows = slice(r0, r0 + half)
        cols = slice(c * MIX_COL_CHUNK, (c + 1) * MIX_COL_CHUNK)
        cols_a = slice(d_model + c * MIX_COL_CHUNK, d_model + (c + 1) * MIX_COL_CHUNK)
        ba = _dot(attn_scr[rows, :], wao_ref[:, cols])
        merged_scr[rows, cols] = (gate_scr[rows, cols] + gate_scr[rows, cols_a] * ba).astype(BF16)

    def out_piece(r0, c):
        rows = slice(r0, r0 + half)
        cols = slice(c * MIX_COL_CHUNK, (c + 1) * MIX_COL_CHUNK)
        o_ref[rows, cols] = h_ref[rows, cols] + _dot(merged_scr[rows, :], wout_ref[:, cols])

    def output_pieces(r0):
        return ([functools.partial(merge_piece, r0, c) for c in range(n_gate_chunks)]
                + [functools.partial(out_piece, r0, c) for c in range(n_gate_chunks)])

    def attn_gate_piece(r0, c):
        rows = slice(r0, r0 + half)
        cols_a = slice(d_model + c * MIX_COL_CHUNK, d_model + (c + 1) * MIX_COL_CHUNK)
        gate_scr[rows, cols_a] = jax.nn.sigmoid(
            _dot(u_scr[rows, :], w_in_ref[:, off_ga + c * MIX_COL_CHUNK: off_ga + (c + 1) * MIX_COL_CHUNK])
            + gbias_ref[:, cols_a])

    n_units = (tm // BLOCK) * N_KV_HEADS
    dense = ([(functools.partial(attn_gate_piece, r0, c), functools.partial(pool_piece, r0, c))
              for r0 in (0, half) for c in range(n_gate_chunks)]
             + [(piece,) for piece in output_pieces(0)])
    assert len(dense) == n_units
    scores = score_unit(0)
    for unit in range(n_units):
        next_scores = score_unit(unit + 1) if unit + 1 < n_units else None
        for piece in dense[unit]:
            piece()
        finish_unit(unit, scores)
        scores = next_scores
    for piece in output_pieces(half):
        piece()
    _cast_slabs(cast_in, cast_out)


def _mixer(h, seq_len, mix_norm, w_in, pool_w, pool_scale, w_pool_out, q_norm, k_norm, sinks,
           w_attn_out, gate_bias, w_out, *, cast_for_later=()):
    n, d = h.shape
    tm = MIX_TILE
    pool_width = w_pool_out.shape[0]
    attn_width = w_attn_out.shape[0]
    kv_width = N_KV_HEADS * HEAD_DIM
    in_width = w_in.shape[1]
    assert seq_len % tm == 0 and tm % BLOCK == 0
    assert pool_width == len(POOL_WINDOWS) * POOL_GROUP and attn_width == N_HEADS * HEAD_DIM
    assert in_width == pool_width + attn_width + 2 * kv_width + 2 * d and kv_width == LANES
    row_spec = pl.BlockSpec((tm, d), lambda i: (i, 0))
    cast_in, cast_out, cast_shapes = _slab_specs(cast_for_later, n // tm)
    kern = functools.partial(_mixer_kernel, tiles_per_seq=seq_len // tm, d_model=d,
                             pool_width=pool_width, attn_width=attn_width, kv_width=kv_width,
                             n_cast=len(cast_for_later))
    lane_rows = jnp.concatenate(
        [jnp.tile(q_norm, LANES // HEAD_DIM)[None], jnp.tile(k_norm, LANES // HEAD_DIM)[None],
         jnp.repeat(sinks.reshape(N_HEADS // 2, 2), HEAD_DIM, axis=1)], axis=0)
    return pl.pallas_call(
        kern,
        grid=(n // tm,),
        in_specs=[row_spec, _resident((1, d)), _resident((d, in_width)), _resident((1, 2 * d)),
                  _resident(pool_w.shape), _resident((1, pool_width)), _resident((pool_width, d)),
                  _resident(lane_rows.shape),
                  _resident((attn_width, d)), _resident((d, d))] + cast_in,
        out_specs=[row_spec] + cast_out,
        out_shape=[jax.ShapeDtypeStruct((n, d), F32)] + cast_shapes,
        scratch_shapes=[
            pltpu.VMEM((tm, d), BF16),
            pltpu.VMEM((POOL_HALO + tm, pool_width), F32),
            pltpu.VMEM((tm, attn_width), F32),
            pltpu.VMEM((tm, attn_width), BF16),
            pltpu.VMEM((N_KV_HEADS, BLOCK + tm, LANES), BF16),
            pltpu.VMEM((N_KV_HEADS, BLOCK + tm, LANES), BF16),
            pltpu.VMEM((N_KV_HEADS, BLOCK + tm, 2 * LANES), BF16),
            pltpu.VMEM((N_KV_HEADS, BLOCK + tm, 2 * LANES), BF16),
            pltpu.VMEM((tm, attn_width), BF16),
            pltpu.VMEM((tm, pool_width), BF16),
            pltpu.VMEM((tm, 2 * d), F32),
            pltpu.VMEM((tm, d), BF16),
        ],
        compiler_params=pltpu.CompilerParams(dimension_semantics=("arbitrary",),
                                             vmem_limit_bytes=VMEM_LIMIT),
        name="mixer",
    )(h, mix_norm.reshape(1, d), w_in, gate_bias.reshape(1, 2 * d),
      pool_w, pool_scale.reshape(1, pool_width), w_pool_out, lane_rows,
      w_attn_out, w_out, *cast_for_later)


def kernel(x, ffn1_norm, ffn1_w_gate, ffn1_w_up, ffn1_w_down, mix_norm, w_in, pool_w, pool_scale,
           w_pool_out, q_norm, k_norm, sinks, w_attn_out, gate_bias, w_out, ffn2_norm, ffn2_w_gate,
           ffn2_w_up, ffn2_w_down):
    b, s, d = x.shape
    h = x.reshape(b * s, d)
    h, w_in_b, w_pool_out_b, w_attn_out_b, w_out_b = _ffn(
        h, ffn1_norm, ffn1_w_gate, ffn1_w_up, ffn1_w_down, name="ffn1",
        cast_for_later=(w_in, w_pool_out, w_attn_out, w_out))
    (h,) = _mixer(h, s, mix_norm, w_in_b, pool_w, pool_scale, w_pool_out_b, q_norm, k_norm, sinks,
                  w_attn_out_b, gate_bias, w_out_b)
    (h,) = _ffn(h, ffn2_norm, ffn2_w_gate, ffn2_w_up, ffn2_w_down, name="ffn2")
    return h.reshape(b, s, d)
```

```python
import functools
import math

import jax
import jax.numpy as jnp
from jax import lax
from jax.experimental import pallas as pl
from jax.experimental.pallas import tpu as pltpu

F32 = jnp.float32
BF16 = jnp.bfloat16

RMS_EPS = 1e-6
HEAD_DIM = 64
N_HEADS = 16
N_KV_HEADS = 2
PAIRS_PER_KV = N_HEADS // N_KV_HEADS // 2
BLOCK = 128
POOL_WINDOWS = (2, 4, 8, 16)
POOL_GROUP = 128
POOL_HALO = 16
LANES = 128
BF16_SUBLANES = 16
V7X_VMEM_BYTES = 64 * 1024 * 1024
VMEM_LIMIT = V7X_VMEM_BYTES - 4 * 1024 * 1024
LOG2E = math.log2(math.e)
MASK_BIAS = -3.0e38

FFN_TILE = 1024
FFN_CHUNK = 256
FFN_DOWN_GROUP = 4
FFN_WEIGHT_SLABS = 16
FFN_STAGE_SLOTS = 3
MIX_TILE = 1024
MIX_COL_CHUNK = 256


def _dot(a, b):
    return jnp.dot(a, b, preferred_element_type=F32)


def _dot_rows_split(a, b):
    half = a.shape[0] // 2
    return jnp.concatenate([_dot(a[:half], b), _dot(a[half:], b)], axis=0)


def _dot_nt(a, b):
    return lax.dot_general(a, b, (((1,), (1,)), ((), ())), preferred_element_type=F32)


def _rmsnorm_rows(x, g):
    ms = jnp.mean(x * x, axis=-1, keepdims=True)
    return x * lax.rsqrt(ms + RMS_EPS) * g


def _cast_slabs(f32_refs, bf16_refs):
    for src, dst in zip(f32_refs, bf16_refs, strict=True):
        dst[...] = src[...].astype(BF16)


def _ffn_kernel(x_ref, g_ref, wg_hbm, wu_hbm, wd_hbm, *rest, d_ff, chunk, n_cast, n_slabs,
                has_row_scale):
    rscale_ref = rest[0] if has_row_scale else None
    rest = rest[int(has_row_scale):]
    cast_in, o_ref, cast_out = rest[:n_cast], rest[n_cast], rest[n_cast + 1:2 * n_cast + 1]
    wg_v, wu_v, wd_v, stage_g, stage_u, stage_d, sems, acc_ref = rest[2 * n_cast + 1:]
    n_slots = stage_g.shape[0]
    streams = ((wg_hbm, stage_g, wg_v), (wu_hbm, stage_u, wu_v), (wd_hbm, stage_d, wd_v))

    def slab_copies(s):
        slot = s % n_slots
        return [pltpu.make_async_copy(hbm.at[pl.ds(s * stage.shape[1], stage.shape[1]), :],
                                      stage.at[slot], sems.at[k, slot])
                for k, (hbm, stage, _) in enumerate(streams)]

    @pl.when(pl.program_id(0) == 0)
    def _():
        for s in range(min(n_slots, n_slabs)):
            for cp in slab_copies(s):
                cp.start()
        for s in range(n_slabs):
            for cp in slab_copies(s):
                cp.wait()
            for _, stage, dst in streams:
                rows = stage.shape[1]
                dst[s * rows:(s + 1) * rows, :] = stage[s % n_slots].astype(BF16)
            if s + n_slots < n_slabs:
                for cp in slab_copies(s + n_slots):
                    cp.start()

    x = x_ref[...]
    if has_row_scale:
        xn = (x * rscale_ref[...] * g_ref[...]).astype(BF16)
    else:
        xn = _rmsnorm_rows(x, g_ref[...]).astype(BF16)
    n_chunks = d_ff // chunk
    for first in range(0, n_chunks, FFN_DOWN_GROUP):
        group = range(first, min(first + FFN_DOWN_GROUP, n_chunks))
        acts = []
        for c in group:
            cols = slice(c * chunk, (c + 1) * chunk)
            g = _dot(xn, wg_v[:, cols])
            u = _dot(xn, wu_v[:, cols])
            acts.append((g * jax.nn.sigmoid(g) * u).astype(BF16))
        d = _dot(jnp.concatenate(acts, axis=1), wd_v[group[0] * chunk:(group[-1] + 1) * chunk, :])
        if first == 0:
            acc_ref[...] = d
        else:
            acc_ref[...] += d
    o_ref[...] = x + 0.5 * acc_ref[...]
    _cast_slabs(cast_in, cast_out)


def _resident(shape):
    return pl.BlockSpec(shape, lambda i: (0,) * len(shape), pipeline_mode=pl.Buffered(1))


def _slab_specs(weights, steps):
    in_specs, out_specs, out_shapes = [], [], []
    for w in weights:
        rows, cols = w.shape
        assert rows % (steps * BF16_SUBLANES) == 0
        in_specs.append(pl.BlockSpec((rows // steps, cols), lambda i: (i, 0)))
        out_specs.append(pl.BlockSpec((rows // steps, cols), lambda i: (i, 0)))
        out_shapes.append(jax.ShapeDtypeStruct(w.shape, BF16))
    return in_specs, out_specs, out_shapes


def _ffn(x, norm_g, w_gate, w_up, w_down, *, name, row_scale=None, cast_for_later=()):
    n, d = x.shape
    d_ff = w_gate.shape[1]
    assert n % FFN_TILE == 0 and d_ff % FFN_CHUNK == 0
    assert d % (FFN_WEIGHT_SLABS * BF16_SUBLANES) == 0 and d_ff % (FFN_WEIGHT_SLABS * BF16_SUBLANES) == 0
    steps = n // FFN_TILE
    row_spec = pl.BlockSpec((FFN_TILE, d), lambda i: (i, 0))
    in_hbm = pl.BlockSpec(memory_space=pl.ANY)
    cast_in, cast_out, cast_shapes = _slab_specs(cast_for_later, steps)
    scale_spec = [] if row_scale is None else [pl.BlockSpec((FFN_TILE, 1), lambda i: (i, 0))]
    scale_arg = [] if row_scale is None else [row_scale]
    return pl.pallas_call(
        functools.partial(_ffn_kernel, d_ff=d_ff, chunk=FFN_CHUNK, n_cast=len(cast_for_later),
                          n_slabs=FFN_WEIGHT_SLABS, has_row_scale=row_scale is not None),
        grid=(steps,),
        in_specs=[row_spec, _resident((1, d)), in_hbm, in_hbm, in_hbm] + scale_spec + cast_in,
        out_specs=[row_spec] + cast_out,
        out_shape=[jax.ShapeDtypeStruct((n, d), F32)] + cast_shapes,
        scratch_shapes=[
            pltpu.VMEM((d, d_ff), BF16),
            pltpu.VMEM((d, d_ff), BF16),
            pltpu.VMEM((d_ff, d), BF16),
            pltpu.VMEM((FFN_STAGE_SLOTS, d // FFN_WEIGHT_SLABS, d_ff), F32),
            pltpu.VMEM((FFN_STAGE_SLOTS, d // FFN_WEIGHT_SLABS, d_ff), F32),
            pltpu.VMEM((FFN_STAGE_SLOTS, d_ff // FFN_WEIGHT_SLABS, d), F32),
            pltpu.SemaphoreType.DMA((3, FFN_STAGE_SLOTS)),
            pltpu.VMEM((FFN_TILE, d), F32),
        ],
        compiler_params=pltpu.CompilerParams(dimension_semantics=("arbitrary",),
                                             vmem_limit_bytes=VMEM_LIMIT),
        name=name,
    )(x, norm_g.reshape(1, d), w_gate, w_up, w_down, *scale_arg, *cast_for_later)


def _head_mean_sq(x, low_half):
    x2 = x * x
    s_low = jnp.sum(jnp.where(low_half, x2, 0.0), axis=-1, keepdims=True)
    s_all = jnp.sum(x2, axis=-1, keepdims=True)
    return jnp.where(low_half, s_low, s_all - s_low) * (1.0 / HEAD_DIM)


def _split_kv_heads(x):
    swapped = pltpu.roll(x, HEAD_DIM, axis=1)
    low = lax.broadcasted_iota(jnp.int32, x.shape, 1) < HEAD_DIM
    zero = jnp.zeros_like(x)
    parts = (jnp.where(low, x, zero), jnp.where(low, zero, swapped),
             jnp.where(low, swapped, zero), jnp.where(low, zero, x))
    return tuple(p.astype(BF16) for p in parts)


def _mixer_kernel(h_ref, mixg_ref, w_in_ref, gbias_ref, poolw_ref, pscale_ref, wpo_ref,
                  lanes_ref, wao_ref, wout_ref, o_ref, rscale_ref,
                  u_scr, xp_scr, q32_scr, q_scr, ka_scr, kb_scr, ve_scr, vo_scr, attn_scr, mixed_scr,
                  gate_scr, merged_scr, sumsq_scr,
                  *, tiles_per_seq, d_model, pool_width, attn_width, kv_width):
    tm = h_ref.shape[0]
    i = pl.program_id(0)
    tile_in_seq = i % tiles_per_seq
    seq_start = tile_in_seq == 0
    lane128 = lax.broadcasted_iota(jnp.int32, (1, LANES), 1)
    low_half = lane128 < HEAD_DIM

    off_q = pool_width
    off_k = off_q + attn_width
    off_gp = off_k + 2 * kv_width
    off_ga = off_gp + d_model

    @pl.when(seq_start)
    def _():
        xp_scr[0:POOL_HALO, :] = jnp.zeros((POOL_HALO, pool_width), F32)
        zk = jnp.zeros((BLOCK, LANES), BF16)
        ones_e = jnp.broadcast_to(jnp.where(low_half, 1.0, 0.0).astype(BF16), (BLOCK + tm, LANES))
        ones_o = jnp.broadcast_to(jnp.where(low_half, 0.0, 1.0).astype(BF16), (BLOCK + tm, LANES))
        for kv in range(N_KV_HEADS):
            ka_scr[kv, 0:BLOCK, :] = zk
            kb_scr[kv, 0:BLOCK, :] = zk
            ve_scr[kv, 0:BLOCK, 0:LANES] = zk
            vo_scr[kv, 0:BLOCK, 0:LANES] = zk
            ve_scr[kv, :, LANES:2 * LANES] = ones_e
            vo_scr[kv, :, LANES:2 * LANES] = ones_o

    @pl.when(jnp.logical_not(seq_start))
    def _():
        xp_scr[0:POOL_HALO, :] = xp_scr[tm:tm + POOL_HALO, :]
        for kv in range(N_KV_HEADS):
            ka_scr[kv, 0:BLOCK, :] = ka_scr[kv, tm:tm + BLOCK, :]
            kb_scr[kv, 0:BLOCK, :] = kb_scr[kv, tm:tm + BLOCK, :]
            ve_scr[kv, 0:BLOCK, 0:LANES] = ve_scr[kv, tm:tm + BLOCK, 0:LANES]
            vo_scr[kv, 0:BLOCK, 0:LANES] = vo_scr[kv, tm:tm + BLOCK, 0:LANES]

    u_scr[...] = _rmsnorm_rows(h_ref[...], mixg_ref[...]).astype(BF16)

    xp_scr[POOL_HALO:POOL_HALO + tm, :] = _dot(u_scr[...], w_in_ref[:, 0:pool_width])
    q32_scr[...] = _dot(u_scr[...], w_in_ref[:, off_q:off_q + attn_width])
    kv_proj = _dot_rows_split(u_scr[...], w_in_ref[:, off_k:off_k + 2 * kv_width])

    t = tile_in_seq * tm + lax.broadcasted_iota(jnp.int32, (tm, 1), 0)
    pooled = []
    for gi, w in enumerate(POOL_WINDOWS):
        cols = slice(gi * POOL_GROUP, (gi + 1) * POOL_GROUP)
        ext = xp_scr[:, cols]
        wsum = ext
        shift = 1
        while shift < w:
            wsum = wsum + pltpu.roll(wsum, shift, axis=0)
            shift *= 2
        count = jnp.minimum(t + 1, w).astype(F32)
        pooled.append((wsum[POOL_HALO:] / count - ext[POOL_HALO:]).astype(BF16))
    for c in range(len(POOL_WINDOWS) // 2):
        cols = slice(c * 2 * POOL_GROUP, (c + 1) * 2 * POOL_GROUP)
        pair = jnp.concatenate(pooled[2 * c:2 * c + 2], axis=1)
        zeros = jnp.zeros((POOL_GROUP, POOL_GROUP), F32)
        w_pair = jnp.concatenate(
            [jnp.concatenate([poolw_ref[2 * c], zeros], axis=1),
             jnp.concatenate([zeros, poolw_ref[2 * c + 1]], axis=1)], axis=0).astype(BF16)
        mixed_scr[:, cols] = (_dot(pair, w_pair) * pscale_ref[:, cols]).astype(BF16)

    q_gain = lanes_ref[0:1, :] * (HEAD_DIM ** -0.5 * LOG2E)
    n_gate_chunks = d_model // MIX_COL_CHUNK
    q_chunks_per_gate_chunk = attn_width // LANES // n_gate_chunks
    for c in range(n_gate_chunks):
        cols = slice(c * MIX_COL_CHUNK, (c + 1) * MIX_COL_CHUNK)
        gp = jax.nn.sigmoid(
            _dot(u_scr[...], w_in_ref[:, off_gp + c * MIX_COL_CHUNK: off_gp + (c + 1) * MIX_COL_CHUNK])
            + gbias_ref[:, cols])
        gate_scr[:, cols] = gp
        for qc in range(c * q_chunks_per_gate_chunk, (c + 1) * q_chunks_per_gate_chunk):
            qcols = slice(qc * LANES, (qc + 1) * LANES)
            q = q32_scr[:, qcols]
            ms = _head_mean_sq(q, low_half)
            q_scr[:, qcols] = (q * lax.rsqrt(ms + RMS_EPS) * q_gain).astype(BF16)

    k = kv_proj[:, 0:kv_width]
    k = k * lax.rsqrt(_head_mean_sq(k, low_half) + RMS_EPS) * lanes_ref[1:2, :]
    ka0, kb0, ka1, kb1 = _split_kv_heads(k)
    ka_scr[0, BLOCK:BLOCK + tm, :] = ka0
    kb_scr[0, BLOCK:BLOCK + tm, :] = kb0
    ka_scr[1, BLOCK:BLOCK + tm, :] = ka1
    kb_scr[1, BLOCK:BLOCK + tm, :] = kb1
    va0, vb0, va1, vb1 = _split_kv_heads(kv_proj[:, kv_width:2 * kv_width])
    ve_scr[0, BLOCK:BLOCK + tm, 0:LANES] = va0
    vo_scr[0, BLOCK:BLOCK + tm, 0:LANES] = vb0
    ve_scr[1, BLOCK:BLOCK + tm, 0:LANES] = va1
    vo_scr[1, BLOCK:BLOCK + tm, 0:LANES] = vb1

    key = lax.broadcasted_iota(jnp.int32, (2 * BLOCK, BLOCK), 0)
    qry = lax.broadcasted_iota(jnp.int32, (2 * BLOCK, BLOCK), 1)
    band = (key > qry) & (key <= qry + BLOCK)
    first_key = jnp.where(seq_start, BLOCK, 0)
    bias = jnp.where(band, 0.0, MASK_BIAS).astype(BF16)
    bias_first = jnp.where(band & (key >= first_key), 0.0, MASK_BIAS).astype(BF16)
    r4 = lax.broadcasted_iota(jnp.int32, (PAIRS_PER_KV * BLOCK, BLOCK), 0) % BLOCK
    c4 = lax.broadcasted_iota(jnp.int32, (PAIRS_PER_KV * BLOCK, BLOCK), 1)
    onehot = jnp.where(r4 == c4, 1.0, 0.0).astype(BF16)
    sinks2 = lanes_ref[2:2 + N_HEADS // 2, :] * LOG2E

    def score_unit(unit):
        b, kv = divmod(unit, N_KV_HEADS)
        rows = slice(b * BLOCK, (b + 1) * BLOCK)
        keys = slice(b * BLOCK, (b + 2) * BLOCK)
        key_bias = bias_first if b == 0 else bias
        base = kv * PAIRS_PER_KV
        qs = jnp.concatenate([q_scr[rows, (base + j) * LANES:(base + j + 1) * LANES]
                              for j in range(PAIRS_PER_KV)], axis=0)
        qs = jnp.concatenate([qs, onehot], axis=1)
        k_even = jnp.concatenate([ka_scr[kv, keys, :], key_bias], axis=1)
        k_odd = jnp.concatenate([kb_scr[kv, keys, :], key_bias], axis=1)
        return _dot_nt(qs, k_even), _dot_nt(qs, k_odd)

    def finish_unit(unit, scores):
        b, kv = divmod(unit, N_KV_HEADS)
        rows = slice(b * BLOCK, (b + 1) * BLOCK)
        keys = slice(b * BLOCK, (b + 2) * BLOCK)
        base = kv * PAIRS_PER_KV
        probs = ([], [])
        sink_terms = []
        for j in range(PAIRS_PER_KV):
            m = []
            for par in range(2):
                s = scores[par][j * BLOCK:(j + 1) * BLOCK]
                m.append(jnp.max(s, axis=-1, keepdims=True))
                probs[par].append(jnp.exp2(s - m[par]).astype(BF16))
            sink_terms.append(jnp.exp2(sinks2[base + j:base + j + 1, :]
                                       - jnp.where(low_half, m[0], m[1])))
        p = jnp.concatenate([jnp.concatenate(probs[0], axis=0),
                             jnp.concatenate(probs[1], axis=0)], axis=1)
        v_stack = jnp.concatenate([ve_scr[kv, keys, :], vo_scr[kv, keys, :]], axis=0)
        acc = _dot_rows_split(p, v_stack)
        den = acc[:, LANES:2 * LANES] + jnp.concatenate(sink_terms, axis=0)
        out = (acc[:, 0:LANES] / den).astype(BF16)
        for j in range(PAIRS_PER_KV):
            attn_scr[rows, (base + j) * LANES:(base + j + 1) * LANES] = out[j * BLOCK:(j + 1) * BLOCK]

    half = tm // 2

    def pool_piece(r0, c):
        rows = slice(r0, r0 + half)
        cols = slice(c * MIX_COL_CHUNK, (c + 1) * MIX_COL_CHUNK)
        gate_scr[rows, cols] = gate_scr[rows, cols] * _dot(mixed_scr[rows, :], wpo_ref[:, cols])

    def merge_piece(r0, c):
        rows = slice(r0, r0 + half)
        cols = slice(c * MIX_COL_CHUNK, (c + 1) * MIX_COL_CHUNK)
        cols_a = slice(d_model + c * MIX_COL_CHUNK, d_model + (c + 1) * MIX_COL_CHUNK)
        ba = _dot(attn_scr[rows, :], wao_ref[:, cols])
        merged_scr[rows, cols] = (gate_scr[rows, cols] + gate_scr[rows, cols_a] * ba).astype(BF16)

    def out_piece(r0, c):
        rows = slice(r0, r0 + half)
        cols = slice(c * MIX_COL_CHUNK, (c + 1) * MIX_COL_CHUNK)
        out = h_ref[rows, cols] + _dot(merged_scr[rows, :], wout_ref[:, cols])
        o_ref[rows, cols] = out
        sq = out * out
        part = sq[:, 0:LANES]
        for k in range(1, MIX_COL_CHUNK // LANES):
            part = part + sq[:, k * LANES:(k + 1) * LANES]
        if c > 0:
            part = part + sumsq_scr[rows, :]
        if c + 1 < n_gate_chunks:
            sumsq_scr[rows, :] = part
        else:
            mean_sq = jnp.sum(part, axis=-1, keepdims=True) * (1.0 / d_model)
            rscale_ref[rows, :] = lax.rsqrt(mean_sq + RMS_EPS)

    def output_pieces(r0):
        return ([functools.partial(merge_piece, r0, c) for c in range(n_gate_chunks)]
                + [functools.partial(out_piece, r0, c) for c in range(n_gate_chunks)])

    def attn_gate_piece(r0, c):
        rows = slice(r0, r0 + half)
        cols_a = slice(d_model + c * MIX_COL_CHUNK, d_model + (c + 1) * MIX_COL_CHUNK)
        gate_scr[rows, cols_a] = jax.nn.sigmoid(
            _dot(u_scr[rows, :], w_in_ref[:, off_ga + c * MIX_COL_CHUNK: off_ga + (c + 1) * MIX_COL_CHUNK])
            + gbias_ref[:, cols_a])

    n_units = (tm // BLOCK) * N_KV_HEADS
    dense = ([(functools.partial(attn_gate_piece, r0, c), functools.partial(pool_piece, r0, c))
              for r0 in (0, half) for c in range(n_gate_chunks)]
             + [(piece,) for piece in output_pieces(0)])
    assert len(dense) == n_units
    scores = score_unit(0)
    for unit in range(n_units):
        next_scores = score_unit(unit + 1) if unit + 1 < n_units else None
        for piece in dense[unit]:
            piece()
        finish_unit(unit, scores)
        scores = next_scores
    for piece in output_pieces(half):
        piece()


def _mixer(h, seq_len, mix_norm, w_in, pool_w, pool_scale, w_pool_out, q_norm, k_norm, sinks,
           w_attn_out, gate_bias, w_out):
    n, d = h.shape
    tm = MIX_TILE
    pool_width = w_pool_out.shape[0]
    attn_width = w_attn_out.shape[0]
    kv_width = N_KV_HEADS * HEAD_DIM
    in_width = w_in.shape[1]
    assert seq_len % tm == 0 and tm % BLOCK == 0
    assert pool_width == len(POOL_WINDOWS) * POOL_GROUP and attn_width == N_HEADS * HEAD_DIM
    assert in_width == pool_width + attn_width + 2 * kv_width + 2 * d and kv_width == LANES
    row_spec = pl.BlockSpec((tm, d), lambda i: (i, 0))
    kern = functools.partial(_mixer_kernel, tiles_per_seq=seq_len // tm, d_model=d,
                             pool_width=pool_width, attn_width=attn_width, kv_width=kv_width)
    lane_rows = jnp.concatenate(
        [jnp.tile(q_norm, LANES // HEAD_DIM)[None], jnp.tile(k_norm, LANES // HEAD_DIM)[None],
         jnp.repeat(sinks.reshape(N_HEADS // 2, 2), HEAD_DIM, axis=1)], axis=0)
    return pl.pallas_call(
        kern,
        grid=(n // tm,),
        in_specs=[row_spec, _resident((1, d)), _resident((d, in_width)), _resident((1, 2 * d)),
                  _resident(pool_w.shape), _resident((1, pool_width)), _resident((pool_width, d)),
                  _resident(lane_rows.shape),
                  _resident((attn_width, d)), _resident((d, d))],
        out_specs=[row_spec, pl.BlockSpec((tm, 1), lambda i: (i, 0))],
        out_shape=[jax.ShapeDtypeStruct((n, d), F32), jax.ShapeDtypeStruct((n, 1), F32)],
        scratch_shapes=[
            pltpu.VMEM((tm, d), BF16),
            pltpu.VMEM((POOL_HALO + tm, pool_width), F32),
            pltpu.VMEM((tm, attn_width), F32),
            pltpu.VMEM((tm, attn_width), BF16),
            pltpu.VMEM((N_KV_HEADS, BLOCK + tm, LANES), BF16),
            pltpu.VMEM((N_KV_HEADS, BLOCK + tm, LANES), BF16),
            pltpu.VMEM((N_KV_HEADS, BLOCK + tm, 2 * LANES), BF16),
            pltpu.VMEM((N_KV_HEADS, BLOCK + tm, 2 * LANES), BF16),
            pltpu.VMEM((tm, attn_width), BF16),
            pltpu.VMEM((tm, pool_width), BF16),
            pltpu.VMEM((tm, 2 * d), F32),
            pltpu.VMEM((tm, d), BF16),
            pltpu.VMEM((tm, LANES), F32),
        ],
        compiler_params=pltpu.CompilerParams(dimension_semantics=("arbitrary",),
                                             vmem_limit_bytes=VMEM_LIMIT),
        name="mixer",
    )(h, mix_norm.reshape(1, d), w_in, gate_bias.reshape(1, 2 * d),
      pool_w, pool_scale.reshape(1, pool_width), w_pool_out, lane_rows,
      w_attn_out, w_out)


def kernel(x, ffn1_norm, ffn1_w_gate, ffn1_w_up, ffn1_w_down, mix_norm, w_in, pool_w, pool_scale,
           w_pool_out, q_norm, k_norm, sinks, w_attn_out, gate_bias, w_out, ffn2_norm, ffn2_w_gate,
           ffn2_w_up, ffn2_w_down):
    b, s, d = x.shape
    h = x.reshape(b * s, d)
    h, w_in_b, w_pool_out_b, w_attn_out_b, w_out_b = _ffn(
        h, ffn1_norm, ffn1_w_gate, ffn1_w_up, ffn1_w_down, name="ffn1",
        cast_for_later=(w_in, w_pool_out, w_attn_out, w_out))
    h, h_row_scale = _mixer(h, s, mix_norm, w_in_b, pool_w, pool_scale, w_pool_out_b, q_norm, k_norm,
                            sinks, w_attn_out_b, gate_bias, w_out_b)
    (h,) = _ffn(h, ffn2_norm, ffn2_w_gate, ffn2_w_up, ffn2_w_down, name="ffn2",
                row_scale=h_row_scale)
    return h.reshape(b, s, d)
```

```python
import functools
import math

import jax
import jax.numpy as jnp
from jax import lax
from jax.experimental import pallas as pl
from jax.experimental.pallas import tpu as pltpu

F32 = jnp.float32
BF16 = jnp.bfloat16

RMS_EPS = 1e-6
HEAD_DIM = 64
N_HEADS = 16
N_KV_HEADS = 2
PAIRS_PER_KV = N_HEADS // N_KV_HEADS // 2
BLOCK = 128
POOL_WINDOWS = (2, 4, 8, 16)
POOL_GROUP = 128
POOL_HALO = 16
LANES = 128
BF16_SUBLANES = 16
V7X_VMEM_BYTES = 64 * 1024 * 1024
VMEM_LIMIT = V7X_VMEM_BYTES - 4 * 1024 * 1024
LOG2E = math.log2(math.e)
MASK_BIAS = -3.0e38

FFN_TILE = 1024
FFN_CHUNK = 256
FFN_DOWN_GROUP = 4
FFN_WEIGHT_SLABS = 16
FFN_STAGE_SLOTS = 3
MIX_TILE = 1024
MIX_COL_CHUNK = 256


def _dot(a, b):
    return jnp.dot(a, b, preferred_element_type=F32)


def _dot_rows_split(a, b):
    half = a.shape[0] // 2
    return jnp.concatenate([_dot(a[:half], b), _dot(a[half:], b)], axis=0)


def _dot_nt(a, b):
    return lax.dot_general(a, b, (((1,), (1,)), ((), ())), preferred_element_type=F32)


def _rmsnorm_rows(x, g):
    ms = jnp.mean(x * x, axis=-1, keepdims=True)
    return x * lax.rsqrt(ms + RMS_EPS) * g


def _cast_slabs(f32_refs, bf16_refs):
    for src, dst in zip(f32_refs, bf16_refs, strict=True):
        dst[...] = src[...].astype(BF16)


def _ffn_kernel(x_ref, g_ref, wg_hbm, wu_hbm, wd_hbm, *rest, d_ff, chunk, n_cast, n_slabs,
                has_row_scale):
    rscale_ref = rest[0] if has_row_scale else None
    rest = rest[int(has_row_scale):]
    cast_in, o_ref, cast_out = rest[:n_cast], rest[n_cast], rest[n_cast + 1:2 * n_cast + 1]
    wg_v, wu_v, wd_v, stage_g, stage_u, stage_d, sems, acc_ref = rest[2 * n_cast + 1:]
    n_slots = stage_g.shape[0]
    streams = ((wg_hbm, stage_g, wg_v), (wu_hbm, stage_u, wu_v), (wd_hbm, stage_d, wd_v))

    def slab_copies(s):
        slot = s % n_slots
        return [pltpu.make_async_copy(hbm.at[pl.ds(s * stage.shape[1], stage.shape[1]), :],
                                      stage.at[slot], sems.at[k, slot])
                for k, (hbm, stage, _) in enumerate(streams)]

    @pl.when(pl.program_id(0) == 0)
    def _():
        for s in range(min(n_slots, n_slabs)):
            for cp in slab_copies(s):
                cp.start()
        for s in range(n_slabs):
            for cp in slab_copies(s):
                cp.wait()
            for _, stage, dst in streams:
                rows = stage.shape[1]
                dst[s * rows:(s + 1) * rows, :] = stage[s % n_slots].astype(BF16)
            if s + n_slots < n_slabs:
                for cp in slab_copies(s + n_slots):
                    cp.start()

    x = x_ref[...]
    if has_row_scale:
        xn = (x * rscale_ref[...] * g_ref[...]).astype(BF16)
    else:
        xn = _rmsnorm_rows(x, g_ref[...]).astype(BF16)
    n_chunks = d_ff // chunk
    for first in range(0, n_chunks, FFN_DOWN_GROUP):
        group = range(first, min(first + FFN_DOWN_GROUP, n_chunks))
        acts = []
        for c in group:
            cols = slice(c * chunk, (c + 1) * chunk)
            g = _dot(xn, wg_v[:, cols])
            u = _dot(xn, wu_v[:, cols])
            acts.append((g * jax.nn.sigmoid(g) * u).astype(BF16))
        d = _dot(jnp.concatenate(acts, axis=1), wd_v[group[0] * chunk:(group[-1] + 1) * chunk, :])
        if first == 0:
            acc_ref[...] = d
        else:
            acc_ref[...] += d
    o_ref[...] = x + 0.5 * acc_ref[...]
    _cast_slabs(cast_in, cast_out)


def _resident(shape):
    return pl.BlockSpec(shape, lambda i: (0,) * len(shape), pipeline_mode=pl.Buffered(1))


def _slab_specs(weights, steps):
    in_specs, out_specs, out_shapes = [], [], []
    for w in weights:
        rows, cols = w.shape
        assert rows % (steps * BF16_SUBLANES) == 0
        in_specs.append(pl.BlockSpec((rows // steps, cols), lambda i: (i, 0)))
        out_specs.append(pl.BlockSpec((rows // steps, cols), lambda i: (i, 0)))
        out_shapes.append(jax.ShapeDtypeStruct(w.shape, BF16))
    return in_specs, out_specs, out_shapes


def _ffn(x, norm_g, w_gate, w_up, w_down, *, name, row_scale=None, cast_for_later=()):
    n, d = x.shape
    d_ff = w_gate.shape[1]
    assert n % FFN_TILE == 0 and d_ff % FFN_CHUNK == 0
    assert d % (FFN_WEIGHT_SLABS * BF16_SUBLANES) == 0 and d_ff % (FFN_WEIGHT_SLABS * BF16_SUBLANES) == 0
    steps = n // FFN_TILE
    row_spec = pl.BlockSpec((FFN_TILE, d), lambda i: (i, 0))
    in_hbm = pl.BlockSpec(memory_space=pl.ANY)
    cast_in, cast_out, cast_shapes = _slab_specs(cast_for_later, steps)
    scale_spec = [] if row_scale is None else [pl.BlockSpec((FFN_TILE, 1), lambda i: (i, 0))]
    scale_arg = [] if row_scale is None else [row_scale]
    return pl.pallas_call(
        functools.partial(_ffn_kernel, d_ff=d_ff, chunk=FFN_CHUNK, n_cast=len(cast_for_later),
                          n_slabs=FFN_WEIGHT_SLABS, has_row_scale=row_scale is not None),
        grid=(steps,),
        in_specs=[row_spec, _resident((1, d)), in_hbm, in_hbm, in_hbm] + scale_spec + cast_in,
        out_specs=[row_spec] + cast_out,
        out_shape=[jax.ShapeDtypeStruct((n, d), F32)] + cast_shapes,
        scratch_shapes=[
            pltpu.VMEM((d, d_ff), BF16),
            pltpu.VMEM((d, d_ff), BF16),
            pltpu.VMEM((d_ff, d), BF16),
            pltpu.VMEM((FFN_STAGE_SLOTS, d // FFN_WEIGHT_SLABS, d_ff), F32),
            pltpu.VMEM((FFN_STAGE_SLOTS, d // FFN_WEIGHT_SLABS, d_ff), F32),
            pltpu.VMEM((FFN_STAGE_SLOTS, d_ff // FFN_WEIGHT_SLABS, d), F32),
            pltpu.SemaphoreType.DMA((3, FFN_STAGE_SLOTS)),
            pltpu.VMEM((FFN_TILE, d), F32),
        ],
        compiler_params=pltpu.CompilerParams(dimension_semantics=("arbitrary",),
                                             vmem_limit_bytes=VMEM_LIMIT),
        name=name,
    )(x, norm_g.reshape(1, d), w_gate, w_up, w_down, *scale_arg, *cast_for_later)


def _head_mean_sq(x, low_half):
    x2 = x * x
    s_low = jnp.sum(jnp.where(low_half, x2, 0.0), axis=-1, keepdims=True)
    s_all = jnp.sum(x2, axis=-1, keepdims=True)
    return jnp.where(low_half, s_low, s_all - s_low) * (1.0 / HEAD_DIM)


def _split_kv_heads(x):
    swapped = pltpu.roll(x, HEAD_DIM, axis=1)
    low = lax.broadcasted_iota(jnp.int32, x.shape, 1) < HEAD_DIM
    zero = jnp.zeros_like(x)
    parts = (jnp.where(low, x, zero), jnp.where(low, zero, swapped),
             jnp.where(low, swapped, zero), jnp.where(low, zero, x))
    return tuple(p.astype(BF16) for p in parts)


def _mixer_kernel(h_ref, mixg_ref, w_in_ref, gbias_ref, poolw_ref, pscale_ref, wpo_ref,
                  lanes_ref, wao_ref, wout_ref, o_ref, rscale_ref,
                  u_scr, xp_scr, q32_scr, q_scr, ka_scr, kb_scr, ve_scr, vo_scr, attn_scr, mixed_scr,
                  gate_scr, merged_scr, sumsq_scr,
                  *, tiles_per_seq, d_model, pool_width, attn_width, kv_width):
    tm = h_ref.shape[0]
    i = pl.program_id(0)
    tile_in_seq = i % tiles_per_seq
    seq_start = tile_in_seq == 0
    lane128 = lax.broadcasted_iota(jnp.int32, (1, LANES), 1)
    low_half = lane128 < HEAD_DIM

    off_q = pool_width
    off_k = off_q + attn_width
    off_gp = off_k + 2 * kv_width
    off_ga = off_gp + d_model

    @pl.when(seq_start)
    def _():
        xp_scr[:, 0:POOL_HALO, :] = jnp.zeros((len(POOL_WINDOWS), POOL_HALO, POOL_GROUP), F32)
        zk = jnp.zeros((BLOCK, LANES), BF16)
        ones_e = jnp.broadcast_to(jnp.where(low_half, 1.0, 0.0).astype(BF16), (BLOCK + tm, LANES))
        ones_o = jnp.broadcast_to(jnp.where(low_half, 0.0, 1.0).astype(BF16), (BLOCK + tm, LANES))
        for kv in range(N_KV_HEADS):
            ka_scr[kv, 0:BLOCK, :] = zk
            kb_scr[kv, 0:BLOCK, :] = zk
            ve_scr[kv, 0:BLOCK, 0:LANES] = zk
            vo_scr[kv, 0:BLOCK, 0:LANES] = zk
            ve_scr[kv, :, LANES:2 * LANES] = ones_e
            vo_scr[kv, :, LANES:2 * LANES] = ones_o

    @pl.when(jnp.logical_not(seq_start))
    def _():
        xp_scr[:, 0:POOL_HALO, :] = xp_scr[:, tm:tm + POOL_HALO, :]
        for kv in range(N_KV_HEADS):
            ka_scr[kv, 0:BLOCK, :] = ka_scr[kv, tm:tm + BLOCK, :]
            kb_scr[kv, 0:BLOCK, :] = kb_scr[kv, tm:tm + BLOCK, :]
            ve_scr[kv, 0:BLOCK, 0:LANES] = ve_scr[kv, tm:tm + BLOCK, 0:LANES]
            vo_scr[kv, 0:BLOCK, 0:LANES] = vo_scr[kv, tm:tm + BLOCK, 0:LANES]

    u_scr[...] = _rmsnorm_rows(h_ref[...], mixg_ref[...]).astype(BF16)

    xp = _dot(u_scr[...], w_in_ref[:, 0:pool_width])
    for gi in range(len(POOL_WINDOWS)):
        xp_scr[gi, POOL_HALO:POOL_HALO + tm, :] = xp[:, gi * POOL_GROUP:(gi + 1) * POOL_GROUP]
    q32 = _dot(u_scr[...], w_in_ref[:, off_q:off_q + attn_width])
    for qc in range(attn_width // LANES):
        q32_scr[qc] = q32[:, qc * LANES:(qc + 1) * LANES]
    kv_proj = _dot_rows_split(u_scr[...], w_in_ref[:, off_k:off_k + 2 * kv_width])

    t = tile_in_seq * tm + lax.broadcasted_iota(jnp.int32, (tm, 1), 0)
    pooled = []
    for gi, w in enumerate(POOL_WINDOWS):
        ext = xp_scr[gi]
        wsum = ext
        shift = 1
        while shift < w:
            wsum = wsum + pltpu.roll(wsum, shift, axis=0)
            shift *= 2
        count = jnp.minimum(t + 1, w).astype(F32)
        pooled.append((wsum[POOL_HALO:] / count - ext[POOL_HALO:]).astype(BF16))
    for c in range(len(POOL_WINDOWS) // 2):
        cols = slice(c * 2 * POOL_GROUP, (c + 1) * 2 * POOL_GROUP)
        pair = jnp.concatenate(pooled[2 * c:2 * c + 2], axis=1)
        zeros = jnp.zeros((POOL_GROUP, POOL_GROUP), F32)
        w_pair = jnp.concatenate(
            [jnp.concatenate([poolw_ref[2 * c], zeros], axis=1),
             jnp.concatenate([zeros, poolw_ref[2 * c + 1]], axis=1)], axis=0).astype(BF16)
        mixed_scr[:, cols] = (_dot(pair, w_pair) * pscale_ref[:, cols]).astype(BF16)

    q_gain = lanes_ref[0:1, :] * (HEAD_DIM ** -0.5 * LOG2E)
    n_gate_chunks = d_model // MIX_COL_CHUNK
    q_chunks_per_gate_chunk = attn_width // LANES // n_gate_chunks
    for c in range(n_gate_chunks):
        cols = slice(c * MIX_COL_CHUNK, (c + 1) * MIX_COL_CHUNK)
        gp = jax.nn.sigmoid(
            _dot(u_scr[...], w_in_ref[:, off_gp + c * MIX_COL_CHUNK: off_gp + (c + 1) * MIX_COL_CHUNK])
            + gbias_ref[:, cols])
        gate_scr[c] = gp
        for qc in range(c * q_chunks_per_gate_chunk, (c + 1) * q_chunks_per_gate_chunk):
            q = q32_scr[qc]
            ms = _head_mean_sq(q, low_half)
            q_scr[qc] = (q * lax.rsqrt(ms + RMS_EPS) * q_gain).astype(BF16)

    k = kv_proj[:, 0:kv_width]
    k = k * lax.rsqrt(_head_mean_sq(k, low_half) + RMS_EPS) * lanes_ref[1:2, :]
    ka0, kb0, ka1, kb1 = _split_kv_heads(k)
    ka_scr[0, BLOCK:BLOCK + tm, :] = ka0
    kb_scr[0, BLOCK:BLOCK + tm, :] = kb0
    ka_scr[1, BLOCK:BLOCK + tm, :] = ka1
    kb_scr[1, BLOCK:BLOCK + tm, :] = kb1
    va0, vb0, va1, vb1 = _split_kv_heads(kv_proj[:, kv_width:2 * kv_width])
    ve_scr[0, BLOCK:BLOCK + tm, 0:LANES] = va0
    vo_scr[0, BLOCK:BLOCK + tm, 0:LANES] = vb0
    ve_scr[1, BLOCK:BLOCK + tm, 0:LANES] = va1
    vo_scr[1, BLOCK:BLOCK + tm, 0:LANES] = vb1

    key = lax.broadcasted_iota(jnp.int32, (2 * BLOCK, BLOCK), 0)
    qry = lax.broadcasted_iota(jnp.int32, (2 * BLOCK, BLOCK), 1)
    band = (key > qry) & (key <= qry + BLOCK)
    first_key = jnp.where(seq_start, BLOCK, 0)
    bias = jnp.where(band, 0.0, MASK_BIAS).astype(BF16)
    bias_first = jnp.where(band & (key >= first_key), 0.0, MASK_BIAS).astype(BF16)
    r4 = lax.broadcasted_iota(jnp.int32, (PAIRS_PER_KV * BLOCK, BLOCK), 0) % BLOCK
    c4 = lax.broadcasted_iota(jnp.int32, (PAIRS_PER_KV * BLOCK, BLOCK), 1)
    onehot = jnp.where(r4 == c4, 1.0, 0.0).astype(BF16)
    sinks2 = lanes_ref[2:2 + N_HEADS // 2, :] * LOG2E

    def score_unit(unit):
        b, kv = divmod(unit, N_KV_HEADS)
        rows = slice(b * BLOCK, (b + 1) * BLOCK)
        keys = slice(b * BLOCK, (b + 2) * BLOCK)
        key_bias = bias_first if b == 0 else bias
        base = kv * PAIRS_PER_KV
        qs = jnp.concatenate([q_scr[base + j, rows, :] for j in range(PAIRS_PER_KV)], axis=0)
        qs = jnp.concatenate([qs, onehot], axis=1)
        k_even = jnp.concatenate([ka_scr[kv, keys, :], key_bias], axis=1)
        k_odd = jnp.concatenate([kb_scr[kv, keys, :], key_bias], axis=1)
        return _dot_nt(qs, k_even), _dot_nt(qs, k_odd)

    def finish_unit(unit, scores):
        b, kv = divmod(unit, N_KV_HEADS)
        rows = slice(b * BLOCK, (b + 1) * BLOCK)
        keys = slice(b * BLOCK, (b + 2) * BLOCK)
        base = kv * PAIRS_PER_KV
        probs = ([], [])
        sink_terms = []
        for j in range(PAIRS_PER_KV):
            m = []
            for par in range(2):
                s = scores[par][j * BLOCK:(j + 1) * BLOCK]
                m.append(jnp.max(s, axis=-1, keepdims=True))
                probs[par].append(jnp.exp2(s - m[par]).astype(BF16))
            sink_terms.append(jnp.exp2(sinks2[base + j:base + j + 1, :]
                                       - jnp.where(low_half, m[0], m[1])))
        p = jnp.concatenate([jnp.concatenate(probs[0], axis=0),
                             jnp.concatenate(probs[1], axis=0)], axis=1)
        v_stack = jnp.concatenate([ve_scr[kv, keys, :], vo_scr[kv, keys, :]], axis=0)
        acc = _dot_rows_split(p, v_stack)
        den = acc[:, LANES:2 * LANES] + jnp.concatenate(sink_terms, axis=0)
        out = (acc[:, 0:LANES] / den).astype(BF16)
        for j in range(PAIRS_PER_KV):
            attn_scr[base + j, rows, :] = out[j * BLOCK:(j + 1) * BLOCK]

    half = tm // 2

    def pool_piece(r0, c):
        rows = slice(r0, r0 + half)
        cols = slice(c * MIX_COL_CHUNK, (c + 1) * MIX_COL_CHUNK)
        gate_scr[c, rows, :] = gate_scr[c, rows, :] * _dot(mixed_scr[rows, :], wpo_ref[:, cols])

    def merge_piece(r0, c):
        rows = slice(r0, r0 + half)
        cols = slice(c * MIX_COL_CHUNK, (c + 1) * MIX_COL_CHUNK)
        attn = jnp.concatenate([attn_scr[j, rows, :] for j in range(attn_width // LANES)], axis=1)
        ba = _dot(attn, wao_ref[:, cols])
        merged_scr[rows, cols] = (gate_scr[c, rows, :]
                                  + gate_scr[n_gate_chunks + c, rows, :] * ba).astype(BF16)

    def out_piece(r0, c):
        rows = slice(r0, r0 + half)
        cols = slice(c * MIX_COL_CHUNK, (c + 1) * MIX_COL_CHUNK)
        out = h_ref[rows, cols] + _dot(merged_scr[rows, :], wout_ref[:, cols])
        o_ref[rows, cols] = out
        sq = out * out
        part = sq[:, 0:LANES]
        for k in range(1, MIX_COL_CHUNK // LANES):
            part = part + sq[:, k * LANES:(k + 1) * LANES]
        if c > 0:
            part = part + sumsq_scr[rows, :]
        if c + 1 < n_gate_chunks:
            sumsq_scr[rows, :] = part
        else:
            mean_sq = jnp.sum(part, axis=-1, keepdims=True) * (1.0 / d_model)
            rscale_ref[rows, :] = lax.rsqrt(mean_sq + RMS_EPS)

    def output_pieces(r0):
        return ([functools.partial(merge_piece, r0, c) for c in range(n_gate_chunks)]
                + [functools.partial(out_piece, r0, c) for c in range(n_gate_chunks)])

    def attn_gate_piece(r0, c):
        rows = slice(r0, r0 + half)
        cols_a = slice(d_model + c * MIX_COL_CHUNK, d_model + (c + 1) * MIX_COL_CHUNK)
        gate_scr[n_gate_chunks + c, rows, :] = jax.nn.sigmoid(
            _dot(u_scr[rows, :], w_in_ref[:, off_ga + c * MIX_COL_CHUNK: off_ga + (c + 1) * MIX_COL_CHUNK])
            + gbias_ref[:, cols_a])

    n_units = (tm // BLOCK) * N_KV_HEADS
    dense = ([(functools.partial(attn_gate_piece, r0, c), functools.partial(pool_piece, r0, c))
              for r0 in (0, half) for c in range(n_gate_chunks)]
             + [(piece,) for piece in output_pieces(0)])
    assert len(dense) == n_units
    scores = score_unit(0)
    for unit in range(n_units):
        next_scores = score_unit(unit + 1) if unit + 1 < n_units else None
        for piece in dense[unit]:
            piece()
        finish_unit(unit, scores)
        scores = next_scores
    for piece in output_pieces(half):
        piece()


def _mixer(h, seq_len, mix_norm, w_in, pool_w, pool_scale, w_pool_out, q_norm, k_norm, sinks,
           w_attn_out, gate_bias, w_out):
    n, d = h.shape
    tm = MIX_TILE
    pool_width = w_pool_out.shape[0]
    attn_width = w_attn_out.shape[0]
    kv_width = N_KV_HEADS * HEAD_DIM
    in_width = w_in.shape[1]
    assert seq_len % tm == 0 and tm % BLOCK == 0
    assert pool_width == len(POOL_WINDOWS) * POOL_GROUP and attn_width == N_HEADS * HEAD_DIM
    assert in_width == pool_width + attn_width + 2 * kv_width + 2 * d and kv_width == LANES
    row_spec = pl.BlockSpec((tm, d), lambda i: (i, 0))
    kern = functools.partial(_mixer_kernel, tiles_per_seq=seq_len // tm, d_model=d,
                             pool_width=pool_width, attn_width=attn_width, kv_width=kv_width)
    lane_rows = jnp.concatenate(
        [jnp.tile(q_norm, LANES // HEAD_DIM)[None], jnp.tile(k_norm, LANES // HEAD_DIM)[None],
         jnp.repeat(sinks.reshape(N_HEADS // 2, 2), HEAD_DIM, axis=1)], axis=0)
    return pl.pallas_call(
        kern,
        grid=(n // tm,),
        in_specs=[row_spec, _resident((1, d)), _resident((d, in_width)), _resident((1, 2 * d)),
                  _resident(pool_w.shape), _resident((1, pool_width)), _resident((pool_width, d)),
                  _resident(lane_rows.shape),
                  _resident((attn_width, d)), _resident((d, d))],
        out_specs=[row_spec, pl.BlockSpec((tm, 1), lambda i: (i, 0))],
        out_shape=[jax.ShapeDtypeStruct((n, d), F32), jax.ShapeDtypeStruct((n, 1), F32)],
        scratch_shapes=[
            pltpu.VMEM((tm, d), BF16),
            pltpu.VMEM((len(POOL_WINDOWS), POOL_HALO + tm, POOL_GROUP), F32),
            pltpu.VMEM((attn_width // LANES, tm, LANES), F32),
            pltpu.VMEM((attn_width // LANES, tm, LANES), BF16),
            pltpu.VMEM((N_KV_HEADS, BLOCK + tm, LANES), BF16),
            pltpu.VMEM((N_KV_HEADS, BLOCK + tm, LANES), BF16),
            pltpu.VMEM((N_KV_HEADS, BLOCK + tm, 2 * LANES), BF16),
            pltpu.VMEM((N_KV_HEADS, BLOCK + tm, 2 * LANES), BF16),
            pltpu.VMEM((attn_width // LANES, tm, LANES), BF16),
            pltpu.VMEM((tm, pool_width), BF16),
            pltpu.VMEM((2 * d // MIX_COL_CHUNK, tm, MIX_COL_CHUNK), F32),
            pltpu.VMEM((tm, d), BF16),
            pltpu.VMEM((tm, LANES), F32),
        ],
        compiler_params=pltpu.CompilerParams(dimension_semantics=("arbitrary",),
                                             vmem_limit_bytes=VMEM_LIMIT),
        name="mixer",
    )(h, mix_norm.reshape(1, d), w_in, gate_bias.reshape(1, 2 * d),
      pool_w, pool_scale.reshape(1, pool_width), w_pool_out, lane_rows,
      w_attn_out, w_out)


def kernel(x, ffn1_norm, ffn1_w_gate, ffn1_w_up, ffn1_w_down, mix_norm, w_in, pool_w, pool_scale,
           w_pool_out, q_norm, k_norm, sinks, w_attn_out, gate_bias, w_out, ffn2_norm, ffn2_w_gate,
           ffn2_w_up, ffn2_w_down):
    b, s, d = x.shape
    h = x.reshape(b * s, d)
    h, w_in_b, w_pool_out_b, w_attn_out_b, w_out_b = _ffn(
        h, ffn1_norm, ffn1_w_gate, ffn1_w_up, ffn1_w_down, name="ffn1",
        cast_for_later=(w_in, w_pool_out, w_attn_out, w_out))
    h, h_row_scale = _mixer(h, s, mix_norm, w_in_b, pool_w, pool_scale, w_pool_out_b, q_norm, k_norm,
                            sinks, w_attn_out_b, gate_bias, w_out_b)
    (h,) = _ffn(h, ffn2_norm, ffn2_w_gate, ffn2_w_up, ffn2_w_down, name="ffn2",
                row_scale=h_row_scale)
    return h.reshape(b, s, d)
```

```python
import functools
import math

import jax
import jax.numpy as jnp
from jax import lax
from jax.experimental import pallas as pl
from jax.experimental.pallas import tpu as pltpu

F32 = jnp.float32
BF16 = jnp.bfloat16

RMS_EPS = 1e-6
HEAD_DIM = 64
N_HEADS = 16
N_KV_HEADS = 2
PAIRS_PER_KV = N_HEADS // N_KV_HEADS // 2
BLOCK = 128
POOL_WINDOWS = (2, 4, 8, 16)
POOL_GROUP = 128
POOL_HALO = 16
LANES = 128
BF16_SUBLANES = 16
V7X_VMEM_BYTES = 64 * 1024 * 1024
VMEM_LIMIT = V7X_VMEM_BYTES - 4 * 1024 * 1024
LOG2E = math.log2(math.e)
MASK_BIAS = -3.0e38

FFN_TILE = 1024
FFN_CHUNK = 256
FFN_DOWN_GROUP = 4
FFN_WEIGHT_SLABS = 16
FFN_STAGE_SLOTS = 3
MIX_TILE = 1024
MIX_COL_CHUNK = 256


def _dot(a, b):
    return jnp.dot(a, b, preferred_element_type=F32)


def _dot_rows_split(a, b):
    half = a.shape[0] // 2
    return jnp.concatenate([_dot(a[:half], b), _dot(a[half:], b)], axis=0)


def _dot_nt(a, b):
    return lax.dot_general(a, b, (((1,), (1,)), ((), ())), preferred_element_type=F32)


def _lane_blocks(ref, cols, rows=slice(None)):
    assert cols.start % LANES == 0 and cols.stop % LANES == 0
    return jnp.concatenate([ref[j, rows, :] for j in range(cols.start // LANES, cols.stop // LANES)],
                           axis=1)


def _rmsnorm_rows(x, g):
    ms = jnp.mean(x * x, axis=-1, keepdims=True)
    return x * lax.rsqrt(ms + RMS_EPS) * g


def _cast_slabs(f32_refs, bf16_refs):
    for src, dst in zip(f32_refs, bf16_refs, strict=True):
        dst[...] = src[...].astype(BF16)


def _ffn_kernel(x_ref, g_ref, wg_hbm, wu_hbm, wd_hbm, *rest, d_ff, chunk, n_cast, n_slabs,
                has_row_scale):
    rscale_ref = rest[0] if has_row_scale else None
    rest = rest[int(has_row_scale):]
    cast_in, o_ref, cast_out = rest[:n_cast], rest[n_cast], rest[n_cast + 1:2 * n_cast + 1]
    wg_v, wu_v, wd_v, stage_g, stage_u, stage_d, sems, acc_ref = rest[2 * n_cast + 1:]
    n_slots = stage_g.shape[0]
    streams = ((wg_hbm, stage_g, wg_v), (wu_hbm, stage_u, wu_v), (wd_hbm, stage_d, wd_v))

    def slab_copies(s):
        slot = s % n_slots
        return [pltpu.make_async_copy(hbm.at[pl.ds(s * stage.shape[1], stage.shape[1]), :],
                                      stage.at[slot], sems.at[k, slot])
                for k, (hbm, stage, _) in enumerate(streams)]

    @pl.when(pl.program_id(0) == 0)
    def _():
        for s in range(min(n_slots, n_slabs)):
            for cp in slab_copies(s):
                cp.start()
        for s in range(n_slabs):
            for cp in slab_copies(s):
                cp.wait()
            for _, stage, dst in streams:
                rows = stage.shape[1]
                slab = stage[s % n_slots].astype(BF16)
                for j in range(dst.shape[0]):
                    dst[j, s * rows:(s + 1) * rows, :] = slab[:, j * LANES:(j + 1) * LANES]
            if s + n_slots < n_slabs:
                for cp in slab_copies(s + n_slots):
                    cp.start()

    x = x_ref[...]
    if has_row_scale:
        xn = (x * rscale_ref[...] * g_ref[...]).astype(BF16)
    else:
        xn = _rmsnorm_rows(x, g_ref[...]).astype(BF16)
    n_chunks = d_ff // chunk
    for first in range(0, n_chunks, FFN_DOWN_GROUP):
        group = range(first, min(first + FFN_DOWN_GROUP, n_chunks))
        acts = []
        for c in group:
            cols = slice(c * chunk, (c + 1) * chunk)
            g = _dot(xn, _lane_blocks(wg_v, cols))
            u = _dot(xn, _lane_blocks(wu_v, cols))
            acts.append((g * jax.nn.sigmoid(g) * u).astype(BF16))
        d = _dot(jnp.concatenate(acts, axis=1),
                 _lane_blocks(wd_v, slice(0, wd_v.shape[0] * LANES),
                              rows=slice(group[0] * chunk, (group[-1] + 1) * chunk)))
        if first == 0:
            acc_ref[...] = d
        else:
            acc_ref[...] += d
    o_ref[...] = x + 0.5 * acc_ref[...]
    _cast_slabs(cast_in, cast_out)


def _resident(shape):
    return pl.BlockSpec(shape, lambda i: (0,) * len(shape), pipeline_mode=pl.Buffered(1))


def _slab_specs(weights, steps):
    in_specs, out_specs, out_shapes = [], [], []
    for w in weights:
        rows, cols = w.shape
        assert rows % (steps * BF16_SUBLANES) == 0
        in_specs.append(pl.BlockSpec((rows // steps, cols), lambda i: (i, 0)))
        out_specs.append(pl.BlockSpec((rows // steps, cols), lambda i: (i, 0)))
        out_shapes.append(jax.ShapeDtypeStruct(w.shape, BF16))
    return in_specs, out_specs, out_shapes


def _ffn(x, norm_g, w_gate, w_up, w_down, *, name, row_scale=None, cast_for_later=()):
    n, d = x.shape
    d_ff = w_gate.shape[1]
    assert n % FFN_TILE == 0 and d_ff % FFN_CHUNK == 0
    assert d % (FFN_WEIGHT_SLABS * BF16_SUBLANES) == 0 and d_ff % (FFN_WEIGHT_SLABS * BF16_SUBLANES) == 0
    steps = n // FFN_TILE
    row_spec = pl.BlockSpec((FFN_TILE, d), lambda i: (i, 0))
    in_hbm = pl.BlockSpec(memory_space=pl.ANY)
    cast_in, cast_out, cast_shapes = _slab_specs(cast_for_later, steps)
    scale_spec = [] if row_scale is None else [pl.BlockSpec((FFN_TILE, 1), lambda i: (i, 0))]
    scale_arg = [] if row_scale is None else [row_scale]
    return pl.pallas_call(
        functools.partial(_ffn_kernel, d_ff=d_ff, chunk=FFN_CHUNK, n_cast=len(cast_for_later),
                          n_slabs=FFN_WEIGHT_SLABS, has_row_scale=row_scale is not None),
        grid=(steps,),
        in_specs=[row_spec, _resident((1, d)), in_hbm, in_hbm, in_hbm] + scale_spec + cast_in,
        out_specs=[row_spec] + cast_out,
        out_shape=[jax.ShapeDtypeStruct((n, d), F32)] + cast_shapes,
        scratch_shapes=[
            pltpu.VMEM((d_ff // LANES, d, LANES), BF16),
            pltpu.VMEM((d_ff // LANES, d, LANES), BF16),
            pltpu.VMEM((d // LANES, d_ff, LANES), BF16),
            pltpu.VMEM((FFN_STAGE_SLOTS, d // FFN_WEIGHT_SLABS, d_ff), F32),
            pltpu.VMEM((FFN_STAGE_SLOTS, d // FFN_WEIGHT_SLABS, d_ff), F32),
            pltpu.VMEM((FFN_STAGE_SLOTS, d_ff // FFN_WEIGHT_SLABS, d), F32),
            pltpu.SemaphoreType.DMA((3, FFN_STAGE_SLOTS)),
            pltpu.VMEM((FFN_TILE, d), F32),
        ],
        compiler_params=pltpu.CompilerParams(dimension_semantics=("arbitrary",),
                                             vmem_limit_bytes=VMEM_LIMIT),
        name=name,
    )(x, norm_g.reshape(1, d), w_gate, w_up, w_down, *scale_arg, *cast_for_later)


def _head_mean_sq(x, low_half):
    x2 = x * x
    s_low = jnp.sum(jnp.where(low_half, x2, 0.0), axis=-1, keepdims=True)
    s_all = jnp.sum(x2, axis=-1, keepdims=True)
    return jnp.where(low_half, s_low, s_all - s_low) * (1.0 / HEAD_DIM)


def _split_kv_heads(x):
    swapped = pltpu.roll(x, HEAD_DIM, axis=1)
    low = lax.broadcasted_iota(jnp.int32, x.shape, 1) < HEAD_DIM
    zero = jnp.zeros_like(x)
    parts = (jnp.where(low, x, zero), jnp.where(low, zero, swapped),
             jnp.where(low, swapped, zero), jnp.where(low, zero, x))
    return tuple(p.astype(BF16) for p in parts)


def _mixer_kernel(h_ref, mixg_ref, w_in_ref, gbias_ref, poolw_ref, pscale_ref, wpo_ref,
                  lanes_ref, wao_ref, wout_ref, o_ref, rscale_ref,
                  u_scr, xp_scr, q32_scr, q_scr, ka_scr, kb_scr, ve_scr, vo_scr, attn_scr, mixed_scr,
                  gate_scr, merged_scr, sumsq_scr,
                  *, tiles_per_seq, d_model, pool_width, attn_width, kv_width):
    tm = h_ref.shape[0]
    i = pl.program_id(0)
    tile_in_seq = i % tiles_per_seq
    seq_start = tile_in_seq == 0
    lane128 = lax.broadcasted_iota(jnp.int32, (1, LANES), 1)
    low_half = lane128 < HEAD_DIM

    off_q = pool_width
    off_k = off_q + attn_width
    off_gp = off_k + 2 * kv_width
    off_ga = off_gp + d_model

    @pl.when(seq_start)
    def _():
        xp_scr[:, 0:POOL_HALO, :] = jnp.zeros((len(POOL_WINDOWS), POOL_HALO, POOL_GROUP), F32)
        zk = jnp.zeros((BLOCK, LANES), BF16)
        ones_e = jnp.broadcast_to(jnp.where(low_half, 1.0, 0.0).astype(BF16), (BLOCK + tm, LANES))
        ones_o = jnp.broadcast_to(jnp.where(low_half, 0.0, 1.0).astype(BF16), (BLOCK + tm, LANES))
        for kv in range(N_KV_HEADS):
            ka_scr[kv, 0:BLOCK, :] = zk
            kb_scr[kv, 0:BLOCK, :] = zk
            ve_scr[kv, 0:BLOCK, 0:LANES] = zk
            vo_scr[kv, 0:BLOCK, 0:LANES] = zk
            ve_scr[kv, :, LANES:2 * LANES] = ones_e
            vo_scr[kv, :, LANES:2 * LANES] = ones_o

    @pl.when(jnp.logical_not(seq_start))
    def _():
        xp_scr[:, 0:POOL_HALO, :] = xp_scr[:, tm:tm + POOL_HALO, :]
        for kv in range(N_KV_HEADS):
            ka_scr[kv, 0:BLOCK, :] = ka_scr[kv, tm:tm + BLOCK, :]
            kb_scr[kv, 0:BLOCK, :] = kb_scr[kv, tm:tm + BLOCK, :]
            ve_scr[kv, 0:BLOCK, 0:LANES] = ve_scr[kv, tm:tm + BLOCK, 0:LANES]
            vo_scr[kv, 0:BLOCK, 0:LANES] = vo_scr[kv, tm:tm + BLOCK, 0:LANES]

    u_scr[...] = _rmsnorm_rows(h_ref[...], mixg_ref[...]).astype(BF16)

    xp = _dot(u_scr[...], w_in_ref[:, 0:pool_width])
    for gi in range(len(POOL_WINDOWS)):
        xp_scr[gi, POOL_HALO:POOL_HALO + tm, :] = xp[:, gi * POOL_GROUP:(gi + 1) * POOL_GROUP]
    q32 = _dot(u_scr[...], w_in_ref[:, off_q:off_q + attn_width])
    for qc in range(attn_width // LANES):
        q32_scr[qc] = q32[:, qc * LANES:(qc + 1) * LANES]
    kv_proj = _dot_rows_split(u_scr[...], w_in_ref[:, off_k:off_k + 2 * kv_width])

    t = tile_in_seq * tm + lax.broadcasted_iota(jnp.int32, (tm, 1), 0)
    pooled = []
    for gi, w in enumerate(POOL_WINDOWS):
        ext = xp_scr[gi]
        wsum = ext
        shift = 1
        while shift < w:
            wsum = wsum + pltpu.roll(wsum, shift, axis=0)
            shift *= 2
        count = jnp.minimum(t + 1, w).astype(F32)
        pooled.append((wsum[POOL_HALO:] / count - ext[POOL_HALO:]).astype(BF16))
    for c in range(len(POOL_WINDOWS) // 2):
        cols = slice(c * 2 * POOL_GROUP, (c + 1) * 2 * POOL_GROUP)
        pair = jnp.concatenate(pooled[2 * c:2 * c + 2], axis=1)
        zeros = jnp.zeros((POOL_GROUP, POOL_GROUP), F32)
        w_pair = jnp.concatenate(
            [jnp.concatenate([poolw_ref[2 * c], zeros], axis=1),
             jnp.concatenate([zeros, poolw_ref[2 * c + 1]], axis=1)], axis=0).astype(BF16)
        mixed_scr[:, cols] = (_dot(pair, w_pair) * pscale_ref[:, cols]).astype(BF16)

    q_gain = lanes_ref[0:1, :] * (HEAD_DIM ** -0.5 * LOG2E)
    n_gate_chunks = d_model // MIX_COL_CHUNK
    q_chunks_per_gate_chunk = attn_width // LANES // n_gate_chunks
    for c in range(n_gate_chunks):
        cols = slice(c * MIX_COL_CHUNK, (c + 1) * MIX_COL_CHUNK)
        gp = jax.nn.sigmoid(
            _dot(u_scr[...], w_in_ref[:, off_gp + c * MIX_COL_CHUNK: off_gp + (c + 1) * MIX_COL_CHUNK])
            + gbias_ref[:, cols])
        gate_scr[c] = gp
        for qc in range(c * q_chunks_per_gate_chunk, (c + 1) * q_chunks_per_gate_chunk):
            q = q32_scr[qc]
            ms = _head_mean_sq(q, low_half)
            q_scr[qc] = (q * lax.rsqrt(ms + RMS_EPS) * q_gain).astype(BF16)

    k = kv_proj[:, 0:kv_width]
    k = k * lax.rsqrt(_head_mean_sq(k, low_half) + RMS_EPS) * lanes_ref[1:2, :]
    ka0, kb0, ka1, kb1 = _split_kv_heads(k)
    ka_scr[0, BLOCK:BLOCK + tm, :] = ka0
    kb_scr[0, BLOCK:BLOCK + tm, :] = kb0
    ka_scr[1, BLOCK:BLOCK + tm, :] = ka1
    kb_scr[1, BLOCK:BLOCK + tm, :] = kb1
    va0, vb0, va1, vb1 = _split_kv_heads(kv_proj[:, kv_width:2 * kv_width])
    ve_scr[0, BLOCK:BLOCK + tm, 0:LANES] = va0
    vo_scr[0, BLOCK:BLOCK + tm, 0:LANES] = vb0
    ve_scr[1, BLOCK:BLOCK + tm, 0:LANES] = va1
    vo_scr[1, BLOCK:BLOCK + tm, 0:LANES] = vb1

    key = lax.broadcasted_iota(jnp.int32, (2 * BLOCK, BLOCK), 0)
    qry = lax.broadcasted_iota(jnp.int32, (2 * BLOCK, BLOCK), 1)
    band = (key > qry) & (key <= qry + BLOCK)
    first_key = jnp.where(seq_start, BLOCK, 0)
    bias = jnp.where(band, 0.0, MASK_BIAS).astype(BF16)
    bias_first = jnp.where(band & (key >= first_key), 0.0, MASK_BIAS).astype(BF16)
    r4 = lax.broadcasted_iota(jnp.int32, (PAIRS_PER_KV * BLOCK, BLOCK), 0) % BLOCK
    c4 = lax.broadcasted_iota(jnp.int32, (PAIRS_PER_KV * BLOCK, BLOCK), 1)
    onehot = jnp.where(r4 == c4, 1.0, 0.0).astype(BF16)
    sinks2 = lanes_ref[2:2 + N_HEADS // 2, :] * LOG2E

    def score_unit(unit):
        b, kv = divmod(unit, N_KV_HEADS)
        rows = slice(b * BLOCK, (b + 1) * BLOCK)
        keys = slice(b * BLOCK, (b + 2) * BLOCK)
        key_bias = bias_first if b == 0 else bias
        base = kv * PAIRS_PER_KV
        qs = jnp.concatenate([q_scr[base + j, rows, :] for j in range(PAIRS_PER_KV)], axis=0)
        qs = jnp.concatenate([qs, onehot], axis=1)
        k_even = jnp.concatenate([ka_scr[kv, keys, :], key_bias], axis=1)
        k_odd = jnp.concatenate([kb_scr[kv, keys, :], key_bias], axis=1)
        return _dot_nt(qs, k_even), _dot_nt(qs, k_odd)

    def finish_unit(unit, scores):
        b, kv = divmod(unit, N_KV_HEADS)
        rows = slice(b * BLOCK, (b + 1) * BLOCK)
        keys = slice(b * BLOCK, (b + 2) * BLOCK)
        base = kv * PAIRS_PER_KV
        probs = ([], [])
        sink_terms = []
        for j in range(PAIRS_PER_KV):
            m = []
            for par in range(2):
                s = scores[par][j * BLOCK:(j + 1) * BLOCK]
                m.append(jnp.max(s, axis=-1, keepdims=True))
                probs[par].append(jnp.exp2(s - m[par]).astype(BF16))
            sink_terms.append(jnp.exp2(sinks2[base + j:base + j + 1, :]
                                       - jnp.where(low_half, m[0], m[1])))
        p = jnp.concatenate([jnp.concatenate(probs[0], axis=0),
                             jnp.concatenate(probs[1], axis=0)], axis=1)
        v_stack = jnp.concatenate([ve_scr[kv, keys, :], vo_scr[kv, keys, :]], axis=0)
        acc = _dot_rows_split(p, v_stack)
        den = acc[:, LANES:2 * LANES] + jnp.concatenate(sink_terms, axis=0)
        out = (acc[:, 0:LANES] / den).astype(BF16)
        for j in range(PAIRS_PER_KV):
            attn_scr[base + j, rows, :] = out[j * BLOCK:(j + 1) * BLOCK]

    half = tm // 2

    def pool_piece(r0, c):
        rows = slice(r0, r0 + half)
        cols = slice(c * MIX_COL_CHUNK, (c + 1) * MIX_COL_CHUNK)
        gate_scr[c, rows, :] = gate_scr[c, rows, :] * _dot(mixed_scr[rows, :], wpo_ref[:, cols])

    def merge_piece(r0, c):
        rows = slice(r0, r0 + half)
        cols = slice(c * MIX_COL_CHUNK, (c + 1) * MIX_COL_CHUNK)
        attn = jnp.concatenate([attn_scr[j, rows, :] for j in range(attn_width // LANES)], axis=1)
        ba = _dot(attn, wao_ref[:, cols])
        merged_scr[rows, cols] = (gate_scr[c, rows, :]
                                  + gate_scr[n_gate_chunks + c, rows, :] * ba).astype(BF16)

    def out_piece(r0, c):
        rows = slice(r0, r0 + half)
        cols = slice(c * MIX_COL_CHUNK, (c + 1) * MIX_COL_CHUNK)
        out = h_ref[rows, cols] + _dot(merged_scr[rows, :], wout_ref[:, cols])
        o_ref[rows, cols] = out
        sq = out * out
        part = sq[:, 0:LANES]
        for k in range(1, MIX_COL_CHUNK // LANES):
            part = part + sq[:, k * LANES:(k + 1) * LANES]
        if c > 0:
            part = part + sumsq_scr[rows, :]
        if c + 1 < n_gate_chunks:
            sumsq_scr[rows, :] = part
        else:
            mean_sq = jnp.sum(part, axis=-1, keepdims=True) * (1.0 / d_model)
            rscale_ref[rows, :] = lax.rsqrt(mean_sq + RMS_EPS)

    def output_pieces(r0):
        return ([functools.partial(merge_piece, r0, c) for c in range(n_gate_chunks)]
                + [functools.partial(out_piece, r0, c) for c in range(n_gate_chunks)])

    def attn_gate_piece(r0, c):
        rows = slice(r0, r0 + half)
        cols_a = slice(d_model + c * MIX_COL_CHUNK, d_model + (c + 1) * MIX_COL_CHUNK)
        gate_scr[n_gate_chunks + c, rows, :] = jax.nn.sigmoid(
            _dot(u_scr[rows, :], w_in_ref[:, off_ga + c * MIX_COL_CHUNK: off_ga + (c + 1) * MIX_COL_CHUNK])
            + gbias_ref[:, cols_a])

    n_units = (tm // BLOCK) * N_KV_HEADS
    dense = ([(functools.partial(attn_gate_piece, r0, c), functools.partial(pool_piece, r0, c))
              for r0 in (0, half) for c in range(n_gate_chunks)]
             + [(piece,) for piece in output_pieces(0)])
    assert len(dense) == n_units
    scores = score_unit(0)
    for unit in range(n_units):
        next_scores = score_unit(unit + 1) if unit + 1 < n_units else None
        for piece in dense[unit]:
            piece()
        finish_unit(unit, scores)
        scores = next_scores
    for piece in output_pieces(half):
        piece()


def _mixer(h, seq_len, mix_norm, w_in, pool_w, pool_scale, w_pool_out, q_norm, k_norm, sinks,
           w_attn_out, gate_bias, w_out):
    n, d = h.shape
    tm = MIX_TILE
    pool_width = w_pool_out.shape[0]
    attn_width = w_attn_out.shape[0]
    kv_width = N_KV_HEADS * HEAD_DIM
    in_width = w_in.shape[1]
    assert seq_len % tm == 0 and tm % BLOCK == 0
    assert pool_width == len(POOL_WINDOWS) * POOL_GROUP and attn_width == N_HEADS * HEAD_DIM
    assert in_width == pool_width + attn_width + 2 * kv_width + 2 * d and kv_width == LANES
    row_spec = pl.BlockSpec((tm, d), lambda i: (i, 0))
    kern = functools.partial(_mixer_kernel, tiles_per_seq=seq_len // tm, d_model=d,
                             pool_width=pool_width, attn_width=attn_width, kv_width=kv_width)
    lane_rows = jnp.concatenate(
        [jnp.tile(q_norm, LANES // HEAD_DIM)[None], jnp.tile(k_norm, LANES // HEAD_DIM)[None],
         jnp.repeat(sinks.reshape(N_HEADS // 2, 2), HEAD_DIM, axis=1)], axis=0)
    return pl.pallas_call(
        kern,
        grid=(n // tm,),
        in_specs=[row_spec, _resident((1, d)), _resident((d, in_width)), _resident((1, 2 * d)),
                  _resident(pool_w.shape), _resident((1, pool_width)), _resident((pool_width, d)),
                  _resident(lane_rows.shape),
                  _resident((attn_width, d)), _resident((d, d))],
        out_specs=[row_spec, pl.BlockSpec((tm, 1), lambda i: (i, 0))],
        out_shape=[jax.ShapeDtypeStruct((n, d), F32), jax.ShapeDtypeStruct((n, 1), F32)],
        scratch_shapes=[
            pltpu.VMEM((tm, d), BF16),
            pltpu.VMEM((len(POOL_WINDOWS), POOL_HALO + tm, POOL_GROUP), F32),
            pltpu.VMEM((attn_width // LANES, tm, LANES), F32),
            pltpu.VMEM((attn_width // LANES, tm, LANES), BF16),
            pltpu.VMEM((N_KV_HEADS, BLOCK + tm, LANES), BF16),
            pltpu.VMEM((N_KV_HEADS, BLOCK + tm, LANES), BF16),
            pltpu.VMEM((N_KV_HEADS, BLOCK + tm, 2 * LANES), BF16),
            pltpu.VMEM((N_KV_HEADS, BLOCK + tm, 2 * LANES), BF16),
            pltpu.VMEM((attn_width // LANES, tm, LANES), BF16),
            pltpu.VMEM((tm, pool_width), BF16),
            pltpu.VMEM((2 * d // MIX_COL_CHUNK, tm, MIX_COL_CHUNK), F32),
            pltpu.VMEM((tm, d), BF16),
            pltpu.VMEM((tm, LANES), F32),
        ],
        compiler_params=pltpu.CompilerParams(dimension_semantics=("arbitrary",),
                                             vmem_limit_bytes=VMEM_LIMIT),
        name="mixer",
    )(h, mix_norm.reshape(1, d), w_in, gate_bias.reshape(1, 2 * d),
      pool_w, pool_scale.reshape(1, pool_width), w_pool_out, lane_rows,
      w_attn_out, w_out)


def kernel(x, ffn1_norm, ffn1_w_gate, ffn1_w_up, ffn1_w_down, mix_norm, w_in, pool_w, pool_scale,
           w_pool_out, q_norm, k_norm, sinks, w_attn_out, gate_bias, w_out, ffn2_norm, ffn2_w_gate,
           ffn2_w_up, ffn2_w_down):
    b, s, d = x.shape
    h = x.reshape(b * s, d)
    h, w_in_b, w_pool_out_b, w_attn_out_b, w_out_b = _ffn(
        h, ffn1_norm, ffn1_w_gate, ffn1_w_up, ffn1_w_down, name="ffn1",
        cast_for_later=(w_in, w_pool_out, w_attn_out, w_out))
    h, h_row_scale = _mixer(h, s, mix_norm, w_in_b, pool_w, pool_scale, w_pool_out_b, q_norm, k_norm,
                            sinks, w_attn_out_b, gate_bias, w_out_b)
    (h,) = _ffn(h, ffn2_norm, ffn2_w_gate, ffn2_w_up, ffn2_w_down, name="ffn2",
                row_scale=h_row_scale)
    return h.reshape(b, s, d)
```

```python
import functools
import math

import jax
import jax.numpy as jnp
from jax import lax
from jax.experimental import pallas as pl
from jax.experimental.pallas import tpu as pltpu

F32 = jnp.float32
BF16 = jnp.bfloat16

RMS_EPS = 1e-6
HEAD_DIM = 64
N_HEADS = 16
N_KV_HEADS = 2
PAIRS_PER_KV = N_HEADS // N_KV_HEADS // 2
BLOCK = 128
POOL_WINDOWS = (2, 4, 8, 16)
POOL_GROUP = 128
POOL_HALO = 16
LANES = 128
BF16_SUBLANES = 16
V7X_VMEM_BYTES = 64 * 1024 * 1024
VMEM_LIMIT = V7X_VMEM_BYTES - 4 * 1024 * 1024
LOG2E = math.log2(math.e)
MASK_BIAS = -3.0e38

FFN_TILE = 1024
FFN_CHUNK = 256
FFN_DOWN_GROUP = 4
FFN_WEIGHT_SLABS = 16
FFN_STAGE_SLOTS = 4
MIX_TILE = 1024
MIX_COL_CHUNK = 256


def _dot(a, b):
    return jnp.dot(a, b, preferred_element_type=F32)


def _dot_rows_split(a, b):
    half = a.shape[0] // 2
    return jnp.concatenate([_dot(a[:half], b), _dot(a[half:], b)], axis=0)


def _dot_nt(a, b):
    return lax.dot_general(a, b, (((1,), (1,)), ((), ())), preferred_element_type=F32)


def _rmsnorm_rows(x, g):
    ms = jnp.mean(x * x, axis=-1, keepdims=True)
    return x * lax.rsqrt(ms + RMS_EPS) * g


def _cast_slabs(f32_refs, bf16_refs):
    for src, dst in zip(f32_refs, bf16_refs, strict=True):
        dst[...] = src[...].astype(BF16)


def _ffn_kernel(x_ref, g_ref, wg_hbm, wu_hbm, wd_hbm, *rest, d_ff, chunk, n_cast, n_slabs,
                has_row_scale):
    rscale_ref = rest[0] if has_row_scale else None
    rest = rest[int(has_row_scale):]
    cast_in, o_ref, cast_out = rest[:n_cast], rest[n_cast], rest[n_cast + 1:2 * n_cast + 1]
    wg_v, wu_v, wd_v, stage_g, stage_u, stage_d, sems, acc_ref = rest[2 * n_cast + 1:]
    n_slots = stage_g.shape[0]
    streams = ((wg_hbm, stage_g, wg_v), (wu_hbm, stage_u, wu_v), (wd_hbm, stage_d, wd_v))

    def slab_copies(s):
        slot = s % n_slots
        return [pltpu.make_async_copy(hbm.at[pl.ds(s * stage.shape[1], stage.shape[1]), :],
                                      stage.at[slot], sems.at[k, slot])
                for k, (hbm, stage, _) in enumerate(streams)]

    @pl.when(pl.program_id(0) == 0)
    def _():
        for s in range(min(n_slots, n_slabs)):
            for cp in slab_copies(s):
                cp.start()
        for s in range(n_slabs):
            for cp in slab_copies(s):
                cp.wait()
            for _, stage, dst in streams:
                rows = stage.shape[1]
                dst[s * rows:(s + 1) * rows, :] = stage[s % n_slots].astype(BF16)
            if s + n_slots < n_slabs:
                for cp in slab_copies(s + n_slots):
                    cp.start()

    x = x_ref[...]
    if has_row_scale:
        xn = (x * rscale_ref[...] * g_ref[...]).astype(BF16)
    else:
        xn = _rmsnorm_rows(x, g_ref[...]).astype(BF16)
    n_chunks = d_ff // chunk
    for first in range(0, n_chunks, FFN_DOWN_GROUP):
        group = range(first, min(first + FFN_DOWN_GROUP, n_chunks))
        acts = []
        for c in group:
            cols = slice(c * chunk, (c + 1) * chunk)
            g = _dot(xn, wg_v[:, cols])
            u = _dot(xn, wu_v[:, cols])
            acts.append((g * jax.nn.sigmoid(g) * u).astype(BF16))
        d = _dot(jnp.concatenate(acts, axis=1), wd_v[group[0] * chunk:(group[-1] + 1) * chunk, :])
        if first == 0:
            acc_ref[...] = d
        else:
            acc_ref[...] += d
    o_ref[...] = x + 0.5 * acc_ref[...]
    _cast_slabs(cast_in, cast_out)


def _resident(shape):
    return pl.BlockSpec(shape, lambda i: (0,) * len(shape), pipeline_mode=pl.Buffered(1))


def _slab_specs(weights, steps):
    in_specs, out_specs, out_shapes = [], [], []
    for w in weights:
        rows, cols = w.shape
        assert rows % (steps * BF16_SUBLANES) == 0
        in_specs.append(pl.BlockSpec((rows // steps, cols), lambda i: (i, 0)))
        out_specs.append(pl.BlockSpec((rows // steps, cols), lambda i: (i, 0)))
        out_shapes.append(jax.ShapeDtypeStruct(w.shape, BF16))
    return in_specs, out_specs, out_shapes


def _ffn(x, norm_g, w_gate, w_up, w_down, *, name, row_scale=None, cast_for_later=()):
    n, d = x.shape
    d_ff = w_gate.shape[1]
    assert n % FFN_TILE == 0 and d_ff % FFN_CHUNK == 0
    assert d % (FFN_WEIGHT_SLABS * BF16_SUBLANES) == 0 and d_ff % (FFN_WEIGHT_SLABS * BF16_SUBLANES) == 0
    steps = n // FFN_TILE
    row_spec = pl.BlockSpec((FFN_TILE, d), lambda i: (i, 0))
    in_hbm = pl.BlockSpec(memory_space=pl.ANY)
    cast_in, cast_out, cast_shapes = _slab_specs(cast_for_later, steps)
    scale_spec = [] if row_scale is None else [pl.BlockSpec((FFN_TILE, 1), lambda i: (i, 0))]
    scale_arg = [] if row_scale is None else [row_scale]
    return pl.pallas_call(
        functools.partial(_ffn_kernel, d_ff=d_ff, chunk=FFN_CHUNK, n_cast=len(cast_for_later),
                          n_slabs=FFN_WEIGHT_SLABS, has_row_scale=row_scale is not None),
        grid=(steps,),
        in_specs=[row_spec, _resident((1, d)), in_hbm, in_hbm, in_hbm] + scale_spec + cast_in,
        out_specs=[row_spec] + cast_out,
        out_shape=[jax.ShapeDtypeStruct((n, d), F32)] + cast_shapes,
        scratch_shapes=[
            pltpu.VMEM((d, d_ff), BF16),
            pltpu.VMEM((d, d_ff), BF16),
            pltpu.VMEM((d_ff, d), BF16),
            pltpu.VMEM((FFN_STAGE_SLOTS, d // FFN_WEIGHT_SLABS, d_ff), F32),
            pltpu.VMEM((FFN_STAGE_SLOTS, d // FFN_WEIGHT_SLABS, d_ff), F32),
            pltpu.VMEM((FFN_STAGE_SLOTS, d_ff // FFN_WEIGHT_SLABS, d), F32),
            pltpu.SemaphoreType.DMA((3, FFN_STAGE_SLOTS)),
            pltpu.VMEM((FFN_TILE, d), F32),
        ],
        compiler_params=pltpu.CompilerParams(dimension_semantics=("arbitrary",),
                                             vmem_limit_bytes=VMEM_LIMIT),
        name=name,
    )(x, norm_g.reshape(1, d), w_gate, w_up, w_down, *scale_arg, *cast_for_later)


def _head_mean_sq(x, low_half):
    x2 = x * x
    s_low = jnp.sum(jnp.where(low_half, x2, 0.0), axis=-1, keepdims=True)
    s_all = jnp.sum(x2, axis=-1, keepdims=True)
    return jnp.where(low_half, s_low, s_all - s_low) * (1.0 / HEAD_DIM)


def _split_kv_heads(x):
    swapped = pltpu.roll(x, HEAD_DIM, axis=1)
    low = lax.broadcasted_iota(jnp.int32, x.shape, 1) < HEAD_DIM
    zero = jnp.zeros_like(x)
    parts = (jnp.where(low, x, zero), jnp.where(low, zero, swapped),
             jnp.where(low, swapped, zero), jnp.where(low, zero, x))
    return tuple(p.astype(BF16) for p in parts)


def _mixer_kernel(h_ref, mixg_ref, w_in_ref, gbias_ref, poolw_ref, pscale_ref, wpo_ref,
                  lanes_ref, wao_ref, wout_ref, o_ref, rscale_ref,
                  u_scr, xp_scr, q32_scr, q_scr, ka_scr, kb_scr, ve_scr, vo_scr, attn_scr, mixed_scr,
                  gate_scr, merged_scr, sumsq_scr,
                  *, tiles_per_seq, d_model, pool_width, attn_width, kv_width):
    tm = h_ref.shape[0]
    i = pl.program_id(0)
    tile_in_seq = i % tiles_per_seq
    seq_start = tile_in_seq == 0
    lane128 = lax.broadcasted_iota(jnp.int32, (1, LANES), 1)
    low_half = lane128 < HEAD_DIM

    off_q = pool_width
    off_k = off_q + attn_width
    off_gp = off_k + 2 * kv_width
    off_ga = off_gp + d_model

    @pl.when(seq_start)
    def _():
        xp_scr[:, 0:POOL_HALO, :] = jnp.zeros((len(POOL_WINDOWS), POOL_HALO, POOL_GROUP), F32)
        zk = jnp.zeros((BLOCK, LANES), BF16)
        ones_e = jnp.broadcast_to(jnp.where(low_half, 1.0, 0.0).astype(BF16), (BLOCK + tm, LANES))
        ones_o = jnp.broadcast_to(jnp.where(low_half, 0.0, 1.0).astype(BF16), (BLOCK + tm, LANES))
        for kv in range(N_KV_HEADS):
            ka_scr[kv, 0:BLOCK, :] = zk
            kb_scr[kv, 0:BLOCK, :] = zk
            ve_scr[kv, 0:BLOCK, 0:LANES] = zk
            vo_scr[kv, 0:BLOCK, 0:LANES] = zk
            ve_scr[kv, :, LANES:2 * LANES] = ones_e
            vo_scr[kv, :, LANES:2 * LANES] = ones_o

    @pl.when(jnp.logical_not(seq_start))
    def _():
        xp_scr[:, 0:POOL_HALO, :] = xp_scr[:, tm:tm + POOL_HALO, :]
        for kv in range(N_KV_HEADS):
            ka_scr[kv, 0:BLOCK, :] = ka_scr[kv, tm:tm + BLOCK, :]
            kb_scr[kv, 0:BLOCK, :] = kb_scr[kv, tm:tm + BLOCK, :]
            ve_scr[kv, 0:BLOCK, 0:LANES] = ve_scr[kv, tm:tm + BLOCK, 0:LANES]
            vo_scr[kv, 0:BLOCK, 0:LANES] = vo_scr[kv, tm:tm + BLOCK, 0:LANES]

    u_scr[...] = _rmsnorm_rows(h_ref[...], mixg_ref[...]).astype(BF16)

    xp = _dot(u_scr[...], w_in_ref[:, 0:pool_width])
    for gi in range(len(POOL_WINDOWS)):
        xp_scr[gi, POOL_HALO:POOL_HALO + tm, :] = xp[:, gi * POOL_GROUP:(gi + 1) * POOL_GROUP]
    q32 = _dot(u_scr[...], w_in_ref[:, off_q:off_q + attn_width])
    for qc in range(attn_width // LANES):
        q32_scr[qc] = q32[:, qc * LANES:(qc + 1) * LANES]
    kv_proj = _dot_rows_split(u_scr[...], w_in_ref[:, off_k:off_k + 2 * kv_width])

    t = tile_in_seq * tm + lax.broadcasted_iota(jnp.int32, (tm, 1), 0)
    pooled = []
    for gi, w in enumerate(POOL_WINDOWS):
        ext = xp_scr[gi]
        wsum = ext
        shift = 1
        while shift < w:
            wsum = wsum + pltpu.roll(wsum, shift, axis=0)
            shift *= 2
        count = jnp.minimum(t + 1, w).astype(F32)
        pooled.append((wsum[POOL_HALO:] / count - ext[POOL_HALO:]).astype(BF16))
    for c in range(len(POOL_WINDOWS) // 2):
        cols = slice(c * 2 * POOL_GROUP, (c + 1) * 2 * POOL_GROUP)
        pair = jnp.concatenate(pooled[2 * c:2 * c + 2], axis=1)
        zeros = jnp.zeros((POOL_GROUP, POOL_GROUP), F32)
        w_pair = jnp.concatenate(
            [jnp.concatenate([poolw_ref[2 * c], zeros], axis=1),
             jnp.concatenate([zeros, poolw_ref[2 * c + 1]], axis=1)], axis=0).astype(BF16)
        mixed_scr[:, cols] = (_dot(pair, w_pair) * pscale_ref[:, cols]).astype(BF16)

    q_gain = lanes_ref[0:1, :] * (HEAD_DIM ** -0.5 * LOG2E)
    n_gate_chunks = d_model // MIX_COL_CHUNK
    q_chunks_per_gate_chunk = attn_width // LANES // n_gate_chunks
    for c in range(n_gate_chunks):
        cols = slice(c * MIX_COL_CHUNK, (c + 1) * MIX_COL_CHUNK)
        gp = jax.nn.sigmoid(
            _dot(u_scr[...], w_in_ref[:, off_gp + c * MIX_COL_CHUNK: off_gp + (c + 1) * MIX_COL_CHUNK])
            + gbias_ref[:, cols])
        gate_scr[c] = gp
        for qc in range(c * q_chunks_per_gate_chunk, (c + 1) * q_chunks_per_gate_chunk):
            q = q32_scr[qc]
            ms = _head_mean_sq(q, low_half)
            q_scr[qc] = (q * lax.rsqrt(ms + RMS_EPS) * q_gain).astype(BF16)

    k = kv_proj[:, 0:kv_width]
    k = k * lax.rsqrt(_head_mean_sq(k, low_half) + RMS_EPS) * lanes_ref[1:2, :]
    ka0, kb0, ka1, kb1 = _split_kv_heads(k)
    ka_scr[0, BLOCK:BLOCK + tm, :] = ka0
    kb_scr[0, BLOCK:BLOCK + tm, :] = kb0
    ka_scr[1, BLOCK:BLOCK + tm, :] = ka1
    kb_scr[1, BLOCK:BLOCK + tm, :] = kb1
    va0, vb0, va1, vb1 = _split_kv_heads(kv_proj[:, kv_width:2 * kv_width])
    ve_scr[0, BLOCK:BLOCK + tm, 0:LANES] = va0
    vo_scr[0, BLOCK:BLOCK + tm, 0:LANES] = vb0
    ve_scr[1, BLOCK:BLOCK + tm, 0:LANES] = va1
    vo_scr[1, BLOCK:BLOCK + tm, 0:LANES] = vb1

    key = lax.broadcasted_iota(jnp.int32, (2 * BLOCK, BLOCK), 0)
    qry = lax.broadcasted_iota(jnp.int32, (2 * BLOCK, BLOCK), 1)
    band = (key > qry) & (key <= qry + BLOCK)
    first_key = jnp.where(seq_start, BLOCK, 0)
    bias = jnp.where(band, 0.0, MASK_BIAS).astype(BF16)
    bias_first = jnp.where(band & (key >= first_key), 0.0, MASK_BIAS).astype(BF16)
    r4 = lax.broadcasted_iota(jnp.int32, (PAIRS_PER_KV * BLOCK, BLOCK), 0) % BLOCK
    c4 = lax.broadcasted_iota(jnp.int32, (PAIRS_PER_KV * BLOCK, BLOCK), 1)
    onehot = jnp.where(r4 == c4, 1.0, 0.0).astype(BF16)
    sinks2 = lanes_ref[2:2 + N_HEADS // 2, :] * LOG2E

    def score_unit(unit):
        b, kv = divmod(unit, N_KV_HEADS)
        rows = slice(b * BLOCK, (b + 1) * BLOCK)
        keys = slice(b * BLOCK, (b + 2) * BLOCK)
        key_bias = bias_first if b == 0 else bias
        base = kv * PAIRS_PER_KV
        qs = jnp.concatenate([q_scr[base + j, rows, :] for j in range(PAIRS_PER_KV)], axis=0)
        qs = jnp.concatenate([qs, onehot], axis=1)
        k_even = jnp.concatenate([ka_scr[kv, keys, :], key_bias], axis=1)
        k_odd = jnp.concatenate([kb_scr[kv, keys, :], key_bias], axis=1)
        return _dot_nt(qs, k_even), _dot_nt(qs, k_odd)

    def finish_unit(unit, scores):
        b, kv = divmod(unit, N_KV_HEADS)
        rows = slice(b * BLOCK, (b + 1) * BLOCK)
        keys = slice(b * BLOCK, (b + 2) * BLOCK)
        base = kv * PAIRS_PER_KV
        probs = ([], [])
        sink_terms = []
        for j in range(PAIRS_PER_KV):
            m = []
            for par in range(2):
                s = scores[par][j * BLOCK:(j + 1) * BLOCK]
                m.append(jnp.max(s, axis=-1, keepdims=True))
                probs[par].append(jnp.exp2(s - m[par]).astype(BF16))
            sink_terms.append(jnp.exp2(sinks2[base + j:base + j + 1, :]
                                       - jnp.where(low_half, m[0], m[1])))
        p = jnp.concatenate([jnp.concatenate(probs[0], axis=0),
                             jnp.concatenate(probs[1], axis=0)], axis=1)
        v_stack = jnp.concatenate([ve_scr[kv, keys, :], vo_scr[kv, keys, :]], axis=0)
        acc = _dot_rows_split(p, v_stack)
        den = acc[:, LANES:2 * LANES] + jnp.concatenate(sink_terms, axis=0)
        out = (acc[:, 0:LANES] / den).astype(BF16)
        for j in range(PAIRS_PER_KV):
            attn_scr[base + j, rows, :] = out[j * BLOCK:(j + 1) * BLOCK]

    half = tm // 2

    def pool_piece(r0, c):
        rows = slice(r0, r0 + half)
        cols = slice(c * MIX_COL_CHUNK, (c + 1) * MIX_COL_CHUNK)
        gate_scr[c, rows, :] = gate_scr[c, rows, :] * _dot(mixed_scr[rows, :], wpo_ref[:, cols])

    def merge_piece(r0, c):
        rows = slice(r0, r0 + half)
        cols = slice(c * MIX_COL_CHUNK, (c + 1) * MIX_COL_CHUNK)
        attn = jnp.concatenate([attn_scr[j, rows, :] for j in range(attn_width // LANES)], axis=1)
        ba = _dot(attn, wao_ref[:, cols])
        merged_scr[rows, cols] = (gate_scr[c, rows, :]
                                  + gate_scr[n_gate_chunks + c, rows, :] * ba).astype(BF16)

    def out_piece(r0, c):
        rows = slice(r0, r0 + half)
        cols = slice(c * MIX_COL_CHUNK, (c + 1) * MIX_COL_CHUNK)
        out = h_ref[rows, cols] + _dot(merged_scr[rows, :], wout_ref[:, cols])
        o_ref[rows, cols] = out
        sq = out * out
        part = sq[:, 0:LANES]
        for k in range(1, MIX_COL_CHUNK // LANES):
            part = part + sq[:, k * LANES:(k + 1) * LANES]
        if c > 0:
            part = part + sumsq_scr[rows, :]
        if c + 1 < n_gate_chunks:
            sumsq_scr[rows, :] = part
        else:
            mean_sq = jnp.sum(part, axis=-1, keepdims=True) * (1.0 / d_model)
            rscale_ref[rows, :] = lax.rsqrt(mean_sq + RMS_EPS)

    def output_pieces(r0):
        return ([functools.partial(merge_piece, r0, c) for c in range(n_gate_chunks)]
                + [functools.partial(out_piece, r0, c) for c in range(n_gate_chunks)])

    def attn_gate_piece(r0, c):
        rows = slice(r0, r0 + half)
        cols_a = slice(d_model + c * MIX_COL_CHUNK, d_model + (c + 1) * MIX_COL_CHUNK)
        gate_scr[n_gate_chunks + c, rows, :] = jax.nn.sigmoid(
            _dot(u_scr[rows, :], w_in_ref[:, off_ga + c * MIX_COL_CHUNK: off_ga + (c + 1) * MIX_COL_CHUNK])
            + gbias_ref[:, cols_a])

    n_units = (tm // BLOCK) * N_KV_HEADS
    dense = ([(functools.partial(attn_gate_piece, r0, c), functools.partial(pool_piece, r0, c))
              for r0 in (0, half) for c in range(n_gate_chunks)]
             + [(piece,) for piece in output_pieces(0)])
    assert len(dense) == n_units
    scores = score_unit(0)
    for unit in range(n_units):
        next_scores = score_unit(unit + 1) if unit + 1 < n_units else None
        for piece in dense[unit]:
            piece()
        finish_unit(unit, scores)
        scores = next_scores
    for piece in output_pieces(half):
        piece()


def _mixer(h, seq_len, mix_norm, w_in, pool_w, pool_scale, w_pool_out, q_norm, k_norm, sinks,
           w_attn_out, gate_bias, w_out):
    n, d = h.shape
    tm = MIX_TILE
    pool_width = w_pool_out.shape[0]
    attn_width = w_attn_out.shape[0]
    kv_width = N_KV_HEADS * HEAD_DIM
    in_width = w_in.shape[1]
    assert seq_len % tm == 0 and tm % BLOCK == 0
    assert pool_width == len(POOL_WINDOWS) * POOL_GROUP and attn_width == N_HEADS * HEAD_DIM
    assert in_width == pool_width + attn_width + 2 * kv_width + 2 * d and kv_width == LANES
    row_spec = pl.BlockSpec((tm, d), lambda i: (i, 0))
    kern = functools.partial(_mixer_kernel, tiles_per_seq=seq_len // tm, d_model=d,
                             pool_width=pool_width, attn_width=attn_width, kv_width=kv_width)
    lane_rows = jnp.concatenate(
        [jnp.tile(q_norm, LANES // HEAD_DIM)[None], jnp.tile(k_norm, LANES // HEAD_DIM)[None],
         jnp.repeat(sinks.reshape(N_HEADS // 2, 2), HEAD_DIM, axis=1)], axis=0)
    return pl.pallas_call(
        kern,
        grid=(n // tm,),
        in_specs=[row_spec, _resident((1, d)), _resident((d, in_width)), _resident((1, 2 * d)),
                  _resident(pool_w.shape), _resident((1, pool_width)), _resident((pool_width, d)),
                  _resident(lane_rows.shape),
                  _resident((attn_width, d)), _resident((d, d))],
        out_specs=[row_spec, pl.BlockSpec((tm, 1), lambda i: (i, 0))],
        out_shape=[jax.ShapeDtypeStruct((n, d), F32), jax.ShapeDtypeStruct((n, 1), F32)],
        scratch_shapes=[
            pltpu.VMEM((tm, d), BF16),
            pltpu.VMEM((len(POOL_WINDOWS), POOL_HALO + tm, POOL_GROUP), F32),
            pltpu.VMEM((attn_width // LANES, tm, LANES), F32),
            pltpu.VMEM((attn_width // LANES, tm, LANES), BF16),
            pltpu.VMEM((N_KV_HEADS, BLOCK + tm, LANES), BF16),
            pltpu.VMEM((N_KV_HEADS, BLOCK + tm, LANES), BF16),
            pltpu.VMEM((N_KV_HEADS, BLOCK + tm, 2 * LANES), BF16),
            pltpu.VMEM((N_KV_HEADS, BLOCK + tm, 2 * LANES), BF16),
            pltpu.VMEM((attn_width // LANES, tm, LANES), BF16),
            pltpu.VMEM((tm, pool_width), BF16),
            pltpu.VMEM((2 * d // MIX_COL_CHUNK, tm, MIX_COL_CHUNK), F32),
            pltpu.VMEM((tm, d), BF16),
            pltpu.VMEM((tm, LANES), F32),
        ],
        compiler_params=pltpu.CompilerParams(dimension_semantics=("arbitrary",),
                                             vmem_limit_bytes=VMEM_LIMIT),
        name="mixer",
    )(h, mix_norm.reshape(1, d), w_in, gate_bias.reshape(1, 2 * d),
      pool_w, pool_scale.reshape(1, pool_width), w_pool_out, lane_rows,
      w_attn_out, w_out)


def kernel(x, ffn1_norm, ffn1_w_gate, ffn1_w_up, ffn1_w_down, mix_norm, w_in, pool_w, pool_scale,
           w_pool_out, q_norm, k_norm, sinks, w_attn_out, gate_bias, w_out, ffn2_norm, ffn2_w_gate,
           ffn2_w_up, ffn2_w_down):
    b, s, d = x.shape
    h = x.reshape(b * s, d)
    h, w_in_b, w_pool_out_b, w_attn_out_b, w_out_b = _ffn(
        h, ffn1_norm, ffn1_w_gate, ffn1_w_up, ffn1_w_down, name="ffn1",
        cast_for_later=(w_in, w_pool_out, w_attn_out, w_out))
    h, h_row_scale = _mixer(h, s, mix_norm, w_in_b, pool_w, pool_scale, w_pool_out_b, q_norm, k_norm,
                            sinks, w_attn_out_b, gate_bias, w_out_b)
    (h,) = _ffn(h, ffn2_norm, ffn2_w_gate, ffn2_w_up, ffn2_w_down, name="ffn2",
                row_scale=h_row_scale)
    return h.reshape(b, s, d)
```

```python
import functools
import math

import jax
import jax.numpy as jnp
from jax import lax
from jax.experimental import pallas as pl
from jax.experimental.pallas import tpu as pltpu

F32 = jnp.float32
BF16 = jnp.bfloat16

RMS_EPS = 1e-6
HEAD_DIM = 64
N_HEADS = 16
N_KV_HEADS = 2
PAIRS_PER_KV = N_HEADS // N_KV_HEADS // 2
BLOCK = 128
POOL_WINDOWS = (2, 4, 8, 16)
POOL_GROUP = 128
POOL_HALO = 16
LANES = 128
BF16_SUBLANES = 16
V7X_VMEM_BYTES = 64 * 1024 * 1024
VMEM_LIMIT = V7X_VMEM_BYTES - 4 * 1024 * 1024
LOG2E = math.log2(math.e)
MASK_BIAS = -3.0e38

FFN_TILE = 1024
FFN_CHUNK = 256
FFN_DOWN_GROUP = 4
FFN_WEIGHT_SLABS = 16
FFN_STAGE_SLOTS = 3
MIX_TILE = 1024
MIX_COL_CHUNK = 256


def _dot(a, b):
    return jnp.dot(a, b, preferred_element_type=F32)


def _dot_rows_split(a, b):
    half = a.shape[0] // 2
    return jnp.concatenate([_dot(a[:half], b), _dot(a[half:], b)], axis=0)


def _dot_nt(a, b):
    return lax.dot_general(a, b, (((1,), (1,)), ((), ())), preferred_element_type=F32)


def _cast_slabs(f32_refs, bf16_refs):
    for src, dst in zip(f32_refs, bf16_refs, strict=True):
        dst[...] = src[...].astype(BF16)


def _ffn_kernel(x_ref, g_ref, wg_hbm, wu_hbm, wd_hbm, *rest, d_ff, chunk, n_cast, n_slabs):
    cast_in, o_ref, cast_out = rest[:n_cast], rest[n_cast], rest[n_cast + 1:2 * n_cast + 1]
    wg_v, wu_v, wd_v, stage_g, stage_u, stage_d, sems, acc_ref = rest[2 * n_cast + 1:]
    n_slots = stage_g.shape[0]
    streams = ((wg_hbm, stage_g, wg_v), (wu_hbm, stage_u, wu_v), (wd_hbm, stage_d, wd_v))

    def slab_copies(s):
        slot = s % n_slots
        return [pltpu.make_async_copy(hbm.at[pl.ds(s * stage.shape[1], stage.shape[1]), :],
                                      stage.at[slot], sems.at[k, slot])
                for k, (hbm, stage, _) in enumerate(streams)]

    @pl.when(pl.program_id(0) == 0)
    def _():
        for s in range(min(n_slots, n_slabs)):
            for cp in slab_copies(s):
                cp.start()
        for s in range(n_slabs):
            for cp in slab_copies(s):
                cp.wait()
            for _, stage, dst in streams:
                rows = stage.shape[1]
                dst[s * rows:(s + 1) * rows, :] = stage[s % n_slots].astype(BF16)
            if s + n_slots < n_slabs:
                for cp in slab_copies(s + n_slots):
                    cp.start()

    x = x_ref[...]
    xs = (x * g_ref[...]).astype(BF16)
    rstd = lax.rsqrt(jnp.mean(x * x, axis=-1, keepdims=True) + RMS_EPS)
    n_chunks = d_ff // chunk
    for first in range(0, n_chunks, FFN_DOWN_GROUP):
        group = range(first, min(first + FFN_DOWN_GROUP, n_chunks))
        acts = []
        for c in group:
            cols = slice(c * chunk, (c + 1) * chunk)
            g = _dot(xs, wg_v[:, cols]) * rstd
            u = _dot(xs, wu_v[:, cols]) * rstd
            acts.append((g * jax.nn.sigmoid(g) * u).astype(BF16))
        d = _dot(jnp.concatenate(acts, axis=1), wd_v[group[0] * chunk:(group[-1] + 1) * chunk, :])
        if first == 0:
            acc_ref[...] = d
        else:
            acc_ref[...] += d
    o_ref[...] = x + 0.5 * acc_ref[...]
    _cast_slabs(cast_in, cast_out)


def _resident(shape):
    return pl.BlockSpec(shape, lambda i: (0,) * len(shape), pipeline_mode=pl.Buffered(1))


def _slab_specs(weights, steps):
    in_specs, out_specs, out_shapes = [], [], []
    for w in weights:
        rows, cols = w.shape
        assert rows % (steps * BF16_SUBLANES) == 0
        in_specs.append(pl.BlockSpec((rows // steps, cols), lambda i: (i, 0)))
        out_specs.append(pl.BlockSpec((rows // steps, cols), lambda i: (i, 0)))
        out_shapes.append(jax.ShapeDtypeStruct(w.shape, BF16))
    return in_specs, out_specs, out_shapes


def _ffn(x, norm_g, w_gate, w_up, w_down, *, name, cast_for_later=()):
    n, d = x.shape
    d_ff = w_gate.shape[1]
    assert n % FFN_TILE == 0 and d_ff % FFN_CHUNK == 0
    assert d % (FFN_WEIGHT_SLABS * BF16_SUBLANES) == 0 and d_ff % (FFN_WEIGHT_SLABS * BF16_SUBLANES) == 0
    steps = n // FFN_TILE
    row_spec = pl.BlockSpec((FFN_TILE, d), lambda i: (i, 0))
    in_hbm = pl.BlockSpec(memory_space=pl.ANY)
    cast_in, cast_out, cast_shapes = _slab_specs(cast_for_later, steps)
    return pl.pallas_call(
        functools.partial(_ffn_kernel, d_ff=d_ff, chunk=FFN_CHUNK, n_cast=len(cast_for_later),
                          n_slabs=FFN_WEIGHT_SLABS),
        grid=(steps,),
        in_specs=[row_spec, _resident((1, d)), in_hbm, in_hbm, in_hbm] + cast_in,
        out_specs=[row_spec] + cast_out,
        out_shape=[jax.ShapeDtypeStruct((n, d), F32)] + cast_shapes,
        scratch_shapes=[
            pltpu.VMEM((d, d_ff), BF16),
            pltpu.VMEM((d, d_ff), BF16),
            pltpu.VMEM((d_ff, d), BF16),
            pltpu.VMEM((FFN_STAGE_SLOTS, d // FFN_WEIGHT_SLABS, d_ff), F32),
            pltpu.VMEM((FFN_STAGE_SLOTS, d // FFN_WEIGHT_SLABS, d_ff), F32),
            pltpu.VMEM((FFN_STAGE_SLOTS, d_ff // FFN_WEIGHT_SLABS, d), F32),
            pltpu.SemaphoreType.DMA((3, FFN_STAGE_SLOTS)),
            pltpu.VMEM((FFN_TILE, d), F32),
        ],
        compiler_params=pltpu.CompilerParams(dimension_semantics=("arbitrary",),
                                             vmem_limit_bytes=VMEM_LIMIT),
        name=name,
    )(x, norm_g.reshape(1, d), w_gate, w_up, w_down, *cast_for_later)


def _head_mean_sq(x, low_half):
    x2 = x * x
    s_low = jnp.sum(jnp.where(low_half, x2, 0.0), axis=-1, keepdims=True)
    s_all = jnp.sum(x2, axis=-1, keepdims=True)
    return jnp.where(low_half, s_low, s_all - s_low) * (1.0 / HEAD_DIM)


def _split_kv_heads(x):
    swapped = pltpu.roll(x, HEAD_DIM, axis=1)
    low = lax.broadcasted_iota(jnp.int32, x.shape, 1) < HEAD_DIM
    zero = jnp.zeros_like(x)
    parts = (jnp.where(low, x, zero), jnp.where(low, zero, swapped),
             jnp.where(low, swapped, zero), jnp.where(low, zero, x))
    return tuple(p.astype(BF16) for p in parts)


def _mixer_kernel(h_ref, mixg_ref, w_in_ref, gbias_ref, poolw_ref, pscale_ref, wpo_ref,
                  lanes_ref, wao_ref, wout_ref, o_ref,
                  u_scr, rstd_scr, xp_scr, q32_scr, q_scr, ka_scr, kb_scr, ve_scr, vo_scr, attn_scr,
                  mixed_scr, gate_scr, merged_scr,
                  *, tiles_per_seq, d_model, pool_width, attn_width, kv_width):
    tm = h_ref.shape[0]
    i = pl.program_id(0)
    tile_in_seq = i % tiles_per_seq
    seq_start = tile_in_seq == 0
    lane128 = lax.broadcasted_iota(jnp.int32, (1, LANES), 1)
    low_half = lane128 < HEAD_DIM

    off_q = pool_width
    off_k = off_q + attn_width
    off_gp = off_k + 2 * kv_width
    off_ga = off_gp + d_model

    @pl.when(seq_start)
    def _():
        xp_scr[:, 0:POOL_HALO, :] = jnp.zeros((len(POOL_WINDOWS), POOL_HALO, POOL_GROUP), F32)
        zk = jnp.zeros((BLOCK, LANES), BF16)
        ones_e = jnp.broadcast_to(jnp.where(low_half, 1.0, 0.0).astype(BF16), (BLOCK + tm, LANES))
        ones_o = jnp.broadcast_to(jnp.where(low_half, 0.0, 1.0).astype(BF16), (BLOCK + tm, LANES))
        for kv in range(N_KV_HEADS):
            ka_scr[kv, 0:BLOCK, :] = zk
            kb_scr[kv, 0:BLOCK, :] = zk
            ve_scr[kv, 0:BLOCK, 0:LANES] = zk
            vo_scr[kv, 0:BLOCK, 0:LANES] = zk
            ve_scr[kv, :, LANES:2 * LANES] = ones_e
            vo_scr[kv, :, LANES:2 * LANES] = ones_o

    @pl.when(jnp.logical_not(seq_start))
    def _():
        xp_scr[:, 0:POOL_HALO, :] = xp_scr[:, tm:tm + POOL_HALO, :]
        for kv in range(N_KV_HEADS):
            ka_scr[kv, 0:BLOCK, :] = ka_scr[kv, tm:tm + BLOCK, :]
            kb_scr[kv, 0:BLOCK, :] = kb_scr[kv, tm:tm + BLOCK, :]
            ve_scr[kv, 0:BLOCK, 0:LANES] = ve_scr[kv, tm:tm + BLOCK, 0:LANES]
            vo_scr[kv, 0:BLOCK, 0:LANES] = vo_scr[kv, tm:tm + BLOCK, 0:LANES]

    h_in = h_ref[...]
    u_scr[...] = (h_in * mixg_ref[...]).astype(BF16)
    rstd_scr[...] = lax.rsqrt(jnp.mean(h_in * h_in, axis=-1, keepdims=True) + RMS_EPS)

    xp = _dot(u_scr[...], w_in_ref[:, 0:pool_width]) * rstd_scr[...]
    for gi in range(len(POOL_WINDOWS)):
        xp_scr[gi, POOL_HALO:POOL_HALO + tm, :] = xp[:, gi * POOL_GROUP:(gi + 1) * POOL_GROUP]
    q32 = _dot(u_scr[...], w_in_ref[:, off_q:off_q + attn_width]) * rstd_scr[...]
    for qc in range(attn_width // LANES):
        q32_scr[qc] = q32[:, qc * LANES:(qc + 1) * LANES]
    kv_proj = _dot_rows_split(u_scr[...], w_in_ref[:, off_k:off_k + 2 * kv_width]) * rstd_scr[...]

    t = tile_in_seq * tm + lax.broadcasted_iota(jnp.int32, (tm, 1), 0)
    pooled = []
    for gi, w in enumerate(POOL_WINDOWS):
        ext = xp_scr[gi]
        wsum = ext
        shift = 1
        while shift < w:
            wsum = wsum + pltpu.roll(wsum, shift, axis=0)
            shift *= 2
        count = jnp.minimum(t + 1, w).astype(F32)
        pooled.append((wsum[POOL_HALO:] / count - ext[POOL_HALO:]).astype(BF16))
    for c in range(len(POOL_WINDOWS) // 2):
        cols = slice(c * 2 * POOL_GROUP, (c + 1) * 2 * POOL_GROUP)
        pair = jnp.concatenate(pooled[2 * c:2 * c + 2], axis=1)
        zeros = jnp.zeros((POOL_GROUP, POOL_GROUP), F32)
        w_pair = jnp.concatenate(
            [jnp.concatenate([poolw_ref[2 * c], zeros], axis=1),
             jnp.concatenate([zeros, poolw_ref[2 * c + 1]], axis=1)], axis=0).astype(BF16)
        mixed_scr[:, cols] = (_dot(pair, w_pair) * pscale_ref[:, cols]).astype(BF16)

    q_gain = lanes_ref[0:1, :] * (HEAD_DIM ** -0.5 * LOG2E)
    n_gate_chunks = d_model // MIX_COL_CHUNK
    q_chunks_per_gate_chunk = attn_width // LANES // n_gate_chunks
    for c in range(n_gate_chunks):
        cols = slice(c * MIX_COL_CHUNK, (c + 1) * MIX_COL_CHUNK)
        gp = jax.nn.sigmoid(
            _dot(u_scr[...], w_in_ref[:, off_gp + c * MIX_COL_CHUNK: off_gp + (c + 1) * MIX_COL_CHUNK])
            * rstd_scr[...] + gbias_ref[:, cols])
        gate_scr[c] = gp
        for qc in range(c * q_chunks_per_gate_chunk, (c + 1) * q_chunks_per_gate_chunk):
            q = q32_scr[qc]
            ms = _head_mean_sq(q, low_half)
            q_scr[qc] = (q * lax.rsqrt(ms + RMS_EPS) * q_gain).astype(BF16)

    k = kv_proj[:, 0:kv_width]
    k = k * lax.rsqrt(_head_mean_sq(k, low_half) + RMS_EPS) * lanes_ref[1:2, :]
    ka0, kb0, ka1, kb1 = _split_kv_heads(k)
    ka_scr[0, BLOCK:BLOCK + tm, :] = ka0
    kb_scr[0, BLOCK:BLOCK + tm, :] = kb0
    ka_scr[1, BLOCK:BLOCK + tm, :] = ka1
    kb_scr[1, BLOCK:BLOCK + tm, :] = kb1
    va0, vb0, va1, vb1 = _split_kv_heads(kv_proj[:, kv_width:2 * kv_width])
    ve_scr[0, BLOCK:BLOCK + tm, 0:LANES] = va0
    vo_scr[0, BLOCK:BLOCK + tm, 0:LANES] = vb0
    ve_scr[1, BLOCK:BLOCK + tm, 0:LANES] = va1
    vo_scr[1, BLOCK:BLOCK + tm, 0:LANES] = vb1

    key = lax.broadcasted_iota(jnp.int32, (2 * BLOCK, BLOCK), 0)
    qry = lax.broadcasted_iota(jnp.int32, (2 * BLOCK, BLOCK), 1)
    band = (key > qry) & (key <= qry + BLOCK)
    first_key = jnp.where(seq_start, BLOCK, 0)
    bias = jnp.where(band, 0.0, MASK_BIAS).astype(BF16)
    bias_first = jnp.where(band & (key >= first_key), 0.0, MASK_BIAS).astype(BF16)
    r4 = lax.broadcasted_iota(jnp.int32, (PAIRS_PER_KV * BLOCK, BLOCK), 0) % BLOCK
    c4 = lax.broadcasted_iota(jnp.int32, (PAIRS_PER_KV * BLOCK, BLOCK), 1)
    onehot = jnp.where(r4 == c4, 1.0, 0.0).astype(BF16)
    sinks2 = lanes_ref[2:2 + N_HEADS // 2, :] * LOG2E

    def score_unit(unit):
        b, kv = divmod(unit, N_KV_HEADS)
        rows = slice(b * BLOCK, (b + 1) * BLOCK)
        keys = slice(b * BLOCK, (b + 2) * BLOCK)
        key_bias = bias_first if b == 0 else bias
        base = kv * PAIRS_PER_KV
        qs = jnp.concatenate([q_scr[base + j, rows, :] for j in range(PAIRS_PER_KV)], axis=0)
        qs = jnp.concatenate([qs, onehot], axis=1)
        k_even = jnp.concatenate([ka_scr[kv, keys, :], key_bias], axis=1)
        k_odd = jnp.concatenate([kb_scr[kv, keys, :], key_bias], axis=1)
        return _dot_nt(qs, k_even), _dot_nt(qs, k_odd)

    def finish_unit(unit, scores):
        b, kv = divmod(unit, N_KV_HEADS)
        rows = slice(b * BLOCK, (b + 1) * BLOCK)
        keys = slice(b * BLOCK, (b + 2) * BLOCK)
        base = kv * PAIRS_PER_KV
        probs = ([], [])
        sink_terms = []
        for j in range(PAIRS_PER_KV):
            m = []
            for par in range(2):
                s = scores[par][j * BLOCK:(j + 1) * BLOCK]
                m.append(jnp.max(s, axis=-1, keepdims=True))
                probs[par].append(jnp.exp2(s - m[par]).astype(BF16))
            sink_terms.append(jnp.exp2(sinks2[base + j:base + j + 1, :]
                                       - jnp.where(low_half, m[0], m[1])))
        p = jnp.concatenate([jnp.concatenate(probs[0], axis=0),
                             jnp.concatenate(probs[1], axis=0)], axis=1)
        v_stack = jnp.concatenate([ve_scr[kv, keys, :], vo_scr[kv, keys, :]], axis=0)
        acc = _dot_rows_split(p, v_stack)
        den = acc[:, LANES:2 * LANES] + jnp.concatenate(sink_terms, axis=0)
        out = (acc[:, 0:LANES] / den).astype(BF16)
        for j in range(PAIRS_PER_KV):
            attn_scr[base + j, rows, :] = out[j * BLOCK:(j + 1) * BLOCK]

    half = tm // 2

    def pool_piece(r0, c):
        rows = slice(r0, r0 + half)
        cols = slice(c * MIX_COL_CHUNK, (c + 1) * MIX_COL_CHUNK)
        gate_scr[c, rows, :] = gate_scr[c, rows, :] * _dot(mixed_scr[rows, :], wpo_ref[:, cols])

    def merge_piece(r0, c):
        rows = slice(r0, r0 + half)
        cols = slice(c * MIX_COL_CHUNK, (c + 1) * MIX_COL_CHUNK)
        attn = jnp.concatenate([attn_scr[j, rows, :] for j in range(attn_width // LANES)], axis=1)
        ba = _dot(attn, wao_ref[:, cols])
        merged_scr[rows, cols] = (gate_scr[c, rows, :]
                                  + gate_scr[n_gate_chunks + c, rows, :] * ba).astype(BF16)

    def out_piece(r0, c):
        rows = slice(r0, r0 + half)
        cols = slice(c * MIX_COL_CHUNK, (c + 1) * MIX_COL_CHUNK)
        o_ref[rows, cols] = h_ref[rows, cols] + _dot(merged_scr[rows, :], wout_ref[:, cols])

    def output_pieces(r0):
        return ([functools.partial(merge_piece, r0, c) for c in range(n_gate_chunks)]
                + [functools.partial(out_piece, r0, c) for c in range(n_gate_chunks)])

    def attn_gate_piece(r0, c):
        rows = slice(r0, r0 + half)
        cols_a = slice(d_model + c * MIX_COL_CHUNK, d_model + (c + 1) * MIX_COL_CHUNK)
        gate_scr[n_gate_chunks + c, rows, :] = jax.nn.sigmoid(
            _dot(u_scr[rows, :], w_in_ref[:, off_ga + c * MIX_COL_CHUNK: off_ga + (c + 1) * MIX_COL_CHUNK])
            * rstd_scr[rows, :] + gbias_ref[:, cols_a])

    n_units = (tm // BLOCK) * N_KV_HEADS
    dense = ([(functools.partial(attn_gate_piece, r0, c), functools.partial(pool_piece, r0, c))
              for r0 in (0, half) for c in range(n_gate_chunks)]
             + [(piece,) for piece in output_pieces(0)])
    assert len(dense) == n_units
    scores = score_unit(0)
    for unit in range(n_units):
        next_scores = score_unit(unit + 1) if unit + 1 < n_units else None
        for piece in dense[unit]:
            piece()
        finish_unit(unit, scores)
        scores = next_scores
    for piece in output_pieces(half):
        piece()


def _mixer(h, seq_len, mix_norm, w_in, pool_w, pool_scale, w_pool_out, q_norm, k_norm, sinks,
           w_attn_out, gate_bias, w_out):
    n, d = h.shape
    tm = MIX_TILE
    pool_width = w_pool_out.shape[0]
    attn_width = w_attn_out.shape[0]
    kv_width = N_KV_HEADS * HEAD_DIM
    in_width = w_in.shape[1]
    assert seq_len % tm == 0 and tm % BLOCK == 0
    assert pool_width == len(POOL_WINDOWS) * POOL_GROUP and attn_width == N_HEADS * HEAD_DIM
    assert in_width == pool_width + attn_width + 2 * kv_width + 2 * d and kv_width == LANES
    row_spec = pl.BlockSpec((tm, d), lambda i: (i, 0))
    kern = functools.partial(_mixer_kernel, tiles_per_seq=seq_len // tm, d_model=d,
                             pool_width=pool_width, attn_width=attn_width, kv_width=kv_width)
    lane_rows = jnp.concatenate(
        [jnp.tile(q_norm, LANES // HEAD_DIM)[None], jnp.tile(k_norm, LANES // HEAD_DIM)[None],
         jnp.repeat(sinks.reshape(N_HEADS // 2, 2), HEAD_DIM, axis=1)], axis=0)
    return pl.pallas_call(
        kern,
        grid=(n // tm,),
        in_specs=[row_spec, _resident((1, d)), _resident((d, in_width)), _resident((1, 2 * d)),
                  _resident(pool_w.shape), _resident((1, pool_width)), _resident((pool_width, d)),
                  _resident(lane_rows.shape),
                  _resident((attn_width, d)), _resident((d, d))],
        out_specs=row_spec,
        out_shape=jax.ShapeDtypeStruct((n, d), F32),
        scratch_shapes=[
            pltpu.VMEM((tm, d), BF16),
            pltpu.VMEM((tm, 1), F32),
            pltpu.VMEM((len(POOL_WINDOWS), POOL_HALO + tm, POOL_GROUP), F32),
            pltpu.VMEM((attn_width // LANES, tm, LANES), F32),
            pltpu.VMEM((attn_width // LANES, tm, LANES), BF16),
            pltpu.VMEM((N_KV_HEADS, BLOCK + tm, LANES), BF16),
            pltpu.VMEM((N_KV_HEADS, BLOCK + tm, LANES), BF16),
            pltpu.VMEM((N_KV_HEADS, BLOCK + tm, 2 * LANES), BF16),
            pltpu.VMEM((N_KV_HEADS, BLOCK + tm, 2 * LANES), BF16),
            pltpu.VMEM((attn_width // LANES, tm, LANES), BF16),
            pltpu.VMEM((tm, pool_width), BF16),
            pltpu.VMEM((2 * d // MIX_COL_CHUNK, tm, MIX_COL_CHUNK), F32),
            pltpu.VMEM((tm, d), BF16),
        ],
        compiler_params=pltpu.CompilerParams(dimension_semantics=("arbitrary",),
                                             vmem_limit_bytes=VMEM_LIMIT),
        name="mixer",
    )(h, mix_norm.reshape(1, d), w_in, gate_bias.reshape(1, 2 * d),
      pool_w, pool_scale.reshape(1, pool_width), w_pool_out, lane_rows,
      w_attn_out, w_out)


def kernel(x, ffn1_norm, ffn1_w_gate, ffn1_w_up, ffn1_w_down, mix_norm, w_in, pool_w, pool_scale,
           w_pool_out, q_norm, k_norm, sinks, w_attn_out, gate_bias, w_out, ffn2_norm, ffn2_w_gate,
           ffn2_w_up, ffn2_w_down):
    b, s, d = x.shape
    h = x.reshape(b * s, d)
    h, w_in_b, w_pool_out_b, w_attn_out_b, w_out_b = _ffn(
        h, ffn1_norm, ffn1_w_gate, ffn1_w_up, ffn1_w_down, name="ffn1",
        cast_for_later=(w_in, w_pool_out, w_attn_out, w_out))
    h = _mixer(h, s, mix_norm, w_in_b, pool_w, pool_scale, w_pool_out_b, q_norm, k_norm, sinks,
               w_attn_out_b, gate_bias, w_out_b)
    (h,) = _ffn(h, ffn2_norm, ffn2_w_gate, ffn2_w_up, ffn2_w_down, name="ffn2")
    return h.reshape(b, s, d)
```

```python
import functools
import math

import jax
import jax.numpy as jnp
from jax import lax
from jax.experimental import pallas as pl
from jax.experimental.pallas import tpu as pltpu

F32 = jnp.float32
BF16 = jnp.bfloat16

RMS_EPS = 1e-6
HEAD_DIM = 64
N_HEADS = 16
N_KV_HEADS = 2
PAIRS_PER_KV = N_HEADS // N_KV_HEADS // 2
BLOCK = 128
POOL_WINDOWS = (2, 4, 8, 16)
POOL_GROUP = 128
POOL_HALO = 16
LANES = 128
BF16_SUBLANES = 16
V7X_VMEM_BYTES = 64 * 1024 * 1024
VMEM_LIMIT = V7X_VMEM_BYTES - 4 * 1024 * 1024
LOG2E = math.log2(math.e)
MASK_BIAS = -3.0e38

FFN_TILE = 1024
FFN_CHUNK = 256
FFN_DOWN_GROUP = 4
FFN_WEIGHT_SLABS = 16
FFN_STAGE_SLOTS = 3
MIX_TILE = 1024
MIX_COL_CHUNK = 256


def _dot(a, b):
    return jnp.dot(a, b, preferred_element_type=F32)


def _dot_rows_split(a, b):
    half = a.shape[0] // 2
    return jnp.concatenate([_dot(a[:half], b), _dot(a[half:], b)], axis=0)


def _dot_nt(a, b):
    return lax.dot_general(a, b, (((1,), (1,)), ((), ())), preferred_element_type=F32)


def _cast_slabs(f32_refs, bf16_refs):
    for src, dst in zip(f32_refs, bf16_refs, strict=True):
        dst[...] = src[...].astype(BF16)


def _ffn_kernel(x_ref, g_ref, wg_hbm, wu_hbm, wd_hbm, *rest, d_ff, chunk, n_cast, n_slabs):
    cast_in, o_ref, cast_out = rest[:n_cast], rest[n_cast], rest[n_cast + 1:2 * n_cast + 1]
    wg_v, wu_v, wd_v, stage_g, stage_u, stage_d, sems, acc_ref = rest[2 * n_cast + 1:]
    n_slots = stage_g.shape[0]
    streams = ((wg_hbm, stage_g, wg_v), (wu_hbm, stage_u, wu_v), (wd_hbm, stage_d, wd_v))

    def slab_copies(s):
        slot = s % n_slots
        return [pltpu.make_async_copy(hbm.at[pl.ds(s * stage.shape[1], stage.shape[1]), :],
                                      stage.at[slot], sems.at[k, slot])
                for k, (hbm, stage, _) in enumerate(streams)]

    @pl.when(pl.program_id(0) == 0)
    def _():
        for s in range(min(n_slots, n_slabs)):
            for cp in slab_copies(s):
                cp.start()
        for s in range(n_slabs):
            for cp in slab_copies(s):
                cp.wait()
            for _, stage, dst in streams:
                rows = stage.shape[1]
                dst[s * rows:(s + 1) * rows, :] = stage[s % n_slots].astype(BF16)
            if s + n_slots < n_slabs:
                for cp in slab_copies(s + n_slots):
                    cp.start()

    x = x_ref[...]
    xs = (x * g_ref[...]).astype(BF16)
    rstd = lax.rsqrt(jnp.mean(x * x, axis=-1, keepdims=True) + RMS_EPS)
    n_chunks = d_ff // chunk
    for first in range(0, n_chunks, FFN_DOWN_GROUP):
        group = range(first, min(first + FFN_DOWN_GROUP, n_chunks))
        acts = []
        for c in group:
            cols = slice(c * chunk, (c + 1) * chunk)
            g = _dot(xs, wg_v[:, cols]) * rstd
            u = _dot(xs, wu_v[:, cols]) * rstd
            acts.append((g * jax.nn.sigmoid(g) * u).astype(BF16))
        d = _dot(jnp.concatenate(acts, axis=1), wd_v[group[0] * chunk:(group[-1] + 1) * chunk, :])
        if first == 0:
            acc_ref[...] = d
        else:
            acc_ref[...] += d
    o_ref[...] = x + 0.5 * acc_ref[...]
    _cast_slabs(cast_in, cast_out)


def _resident(shape):
    return pl.BlockSpec(shape, lambda i: (0,) * len(shape), pipeline_mode=pl.Buffered(1))


def _slab_specs(weights, steps):
    in_specs, out_specs, out_shapes = [], [], []
    for w in weights:
        rows, cols = w.shape
        assert rows % (steps * BF16_SUBLANES) == 0
        in_specs.append(pl.BlockSpec((rows // steps, cols), lambda i: (i, 0)))
        out_specs.append(pl.BlockSpec((rows // steps, cols), lambda i: (i, 0)))
        out_shapes.append(jax.ShapeDtypeStruct(w.shape, BF16))
    return in_specs, out_specs, out_shapes


def _ffn(x, norm_g, w_gate, w_up, w_down, *, name, cast_for_later=()):
    n, d = x.shape
    d_ff = w_gate.shape[1]
    assert n % FFN_TILE == 0 and d_ff % FFN_CHUNK == 0
    assert d % (FFN_WEIGHT_SLABS * BF16_SUBLANES) == 0 and d_ff % (FFN_WEIGHT_SLABS * BF16_SUBLANES) == 0
    steps = n // FFN_TILE
    row_spec = pl.BlockSpec((FFN_TILE, d), lambda i: (i, 0))
    in_hbm = pl.BlockSpec(memory_space=pl.ANY)
    cast_in, cast_out, cast_shapes = _slab_specs(cast_for_later, steps)
    return pl.pallas_call(
        functools.partial(_ffn_kernel, d_ff=d_ff, chunk=FFN_CHUNK, n_cast=len(cast_for_later),
                          n_slabs=FFN_WEIGHT_SLABS),
        grid=(steps,),
        in_specs=[row_spec, _resident((1, d)), in_hbm, in_hbm, in_hbm] + cast_in,
        out_specs=[row_spec] + cast_out,
        out_shape=[jax.ShapeDtypeStruct((n, d), F32)] + cast_shapes,
        scratch_shapes=[
            pltpu.VMEM((d, d_ff), BF16),
            pltpu.VMEM((d, d_ff), BF16),
            pltpu.VMEM((d_ff, d), BF16),
            pltpu.VMEM((FFN_STAGE_SLOTS, d // FFN_WEIGHT_SLABS, d_ff), F32),
            pltpu.VMEM((FFN_STAGE_SLOTS, d // FFN_WEIGHT_SLABS, d_ff), F32),
            pltpu.VMEM((FFN_STAGE_SLOTS, d_ff // FFN_WEIGHT_SLABS, d), F32),
            pltpu.SemaphoreType.DMA((3, FFN_STAGE_SLOTS)),
            pltpu.VMEM((FFN_TILE, d), F32),
        ],
        compiler_params=pltpu.CompilerParams(dimension_semantics=("arbitrary",),
                                             vmem_limit_bytes=VMEM_LIMIT),
        name=name,
    )(x, norm_g.reshape(1, d), w_gate, w_up, w_down, *cast_for_later)


def _head_mean_sq(x, low_half):
    x2 = x * x
    s_low = jnp.sum(jnp.where(low_half, x2, 0.0), axis=-1, keepdims=True)
    s_all = jnp.sum(x2, axis=-1, keepdims=True)
    return jnp.where(low_half, s_low, s_all - s_low) * (1.0 / HEAD_DIM)


def _split_kv_heads(x):
    swapped = pltpu.roll(x, HEAD_DIM, axis=1)
    low = lax.broadcasted_iota(jnp.int32, x.shape, 1) < HEAD_DIM
    zero = jnp.zeros_like(x)
    parts = (jnp.where(low, x, zero), jnp.where(low, zero, swapped),
             jnp.where(low, swapped, zero), jnp.where(low, zero, x))
    return tuple(p.astype(BF16) for p in parts)


def _mixer_kernel(h_ref, mixg_ref, w_in_ref, gbias_ref, poolw_ref, pscale_ref, wpo_ref,
                  lanes_ref, wao_ref, wout_ref, o_ref,
                  u_scr, xp_scr, q32_scr, q_scr, ka_scr, kb_scr, ve_scr, vo_scr, attn_scr,
                  mixed_scr, gate_scr, merged_scr,
                  *, tiles_per_seq, d_model, pool_width, attn_width, kv_width):
    tm = h_ref.shape[0]
    i = pl.program_id(0)
    tile_in_seq = i % tiles_per_seq
    seq_start = tile_in_seq == 0
    lane128 = lax.broadcasted_iota(jnp.int32, (1, LANES), 1)
    low_half = lane128 < HEAD_DIM

    off_q = pool_width
    off_k = off_q + attn_width
    off_gp = off_k + 2 * kv_width
    off_ga = off_gp + d_model

    @pl.when(seq_start)
    def _():
        xp_scr[:, 0:POOL_HALO, :] = jnp.zeros((len(POOL_WINDOWS), POOL_HALO, POOL_GROUP), F32)
        zk = jnp.zeros((BLOCK, LANES), BF16)
        ones_e = jnp.broadcast_to(jnp.where(low_half, 1.0, 0.0).astype(BF16), (BLOCK + tm, LANES))
        ones_o = jnp.broadcast_to(jnp.where(low_half, 0.0, 1.0).astype(BF16), (BLOCK + tm, LANES))
        for kv in range(N_KV_HEADS):
            ka_scr[kv, 0:BLOCK, :] = zk
            kb_scr[kv, 0:BLOCK, :] = zk
            ve_scr[kv, 0:BLOCK, 0:LANES] = zk
            vo_scr[kv, 0:BLOCK, 0:LANES] = zk
            ve_scr[kv, :, LANES:2 * LANES] = ones_e
            vo_scr[kv, :, LANES:2 * LANES] = ones_o

    @pl.when(jnp.logical_not(seq_start))
    def _():
        xp_scr[:, 0:POOL_HALO, :] = xp_scr[:, tm:tm + POOL_HALO, :]
        for kv in range(N_KV_HEADS):
            ka_scr[kv, 0:BLOCK, :] = ka_scr[kv, tm:tm + BLOCK, :]
            kb_scr[kv, 0:BLOCK, :] = kb_scr[kv, tm:tm + BLOCK, :]
            ve_scr[kv, 0:BLOCK, 0:LANES] = ve_scr[kv, tm:tm + BLOCK, 0:LANES]
            vo_scr[kv, 0:BLOCK, 0:LANES] = vo_scr[kv, tm:tm + BLOCK, 0:LANES]

    h_in = h_ref[...]
    rstd = lax.rsqrt(jnp.mean(h_in * h_in, axis=-1, keepdims=True) + RMS_EPS)
    u_scr[...] = (h_in * rstd * mixg_ref[...]).astype(BF16)

    xp = _dot(u_scr[...], w_in_ref[:, 0:pool_width])
    for gi in range(len(POOL_WINDOWS)):
        xp_scr[gi, POOL_HALO:POOL_HALO + tm, :] = xp[:, gi * POOL_GROUP:(gi + 1) * POOL_GROUP]
    q32 = _dot(u_scr[...], w_in_ref[:, off_q:off_q + attn_width])
    for qc in range(attn_width // LANES):
        q32_scr[qc] = q32[:, qc * LANES:(qc + 1) * LANES]
    kv_proj = _dot_rows_split(u_scr[...], w_in_ref[:, off_k:off_k + 2 * kv_width])

    t = tile_in_seq * tm + lax.broadcasted_iota(jnp.int32, (tm, 1), 0)
    pooled = []
    for gi, w in enumerate(POOL_WINDOWS):
        ext = xp_scr[gi]
        wsum = ext
        shift = 1
        while shift < w:
            wsum = wsum + pltpu.roll(wsum, shift, axis=0)
            shift *= 2
        count = jnp.minimum(t + 1, w).astype(F32)
        pooled.append((wsum[POOL_HALO:] / count - ext[POOL_HALO:]).astype(BF16))
    for c in range(len(POOL_WINDOWS) // 2):
        cols = slice(c * 2 * POOL_GROUP, (c + 1) * 2 * POOL_GROUP)
        pair = jnp.concatenate(pooled[2 * c:2 * c + 2], axis=1)
        zeros = jnp.zeros((POOL_GROUP, POOL_GROUP), F32)
        w_pair = jnp.concatenate(
            [jnp.concatenate([poolw_ref[2 * c], zeros], axis=1),
             jnp.concatenate([zeros, poolw_ref[2 * c + 1]], axis=1)], axis=0).astype(BF16)
        mixed_scr[:, cols] = (_dot(pair, w_pair) * pscale_ref[:, cols]).astype(BF16)

    q_gain = lanes_ref[0:1, :] * (HEAD_DIM ** -0.5 * LOG2E)
    n_gate_chunks = d_model // MIX_COL_CHUNK
    q_chunks_per_gate_chunk = attn_width // LANES // n_gate_chunks
    for c in range(n_gate_chunks):
        cols = slice(c * MIX_COL_CHUNK, (c + 1) * MIX_COL_CHUNK)
        gp = jax.nn.sigmoid(
            _dot(u_scr[...], w_in_ref[:, off_gp + c * MIX_COL_CHUNK: off_gp + (c + 1) * MIX_COL_CHUNK])
            + gbias_ref[:, cols])
        gate_scr[c] = gp
        for qc in range(c * q_chunks_per_gate_chunk, (c + 1) * q_chunks_per_gate_chunk):
            q = q32_scr[qc]
            ms = _head_mean_sq(q, low_half)
            q_scr[qc] = (q * lax.rsqrt(ms + RMS_EPS) * q_gain).astype(BF16)

    k = kv_proj[:, 0:kv_width]
    k = k * lax.rsqrt(_head_mean_sq(k, low_half) + RMS_EPS) * lanes_ref[1:2, :]
    ka0, kb0, ka1, kb1 = _split_kv_heads(k)
    ka_scr[0, BLOCK:BLOCK + tm, :] = ka0
    kb_scr[0, BLOCK:BLOCK + tm, :] = kb0
    ka_scr[1, BLOCK:BLOCK + tm, :] = ka1
    kb_scr[1, BLOCK:BLOCK + tm, :] = kb1
    va0, vb0, va1, vb1 = _split_kv_heads(kv_proj[:, kv_width:2 * kv_width])
    ve_scr[0, BLOCK:BLOCK + tm, 0:LANES] = va0
    vo_scr[0, BLOCK:BLOCK + tm, 0:LANES] = vb0
    ve_scr[1, BLOCK:BLOCK + tm, 0:LANES] = va1
    vo_scr[1, BLOCK:BLOCK + tm, 0:LANES] = vb1

    key = lax.broadcasted_iota(jnp.int32, (2 * BLOCK, BLOCK), 0)
    qry = lax.broadcasted_iota(jnp.int32, (2 * BLOCK, BLOCK), 1)
    band = (key > qry) & (key <= qry + BLOCK)
    first_key = jnp.where(seq_start, BLOCK, 0)
    bias = jnp.where(band, 0.0, MASK_BIAS).astype(BF16)
    bias_first = jnp.where(band & (key >= first_key), 0.0, MASK_BIAS).astype(BF16)
    r4 = lax.broadcasted_iota(jnp.int32, (PAIRS_PER_KV * BLOCK, BLOCK), 0) % BLOCK
    c4 = lax.broadcasted_iota(jnp.int32, (PAIRS_PER_KV * BLOCK, BLOCK), 1)
    onehot = jnp.where(r4 == c4, 1.0, 0.0).astype(BF16)
    sinks2 = lanes_ref[2:2 + N_HEADS // 2, :] * LOG2E

    def score_unit(unit):
        b, kv = divmod(unit, N_KV_HEADS)
        rows = slice(b * BLOCK, (b + 1) * BLOCK)
        keys = slice(b * BLOCK, (b + 2) * BLOCK)
        key_bias = bias_first if b == 0 else bias
        base = kv * PAIRS_PER_KV
        qs = jnp.concatenate([q_scr[base + j, rows, :] for j in range(PAIRS_PER_KV)], axis=0)
        qs = jnp.concatenate([qs, onehot], axis=1)
        k_even = jnp.concatenate([ka_scr[kv, keys, :], key_bias], axis=1)
        k_odd = jnp.concatenate([kb_scr[kv, keys, :], key_bias], axis=1)
        return _dot_nt(qs, k_even), _dot_nt(qs, k_odd)

    def finish_unit(unit, scores):
        b, kv = divmod(unit, N_KV_HEADS)
        rows = slice(b * BLOCK, (b + 1) * BLOCK)
        keys = slice(b * BLOCK, (b + 2) * BLOCK)
        base = kv * PAIRS_PER_KV
        probs = ([], [])
        sink_terms = []
        for j in range(PAIRS_PER_KV):
            m = []
            for par in range(2):
                s = scores[par][j * BLOCK:(j + 1) * BLOCK]
                m.append(jnp.max(s, axis=-1, keepdims=True))
                probs[par].append(jnp.exp2(s - m[par]).astype(BF16))
            sink_terms.append(jnp.exp2(sinks2[base + j:base + j + 1, :]
                                       - jnp.where(low_half, m[0], m[1])))
        p = jnp.concatenate([jnp.concatenate(probs[0], axis=0),
                             jnp.concatenate(probs[1], axis=0)], axis=1)
        v_stack = jnp.concatenate([ve_scr[kv, keys, :], vo_scr[kv, keys, :]], axis=0)
        acc = _dot_rows_split(p, v_stack)
        den = acc[:, LANES:2 * LANES] + jnp.concatenate(sink_terms, axis=0)
        out = (acc[:, 0:LANES] / den).astype(BF16)
        for j in range(PAIRS_PER_KV):
            attn_scr[base + j, rows, :] = out[j * BLOCK:(j + 1) * BLOCK]

    half = tm // 2

    def pool_piece(r0, c):
        rows = slice(r0, r0 + half)
        cols = slice(c * MIX_COL_CHUNK, (c + 1) * MIX_COL_CHUNK)
        gate_scr[c, rows, :] = gate_scr[c, rows, :] * _dot(mixed_scr[rows, :], wpo_ref[:, cols])

    def merge_piece(r0, c):
        rows = slice(r0, r0 + half)
        cols = slice(c * MIX_COL_CHUNK, (c + 1) * MIX_COL_CHUNK)
        attn = jnp.concatenate([attn_scr[j, rows, :] for j in range(attn_width // LANES)], axis=1)
        ba = _dot(attn, wao_ref[:, cols])
        merged_scr[rows, cols] = (gate_scr[c, rows, :]
                                  + gate_scr[n_gate_chunks + c, rows, :] * ba).astype(BF16)

    def out_piece(r0, c):
        rows = slice(r0, r0 + half)
        cols = slice(c * MIX_COL_CHUNK, (c + 1) * MIX_COL_CHUNK)
        o_ref[rows, cols] = h_ref[rows, cols] + _dot(merged_scr[rows, :], wout_ref[:, cols])

    def output_pieces(r0):
        return ([functools.partial(merge_piece, r0, c) for c in range(n_gate_chunks)]
                + [functools.partial(out_piece, r0, c) for c in range(n_gate_chunks)])

    def attn_gate_piece(r0, c):
        rows = slice(r0, r0 + half)
        cols_a = slice(d_model + c * MIX_COL_CHUNK, d_model + (c + 1) * MIX_COL_CHUNK)
        gate_scr[n_gate_chunks + c, rows, :] = jax.nn.sigmoid(
            _dot(u_scr[rows, :], w_in_ref[:, off_ga + c * MIX_COL_CHUNK: off_ga + (c + 1) * MIX_COL_CHUNK])
            + gbias_ref[:, cols_a])

    n_units = (tm // BLOCK) * N_KV_HEADS
    dense = ([(functools.partial(attn_gate_piece, r0, c), functools.partial(pool_piece, r0, c))
              for r0 in (0, half) for c in range(n_gate_chunks)]
             + [(piece,) for piece in output_pieces(0)])
    assert len(dense) == n_units
    scores = score_unit(0)
    for unit in range(n_units):
        next_scores = score_unit(unit + 1) if unit + 1 < n_units else None
        for piece in dense[unit]:
            piece()
        finish_unit(unit, scores)
        scores = next_scores
    for piece in output_pieces(half):
        piece()


def _mixer(h, seq_len, mix_norm, w_in, pool_w, pool_scale, w_pool_out, q_norm, k_norm, sinks,
           w_attn_out, gate_bias, w_out):
    n, d = h.shape
    tm = MIX_TILE
    pool_width = w_pool_out.shape[0]
    attn_width = w_attn_out.shape[0]
    kv_width = N_KV_HEADS * HEAD_DIM
    in_width = w_in.shape[1]
    assert seq_len % tm == 0 and tm % BLOCK == 0
    assert pool_width == len(POOL_WINDOWS) * POOL_GROUP and attn_width == N_HEADS * HEAD_DIM
    assert in_width == pool_width + attn_width + 2 * kv_width + 2 * d and kv_width == LANES
    row_spec = pl.BlockSpec((tm, d), lambda i: (i, 0))
    kern = functools.partial(_mixer_kernel, tiles_per_seq=seq_len // tm, d_model=d,
                             pool_width=pool_width, attn_width=attn_width, kv_width=kv_width)
    lane_rows = jnp.concatenate(
        [jnp.tile(q_norm, LANES // HEAD_DIM)[None], jnp.tile(k_norm, LANES // HEAD_DIM)[None],
         jnp.repeat(sinks.reshape(N_HEADS // 2, 2), HEAD_DIM, axis=1)], axis=0)
    return pl.pallas_call(
        kern,
        grid=(n // tm,),
        in_specs=[row_spec, _resident((1, d)), _resident((d, in_width)), _resident((1, 2 * d)),
                  _resident(pool_w.shape), _resident((1, pool_width)), _resident((pool_width, d)),
                  _resident(lane_rows.shape),
                  _resident((attn_width, d)), _resident((d, d))],
        out_specs=row_spec,
        out_shape=jax.ShapeDtypeStruct((n, d), F32),
        scratch_shapes=[
            pltpu.VMEM((tm, d), BF16),
            pltpu.VMEM((len(POOL_WINDOWS), POOL_HALO + tm, POOL_GROUP), F32),
            pltpu.VMEM((attn_width // LANES, tm, LANES), F32),
            pltpu.VMEM((attn_width // LANES, tm, LANES), BF16),
            pltpu.VMEM((N_KV_HEADS, BLOCK + tm, LANES), BF16),
            pltpu.VMEM((N_KV_HEADS, BLOCK + tm, LANES), BF16),
            pltpu.VMEM((N_KV_HEADS, BLOCK + tm, 2 * LANES), BF16),
            pltpu.VMEM((N_KV_HEADS, BLOCK + tm, 2 * LANES), BF16),
            pltpu.VMEM((attn_width // LANES, tm, LANES), BF16),
            pltpu.VMEM((tm, pool_width), BF16),
            pltpu.VMEM((2 * d // MIX_COL_CHUNK, tm, MIX_COL_CHUNK), F32),
            pltpu.VMEM((tm, d), BF16),
        ],
        compiler_params=pltpu.CompilerParams(dimension_semantics=("arbitrary",),
                                             vmem_limit_bytes=VMEM_LIMIT),
        name="mixer",
    )(h, mix_norm.reshape(1, d), w_in, gate_bias.reshape(1, 2 * d),
      pool_w, pool_scale.reshape(1, pool_width), w_pool_out, lane_rows,
      w_attn_out, w_out)


def kernel(x, ffn1_norm, ffn1_w_gate, ffn1_w_up, ffn1_w_down, mix_norm, w_in, pool_w, pool_scale,
           w_pool_out, q_norm, k_norm, sinks, w_attn_out, gate_bias, w_out, ffn2_norm, ffn2_w_gate,
           ffn2_w_up, ffn2_w_down):
    b, s, d = x.shape
    h = x.reshape(b * s, d)
    h, w_in_b, w_pool_out_b, w_attn_out_b, w_out_b = _ffn(
        h, ffn1_norm, ffn1_w_gate, ffn1_w_up, ffn1_w_down, name="ffn1",
        cast_for_later=(w_in, w_pool_out, w_attn_out, w_out))
    h = _mixer(h, s, mix_norm, w_in_b, pool_w, pool_scale, w_pool_out_b, q_norm, k_norm, sinks,
               w_attn_out_b, gate_bias, w_out_b)
    (h,) = _ffn(h, ffn2_norm, ffn2_w_gate, ffn2_w_up, ffn2_w_down, name="ffn2")
    return h.reshape(b, s, d)
```

```python
import functools
import math

import jax
import jax.numpy as jnp
from jax import lax
from jax.experimental import pallas as pl
from jax.experimental.pallas import tpu as pltpu

F32 = jnp.float32
BF16 = jnp.bfloat16

RMS_EPS = 1e-6
HEAD_DIM = 64
N_HEADS = 16
N_KV_HEADS = 2
PAIRS_PER_KV = N_HEADS // N_KV_HEADS // 2
BLOCK = 128
POOL_WINDOWS = (2, 4, 8, 16)
POOL_GROUP = 128
POOL_HALO = 16
LANES = 128
BF16_SUBLANES = 16
V7X_VMEM_BYTES = 64 * 1024 * 1024
VMEM_LIMIT = V7X_VMEM_BYTES - 4 * 1024 * 1024
LOG2E = math.log2(math.e)
MASK_BIAS = -3.0e38

FFN_TILE = 1024
FFN_CHUNK = 256
FFN_DOWN_GROUP = 4
FFN_WEIGHT_SLABS = 16
FFN_STAGE_SLOTS = 3
MIX_TILE = 1024
MIX_COL_CHUNK = 256


def _dot(a, b):
    return jnp.dot(a, b, preferred_element_type=F32)


def _dot_rows_split(a, b):
    half = a.shape[0] // 2
    return jnp.concatenate([_dot(a[:half], b), _dot(a[half:], b)], axis=0)


def _dot_nt(a, b):
    return lax.dot_general(a, b, (((1,), (1,)), ((), ())), preferred_element_type=F32)


def _cast_slabs(f32_refs, bf16_refs):
    for src, dst in zip(f32_refs, bf16_refs, strict=True):
        dst[...] = src[...].astype(BF16)


def _ffn_kernel(x_ref, g_ref, wg_hbm, wu_hbm, wd_hbm, *rest, d_ff, chunk, n_cast, n_slabs):
    cast_in, o_ref, cast_out = rest[:n_cast], rest[n_cast], rest[n_cast + 1:2 * n_cast + 1]
    wg_v, wu_v, wd_v, stage_g, stage_u, stage_d, sems, acc_ref = rest[2 * n_cast + 1:]
    n_slots = stage_g.shape[0]
    streams = ((wg_hbm, stage_g, wg_v), (wu_hbm, stage_u, wu_v), (wd_hbm, stage_d, wd_v))

    def slab_copies(s):
        slot = s % n_slots
        return [pltpu.make_async_copy(hbm.at[pl.ds(s * stage.shape[1], stage.shape[1]), :],
                                      stage.at[slot], sems.at[k, slot])
                for k, (hbm, stage, _) in enumerate(streams)]

    @pl.when(pl.program_id(0) == 0)
    def _():
        for s in range(min(n_slots, n_slabs)):
            for cp in slab_copies(s):
                cp.start()
        for s in range(n_slabs):
            for cp in slab_copies(s):
                cp.wait()
            for _, stage, dst in streams:
                rows = stage.shape[1]
                dst[s * rows:(s + 1) * rows, :] = stage[s % n_slots].astype(BF16)
            if s + n_slots < n_slabs:
                for cp in slab_copies(s + n_slots):
                    cp.start()

    x = x_ref[...]
    xs = (x * g_ref[...]).astype(BF16)
    rstd = lax.rsqrt(jnp.mean(x * x, axis=-1, keepdims=True) + RMS_EPS)
    n_chunks = d_ff // chunk
    for first in range(0, n_chunks, FFN_DOWN_GROUP):
        group = range(first, min(first + FFN_DOWN_GROUP, n_chunks))
        acts = []
        for c in group:
            cols = slice(c * chunk, (c + 1) * chunk)
            g = _dot(xs, wg_v[:, cols]) * rstd
            u = _dot(xs, wu_v[:, cols]) * rstd
            acts.append((g * jax.nn.sigmoid(g) * u).astype(BF16))
        d = _dot(jnp.concatenate(acts, axis=1), wd_v[group[0] * chunk:(group[-1] + 1) * chunk, :])
        if first == 0:
            acc_ref[...] = d
        else:
            acc_ref[...] += d
    o_ref[...] = x + 0.5 * acc_ref[...]
    _cast_slabs(cast_in, cast_out)


def _resident(shape):
    return pl.BlockSpec(shape, lambda i: (0,) * len(shape), pipeline_mode=pl.Buffered(1))


def _slab_specs(weights, steps):
    in_specs, out_specs, out_shapes = [], [], []
    for w in weights:
        rows, cols = w.shape
        assert rows % (steps * BF16_SUBLANES) == 0
        in_specs.append(pl.BlockSpec((rows // steps, cols), lambda i: (i, 0)))
        out_specs.append(pl.BlockSpec((rows // steps, cols), lambda i: (i, 0)))
        out_shapes.append(jax.ShapeDtypeStruct(w.shape, BF16))
    return in_specs, out_specs, out_shapes


def _ffn(x, norm_g, w_gate, w_up, w_down, *, name, cast_for_later=()):
    n, d = x.shape
    d_ff = w_gate.shape[1]
    assert n % FFN_TILE == 0 and d_ff % FFN_CHUNK == 0
    assert d % (FFN_WEIGHT_SLABS * BF16_SUBLANES) == 0 and d_ff % (FFN_WEIGHT_SLABS * BF16_SUBLANES) == 0
    steps = n // FFN_TILE
    row_spec = pl.BlockSpec((FFN_TILE, d), lambda i: (i, 0))
    in_hbm = pl.BlockSpec(memory_space=pl.ANY)
    cast_in, cast_out, cast_shapes = _slab_specs(cast_for_later, steps)
    return pl.pallas_call(
        functools.partial(_ffn_kernel, d_ff=d_ff, chunk=FFN_CHUNK, n_cast=len(cast_for_later),
                          n_slabs=FFN_WEIGHT_SLABS),
        grid=(steps,),
        in_specs=[row_spec, _resident((1, d)), in_hbm, in_hbm, in_hbm] + cast_in,
        out_specs=[row_spec] + cast_out,
        out_shape=[jax.ShapeDtypeStruct((n, d), F32)] + cast_shapes,
        scratch_shapes=[
            pltpu.VMEM((d, d_ff), BF16),
            pltpu.VMEM((d, d_ff), BF16),
            pltpu.VMEM((d_ff, d), BF16),
            pltpu.VMEM((FFN_STAGE_SLOTS, d // FFN_WEIGHT_SLABS, d_ff), F32),
            pltpu.VMEM((FFN_STAGE_SLOTS, d // FFN_WEIGHT_SLABS, d_ff), F32),
            pltpu.VMEM((FFN_STAGE_SLOTS, d_ff // FFN_WEIGHT_SLABS, d), F32),
            pltpu.SemaphoreType.DMA((3, FFN_STAGE_SLOTS)),
            pltpu.VMEM((FFN_TILE, d), F32),
        ],
        compiler_params=pltpu.CompilerParams(dimension_semantics=("arbitrary",),
                                             vmem_limit_bytes=VMEM_LIMIT),
        name=name,
    )(x, norm_g.reshape(1, d), w_gate, w_up, w_down, *cast_for_later)


def _head_mean_sq(x, low_half):
    x2 = x * x
    s_low = jnp.sum(jnp.where(low_half, x2, 0.0), axis=-1, keepdims=True)
    s_all = jnp.sum(x2, axis=-1, keepdims=True)
    return jnp.where(low_half, s_low, s_all - s_low) * (1.0 / HEAD_DIM)


def _split_kv_heads(x):
    swapped = pltpu.roll(x, HEAD_DIM, axis=1)
    low = lax.broadcasted_iota(jnp.int32, x.shape, 1) < HEAD_DIM
    zero = jnp.zeros_like(x)
    parts = (jnp.where(low, x, zero), jnp.where(low, zero, swapped),
             jnp.where(low, swapped, zero), jnp.where(low, zero, x))
    return tuple(p.astype(BF16) for p in parts)


def _mixer_kernel(h_ref, mixg_ref, w_in_ref, gbias_ref, poolw_ref, pscale_ref, wpo_ref,
                  lanes_ref, wao_ref, wout_ref, o_ref,
                  u_scr, xp_scr, q32_scr, q_scr, ka_scr, kb_scr, ve_scr, vo_scr, attn_scr,
                  mixed_scr, gate_scr, merged_scr,
                  *, tiles_per_seq, d_model, pool_width, attn_width, kv_width):
    tm = h_ref.shape[0]
    i = pl.program_id(0)
    tile_in_seq = i % tiles_per_seq
    seq_start = tile_in_seq == 0
    lane128 = lax.broadcasted_iota(jnp.int32, (1, LANES), 1)
    low_half = lane128 < HEAD_DIM

    off_q = pool_width
    off_k = off_q + attn_width
    off_gp = off_k + 2 * kv_width
    off_ga = off_gp + d_model

    @pl.when(seq_start)
    def _():
        xp_scr[:, 0:POOL_HALO, :] = jnp.zeros((len(POOL_WINDOWS), POOL_HALO, POOL_GROUP), F32)
        zk = jnp.zeros((BLOCK, LANES), BF16)
        ones_e = jnp.broadcast_to(jnp.where(low_half, 1.0, 0.0).astype(BF16), (BLOCK + tm, LANES))
        ones_o = jnp.broadcast_to(jnp.where(low_half, 0.0, 1.0).astype(BF16), (BLOCK + tm, LANES))
        for kv in range(N_KV_HEADS):
            ka_scr[kv, 0:BLOCK, :] = zk
            kb_scr[kv, 0:BLOCK, :] = zk
            ve_scr[kv, 0:BLOCK, 0:LANES] = zk
            vo_scr[kv, 0:BLOCK, 0:LANES] = zk
            ve_scr[kv, :, LANES:2 * LANES] = ones_e
            vo_scr[kv, :, LANES:2 * LANES] = ones_o

    @pl.when(jnp.logical_not(seq_start))
    def _():
        xp_scr[:, 0:POOL_HALO, :] = xp_scr[:, tm:tm + POOL_HALO, :]
        for kv in range(N_KV_HEADS):
            ka_scr[kv, 0:BLOCK, :] = ka_scr[kv, tm:tm + BLOCK, :]
            kb_scr[kv, 0:BLOCK, :] = kb_scr[kv, tm:tm + BLOCK, :]
            ve_scr[kv, 0:BLOCK, 0:LANES] = ve_scr[kv, tm:tm + BLOCK, 0:LANES]
            vo_scr[kv, 0:BLOCK, 0:LANES] = vo_scr[kv, tm:tm + BLOCK, 0:LANES]

    h_in = h_ref[...]
    u_unscaled = (h_in * mixg_ref[...]).astype(BF16)
    rstd = lax.rsqrt(jnp.mean(h_in * h_in, axis=-1, keepdims=True) + RMS_EPS)
    u_scr[...] = (h_in * rstd * mixg_ref[...]).astype(BF16)

    xp = _dot(u_unscaled, w_in_ref[:, 0:pool_width]) * rstd
    for gi in range(len(POOL_WINDOWS)):
        xp_scr[gi, POOL_HALO:POOL_HALO + tm, :] = xp[:, gi * POOL_GROUP:(gi + 1) * POOL_GROUP]
    q32 = _dot(u_scr[...], w_in_ref[:, off_q:off_q + attn_width])
    for qc in range(attn_width // LANES):
        q32_scr[qc] = q32[:, qc * LANES:(qc + 1) * LANES]
    kv_proj = _dot_rows_split(u_scr[...], w_in_ref[:, off_k:off_k + 2 * kv_width])

    t = tile_in_seq * tm + lax.broadcasted_iota(jnp.int32, (tm, 1), 0)
    pooled = []
    for gi, w in enumerate(POOL_WINDOWS):
        ext = xp_scr[gi]
        wsum = ext
        shift = 1
        while shift < w:
            wsum = wsum + pltpu.roll(wsum, shift, axis=0)
            shift *= 2
        count = jnp.minimum(t + 1, w).astype(F32)
        pooled.append((wsum[POOL_HALO:] / count - ext[POOL_HALO:]).astype(BF16))
    for c in range(len(POOL_WINDOWS) // 2):
        cols = slice(c * 2 * POOL_GROUP, (c + 1) * 2 * POOL_GROUP)
        pair = jnp.concatenate(pooled[2 * c:2 * c + 2], axis=1)
        zeros = jnp.zeros((POOL_GROUP, POOL_GROUP), F32)
        w_pair = jnp.concatenate(
            [jnp.concatenate([poolw_ref[2 * c], zeros], axis=1),
             jnp.concatenate([zeros, poolw_ref[2 * c + 1]], axis=1)], axis=0).astype(BF16)
        mixed_scr[:, cols] = (_dot(pair, w_pair) * pscale_ref[:, cols]).astype(BF16)

    q_gain = lanes_ref[0:1, :] * (HEAD_DIM ** -0.5 * LOG2E)
    n_gate_chunks = d_model // MIX_COL_CHUNK
    q_chunks_per_gate_chunk = attn_width // LANES // n_gate_chunks
    for c in range(n_gate_chunks):
        cols = slice(c * MIX_COL_CHUNK, (c + 1) * MIX_COL_CHUNK)
        gp = jax.nn.sigmoid(
            _dot(u_scr[...], w_in_ref[:, off_gp + c * MIX_COL_CHUNK: off_gp + (c + 1) * MIX_COL_CHUNK])
            + gbias_ref[:, cols])
        gate_scr[c] = gp
        for qc in range(c * q_chunks_per_gate_chunk, (c + 1) * q_chunks_per_gate_chunk):
            q = q32_scr[qc]
            ms = _head_mean_sq(q, low_half)
            q_scr[qc] = (q * lax.rsqrt(ms + RMS_EPS) * q_gain).astype(BF16)

    k = kv_proj[:, 0:kv_width]
    k = k * lax.rsqrt(_head_mean_sq(k, low_half) + RMS_EPS) * lanes_ref[1:2, :]
    ka0, kb0, ka1, kb1 = _split_kv_heads(k)
    ka_scr[0, BLOCK:BLOCK + tm, :] = ka0
    kb_scr[0, BLOCK:BLOCK + tm, :] = kb0
    ka_scr[1, BLOCK:BLOCK + tm, :] = ka1
    kb_scr[1, BLOCK:BLOCK + tm, :] = kb1
    va0, vb0, va1, vb1 = _split_kv_heads(kv_proj[:, kv_width:2 * kv_width])
    ve_scr[0, BLOCK:BLOCK + tm, 0:LANES] = va0
    vo_scr[0, BLOCK:BLOCK + tm, 0:LANES] = vb0
    ve_scr[1, BLOCK:BLOCK + tm, 0:LANES] = va1
    vo_scr[1, BLOCK:BLOCK + tm, 0:LANES] = vb1

    key = lax.broadcasted_iota(jnp.int32, (2 * BLOCK, BLOCK), 0)
    qry = lax.broadcasted_iota(jnp.int32, (2 * BLOCK, BLOCK), 1)
    band = (key > qry) & (key <= qry + BLOCK)
    first_key = jnp.where(seq_start, BLOCK, 0)
    bias = jnp.where(band, 0.0, MASK_BIAS).astype(BF16)
    bias_first = jnp.where(band & (key >= first_key), 0.0, MASK_BIAS).astype(BF16)
    r4 = lax.broadcasted_iota(jnp.int32, (PAIRS_PER_KV * BLOCK, BLOCK), 0) % BLOCK
    c4 = lax.broadcasted_iota(jnp.int32, (PAIRS_PER_KV * BLOCK, BLOCK), 1)
    onehot = jnp.where(r4 == c4, 1.0, 0.0).astype(BF16)
    sinks2 = lanes_ref[2:2 + N_HEADS // 2, :] * LOG2E

    def score_unit(unit):
        b, kv = divmod(unit, N_KV_HEADS)
        rows = slice(b * BLOCK, (b + 1) * BLOCK)
        keys = slice(b * BLOCK, (b + 2) * BLOCK)
        key_bias = bias_first if b == 0 else bias
        base = kv * PAIRS_PER_KV
        qs = jnp.concatenate([q_scr[base + j, rows, :] for j in range(PAIRS_PER_KV)], axis=0)
        qs = jnp.concatenate([qs, onehot], axis=1)
        k_even = jnp.concatenate([ka_scr[kv, keys, :], key_bias], axis=1)
        k_odd = jnp.concatenate([kb_scr[kv, keys, :], key_bias], axis=1)
        return _dot_nt(qs, k_even), _dot_nt(qs, k_odd)

    def finish_unit(unit, scores):
        b, kv = divmod(unit, N_KV_HEADS)
        rows = slice(b * BLOCK, (b + 1) * BLOCK)
        keys = slice(b * BLOCK, (b + 2) * BLOCK)
        base = kv * PAIRS_PER_KV
        probs = ([], [])
        sink_terms = []
        for j in range(PAIRS_PER_KV):
            m = []
            for par in range(2):
                s = scores[par][j * BLOCK:(j + 1) * BLOCK]
                m.append(jnp.max(s, axis=-1, keepdims=True))
                probs[par].append(jnp.exp2(s - m[par]).astype(BF16))
            sink_terms.append(jnp.exp2(sinks2[base + j:base + j + 1, :]
                                       - jnp.where(low_half, m[0], m[1])))
        p = jnp.concatenate([jnp.concatenate(probs[0], axis=0),
                             jnp.concatenate(probs[1], axis=0)], axis=1)
        v_stack = jnp.concatenate([ve_scr[kv, keys, :], vo_scr[kv, keys, :]], axis=0)
        acc = _dot_rows_split(p, v_stack)
        den = acc[:, LANES:2 * LANES] + jnp.concatenate(sink_terms, axis=0)
        out = (acc[:, 0:LANES] / den).astype(BF16)
        for j in range(PAIRS_PER_KV):
            attn_scr[base + j, rows, :] = out[j * BLOCK:(j + 1) * BLOCK]

    half = tm // 2

    def pool_piece(r0, c):
        rows = slice(r0, r0 + half)
        cols = slice(c * MIX_COL_CHUNK, (c + 1) * MIX_COL_CHUNK)
        gate_scr[c, rows, :] = gate_scr[c, rows, :] * _dot(mixed_scr[rows, :], wpo_ref[:, cols])

    def merge_piece(r0, c):
        rows = slice(r0, r0 + half)
        cols = slice(c * MIX_COL_CHUNK, (c + 1) * MIX_COL_CHUNK)
        attn = jnp.concatenate([attn_scr[j, rows, :] for j in range(attn_width // LANES)], axis=1)
        ba = _dot(attn, wao_ref[:, cols])
        merged_scr[rows, cols] = (gate_scr[c, rows, :]
                                  + gate_scr[n_gate_chunks + c, rows, :] * ba).astype(BF16)

    def out_piece(r0, c):
        rows = slice(r0, r0 + half)
        cols = slice(c * MIX_COL_CHUNK, (c + 1) * MIX_COL_CHUNK)
        o_ref[rows, cols] = h_ref[rows, cols] + _dot(merged_scr[rows, :], wout_ref[:, cols])

    def output_pieces(r0):
        return ([functools.partial(merge_piece, r0, c) for c in range(n_gate_chunks)]
                + [functools.partial(out_piece, r0, c) for c in range(n_gate_chunks)])

    def attn_gate_piece(r0, c):
        rows = slice(r0, r0 + half)
        cols_a = slice(d_model + c * MIX_COL_CHUNK, d_model + (c + 1) * MIX_COL_CHUNK)
        gate_scr[n_gate_chunks + c, rows, :] = jax.nn.sigmoid(
            _dot(u_scr[rows, :], w_in_ref[:, off_ga + c * MIX_COL_CHUNK: off_ga + (c + 1) * MIX_COL_CHUNK])
            + gbias_ref[:, cols_a])

    n_units = (tm // BLOCK) * N_KV_HEADS
    dense = ([(functools.partial(attn_gate_piece, r0, c), functools.partial(pool_piece, r0, c))
              for r0 in (0, half) for c in range(n_gate_chunks)]
             + [(piece,) for piece in output_pieces(0)])
    assert len(dense) == n_units
    scores = score_unit(0)
    for unit in range(n_units):
        next_scores = score_unit(unit + 1) if unit + 1 < n_units else None
        for piece in dense[unit]:
            piece()
        finish_unit(unit, scores)
        scores = next_scores
    for piece in output_pieces(half):
        piece()


def _mixer(h, seq_len, mix_norm, w_in, pool_w, pool_scale, w_pool_out, q_norm, k_norm, sinks,
           w_attn_out, gate_bias, w_out):
    n, d = h.shape
    tm = MIX_TILE
    pool_width = w_pool_out.shape[0]
    attn_width = w_attn_out.shape[0]
    kv_width = N_KV_HEADS * HEAD_DIM
    in_width = w_in.shape[1]
    assert seq_len % tm == 0 and tm % BLOCK == 0
    assert pool_width == len(POOL_WINDOWS) * POOL_GROUP and attn_width == N_HEADS * HEAD_DIM
    assert in_width == pool_width + attn_width + 2 * kv_width + 2 * d and kv_width == LANES
    row_spec = pl.BlockSpec((tm, d), lambda i: (i, 0))
    kern = functools.partial(_mixer_kernel, tiles_per_seq=seq_len // tm, d_model=d,
                             pool_width=pool_width, attn_width=attn_width, kv_width=kv_width)
    lane_rows = jnp.concatenate(
        [jnp.tile(q_norm, LANES // HEAD_DIM)[None], jnp.tile(k_norm, LANES // HEAD_DIM)[None],
         jnp.repeat(sinks.reshape(N_HEADS // 2, 2), HEAD_DIM, axis=1)], axis=0)
    return pl.pallas_call(
        kern,
        grid=(n // tm,),
        in_specs=[row_spec, _resident((1, d)), _resident((d, in_width)), _resident((1, 2 * d)),
                  _resident(pool_w.shape), _resident((1, pool_width)), _resident((pool_width, d)),
                  _resident(lane_rows.shape),
                  _resident((attn_width, d)), _resident((d, d))],
        out_specs=row_spec,
        out_shape=jax.ShapeDtypeStruct((n, d), F32),
        scratch_shapes=[
            pltpu.VMEM((tm, d), BF16),
            pltpu.VMEM((len(POOL_WINDOWS), POOL_HALO + tm, POOL_GROUP), F32),
            pltpu.VMEM((attn_width // LANES, tm, LANES), F32),
            pltpu.VMEM((attn_width // LANES, tm, LANES), BF16),
            pltpu.VMEM((N_KV_HEADS, BLOCK + tm, LANES), BF16),
            pltpu.VMEM((N_KV_HEADS, BLOCK + tm, LANES), BF16),
            pltpu.VMEM((N_KV_HEADS, BLOCK + tm, 2 * LANES), BF16),
            pltpu.VMEM((N_KV_HEADS, BLOCK + tm, 2 * LANES), BF16),
            pltpu.VMEM((attn_width // LANES, tm, LANES), BF16),
            pltpu.VMEM((tm, pool_width), BF16),
            pltpu.VMEM((2 * d // MIX_COL_CHUNK, tm, MIX_COL_CHUNK), F32),
            pltpu.VMEM((tm, d), BF16),
        ],
        compiler_params=pltpu.CompilerParams(dimension_semantics=("arbitrary",),
                                             vmem_limit_bytes=VMEM_LIMIT),
        name="mixer",
    )(h, mix_norm.reshape(1, d), w_in, gate_bias.reshape(1, 2 * d),
      pool_w, pool_scale.reshape(1, pool_width), w_pool_out, lane_rows,
      w_attn_out, w_out)


def kernel(x, ffn1_norm, ffn1_w_gate, ffn1_w_up, ffn1_w_down, mix_norm, w_in, pool_w, pool_scale,
           w_pool_out, q_norm, k_norm, sinks, w_attn_out, gate_bias, w_out, ffn2_norm, ffn2_w_gate,
           ffn2_w_up, ffn2_w_down):
    b, s, d = x.shape
    h = x.reshape(b * s, d)
    h, w_in_b, w_pool_out_b, w_attn_out_b, w_out_b = _ffn(
        h, ffn1_norm, ffn1_w_gate, ffn1_w_up, ffn1_w_down, name="ffn1",
        cast_for_later=(w_in, w_pool_out, w_attn_out, w_out))
    h = _mixer(h, s, mix_norm, w_in_b, pool_w, pool_scale, w_pool_out_b, q_norm, k_norm, sinks,
               w_attn_out_b, gate_bias, w_out_b)
    (h,) = _ffn(h, ffn2_norm, ffn2_w_gate, ffn2_w_up, ffn2_w_down, name="ffn2")
    return h.reshape(b, s, d)
```

```python
import functools
import math

import jax
import jax.numpy as jnp
from jax import lax
from jax.experimental import pallas as pl
from jax.experimental.pallas import tpu as pltpu

F32 = jnp.float32
BF16 = jnp.bfloat16

RMS_EPS = 1e-6
HEAD_DIM = 64
N_HEADS = 16
N_KV_HEADS = 2
PAIRS_PER_KV = N_HEADS // N_KV_HEADS // 2
BLOCK = 128
POOL_WINDOWS = (2, 4, 8, 16)
POOL_GROUP = 128
POOL_HALO = 16
LANES = 128
BF16_SUBLANES = 16
V7X_VMEM_BYTES = 64 * 1024 * 1024
VMEM_LIMIT = V7X_VMEM_BYTES - 4 * 1024 * 1024
LOG2E = math.log2(math.e)
MASK_BIAS = -3.0e38

FFN_TILE = 1024
FFN_CHUNK = 256
FFN_WEIGHT_SLABS = 16
FFN_STAGE_SLOTS = 3
MIX_TILE = 1024
MIX_COL_CHUNK = 256


def _dot(a, b):
    return jnp.dot(a, b, preferred_element_type=F32)


def _dot_rows_split(a, b):
    half = a.shape[0] // 2
    return jnp.concatenate([_dot(a[:half], b), _dot(a[half:], b)], axis=0)


def _dot_nt(a, b):
    return lax.dot_general(a, b, (((1,), (1,)), ((), ())), preferred_element_type=F32)


def _cast_slabs(f32_refs, bf16_refs):
    for src, dst in zip(f32_refs, bf16_refs, strict=True):
        dst[...] = src[...].astype(BF16)


def _ffn_kernel(x_ref, g_ref, wg_hbm, wu_hbm, wd_hbm, *rest, d_ff, chunk, n_cast, n_slabs):
    cast_in, o_ref, cast_out = rest[:n_cast], rest[n_cast], rest[n_cast + 1:2 * n_cast + 1]
    wg_v, wu_v, wd_v, stage_g, stage_u, stage_d, sems = rest[2 * n_cast + 1:]
    n_slots = stage_g.shape[0]
    streams = ((wg_hbm, stage_g, wg_v), (wu_hbm, stage_u, wu_v), (wd_hbm, stage_d, wd_v))

    def slab_copies(s):
        slot = s % n_slots
        return [pltpu.make_async_copy(hbm.at[pl.ds(s * stage.shape[1], stage.shape[1]), :],
                                      stage.at[slot], sems.at[k, slot])
                for k, (hbm, stage, _) in enumerate(streams)]

    @pl.when(pl.program_id(0) == 0)
    def _():
        for s in range(min(n_slots, n_slabs)):
            for cp in slab_copies(s):
                cp.start()
        for s in range(n_slabs):
            for cp in slab_copies(s):
                cp.wait()
            for _, stage, dst in streams:
                rows = stage.shape[1]
                dst[s * rows:(s + 1) * rows, :] = stage[s % n_slots].astype(BF16)
            if s + n_slots < n_slabs:
                for cp in slab_copies(s + n_slots):
                    cp.start()

    x = x_ref[...]
    xs = (x * g_ref[...]).astype(BF16)
    rstd = lax.rsqrt(jnp.mean(x * x, axis=-1, keepdims=True) + RMS_EPS)
    acts = []
    for c in range(d_ff // chunk):
        cols = slice(c * chunk, (c + 1) * chunk)
        g = _dot(xs, wg_v[:, cols]) * rstd
        u = _dot(xs, wu_v[:, cols]) * rstd
        acts.append((g * jax.nn.sigmoid(g) * u).astype(BF16))
    o_ref[...] = x + 0.5 * _dot(jnp.concatenate(acts, axis=1), wd_v[...])
    _cast_slabs(cast_in, cast_out)


def _resident(shape):
    return pl.BlockSpec(shape, lambda i: (0,) * len(shape), pipeline_mode=pl.Buffered(1))


def _slab_specs(weights, steps):
    in_specs, out_specs, out_shapes = [], [], []
    for w in weights:
        rows, cols = w.shape
        assert rows % (steps * BF16_SUBLANES) == 0
        in_specs.append(pl.BlockSpec((rows // steps, cols), lambda i: (i, 0)))
        out_specs.append(pl.BlockSpec((rows // steps, cols), lambda i: (i, 0)))
        out_shapes.append(jax.ShapeDtypeStruct(w.shape, BF16))
    return in_specs, out_specs, out_shapes


def _ffn(x, norm_g, w_gate, w_up, w_down, *, name, cast_for_later=()):
    n, d = x.shape
    d_ff = w_gate.shape[1]
    assert n % FFN_TILE == 0 and d_ff % FFN_CHUNK == 0
    assert d % (FFN_WEIGHT_SLABS * BF16_SUBLANES) == 0 and d_ff % (FFN_WEIGHT_SLABS * BF16_SUBLANES) == 0
    steps = n // FFN_TILE
    row_spec = pl.BlockSpec((FFN_TILE, d), lambda i: (i, 0))
    in_hbm = pl.BlockSpec(memory_space=pl.ANY)
    cast_in, cast_out, cast_shapes = _slab_specs(cast_for_later, steps)
    return pl.pallas_call(
        functools.partial(_ffn_kernel, d_ff=d_ff, chunk=FFN_CHUNK, n_cast=len(cast_for_later),
                          n_slabs=FFN_WEIGHT_SLABS),
        grid=(steps,),
        in_specs=[row_spec, _resident((1, d)), in_hbm, in_hbm, in_hbm] + cast_in,
        out_specs=[row_spec] + cast_out,
        out_shape=[jax.ShapeDtypeStruct((n, d), F32)] + cast_shapes,
        scratch_shapes=[
            pltpu.VMEM((d, d_ff), BF16),
            pltpu.VMEM((d, d_ff), BF16),
            pltpu.VMEM((d_ff, d), BF16),
            pltpu.VMEM((FFN_STAGE_SLOTS, d // FFN_WEIGHT_SLABS, d_ff), F32),
            pltpu.VMEM((FFN_STAGE_SLOTS, d // FFN_WEIGHT_SLABS, d_ff), F32),
            pltpu.VMEM((FFN_STAGE_SLOTS, d_ff // FFN_WEIGHT_SLABS, d), F32),
            pltpu.SemaphoreType.DMA((3, FFN_STAGE_SLOTS)),
        ],
        compiler_params=pltpu.CompilerParams(dimension_semantics=("arbitrary",),
                                             vmem_limit_bytes=VMEM_LIMIT),
        name=name,
    )(x, norm_g.reshape(1, d), w_gate, w_up, w_down, *cast_for_later)


def _head_mean_sq(x, low_half):
    x2 = x * x
    s_low = jnp.sum(jnp.where(low_half, x2, 0.0), axis=-1, keepdims=True)
    s_all = jnp.sum(x2, axis=-1, keepdims=True)
    return jnp.where(low_half, s_low, s_all - s_low) * (1.0 / HEAD_DIM)


def _split_kv_heads(x):
    swapped = pltpu.roll(x, HEAD_DIM, axis=1)
    low = lax.broadcasted_iota(jnp.int32, x.shape, 1) < HEAD_DIM
    zero = jnp.zeros_like(x)
    parts = (jnp.where(low, x, zero), jnp.where(low, zero, swapped),
             jnp.where(low, swapped, zero), jnp.where(low, zero, x))
    return tuple(p.astype(BF16) for p in parts)


def _mixer_kernel(h_ref, mixg_ref, w_in_ref, gbias_ref, poolw_ref, pscale_ref, wpo_ref,
                  lanes_ref, wao_ref, wout_ref, o_ref,
                  u_scr, xp_scr, q32_scr, q_scr, ka_scr, kb_scr, ve_scr, vo_scr, attn_scr,
                  mixed_scr, gate_scr, merged_scr,
                  *, tiles_per_seq, d_model, pool_width, attn_width, kv_width):
    tm = h_ref.shape[0]
    i = pl.program_id(0)
    tile_in_seq = i % tiles_per_seq
    seq_start = tile_in_seq == 0
    lane128 = lax.broadcasted_iota(jnp.int32, (1, LANES), 1)
    low_half = lane128 < HEAD_DIM

    off_q = pool_width
    off_k = off_q + attn_width
    off_gp = off_k + 2 * kv_width
    off_ga = off_gp + d_model

    @pl.when(seq_start)
    def _():
        xp_scr[:, 0:POOL_HALO, :] = jnp.zeros((len(POOL_WINDOWS), POOL_HALO, POOL_GROUP), F32)
        zk = jnp.zeros((BLOCK, LANES), BF16)
        ones_e = jnp.broadcast_to(jnp.where(low_half, 1.0, 0.0).astype(BF16), (BLOCK + tm, LANES))
        ones_o = jnp.broadcast_to(jnp.where(low_half, 0.0, 1.0).astype(BF16), (BLOCK + tm, LANES))
        for kv in range(N_KV_HEADS):
            ka_scr[kv, 0:BLOCK, :] = zk
            kb_scr[kv, 0:BLOCK, :] = zk
            ve_scr[kv, 0:BLOCK, 0:LANES] = zk
            vo_scr[kv, 0:BLOCK, 0:LANES] = zk
            ve_scr[kv, :, LANES:2 * LANES] = ones_e
            vo_scr[kv, :, LANES:2 * LANES] = ones_o

    @pl.when(jnp.logical_not(seq_start))
    def _():
        xp_scr[:, 0:POOL_HALO, :] = xp_scr[:, tm:tm + POOL_HALO, :]
        for kv in range(N_KV_HEADS):
            ka_scr[kv, 0:BLOCK, :] = ka_scr[kv, tm:tm + BLOCK, :]
            kb_scr[kv, 0:BLOCK, :] = kb_scr[kv, tm:tm + BLOCK, :]
            ve_scr[kv, 0:BLOCK, 0:LANES] = ve_scr[kv, tm:tm + BLOCK, 0:LANES]
            vo_scr[kv, 0:BLOCK, 0:LANES] = vo_scr[kv, tm:tm + BLOCK, 0:LANES]

    h_in = h_ref[...]
    u_unscaled = (h_in * mixg_ref[...]).astype(BF16)
    rstd = lax.rsqrt(jnp.mean(h_in * h_in, axis=-1, keepdims=True) + RMS_EPS)
    u_scr[...] = (h_in * rstd * mixg_ref[...]).astype(BF16)

    xp = _dot(u_unscaled, w_in_ref[:, 0:pool_width]) * rstd
    for gi in range(len(POOL_WINDOWS)):
        xp_scr[gi, POOL_HALO:POOL_HALO + tm, :] = xp[:, gi * POOL_GROUP:(gi + 1) * POOL_GROUP]
    q32 = _dot(u_scr[...], w_in_ref[:, off_q:off_q + attn_width])
    for qc in range(attn_width // LANES):
        q32_scr[qc] = q32[:, qc * LANES:(qc + 1) * LANES]
    kv_proj = _dot_rows_split(u_scr[...], w_in_ref[:, off_k:off_k + 2 * kv_width])

    t = tile_in_seq * tm + lax.broadcasted_iota(jnp.int32, (tm, 1), 0)
    pooled = []
    for gi, w in enumerate(POOL_WINDOWS):
        ext = xp_scr[gi]
        wsum = ext
        shift = 1
        while shift < w:
            wsum = wsum + pltpu.roll(wsum, shift, axis=0)
            shift *= 2
        count = jnp.minimum(t + 1, w).astype(F32)
        pooled.append((wsum[POOL_HALO:] / count - ext[POOL_HALO:]).astype(BF16))
    for c in range(len(POOL_WINDOWS) // 2):
        cols = slice(c * 2 * POOL_GROUP, (c + 1) * 2 * POOL_GROUP)
        pair = jnp.concatenate(pooled[2 * c:2 * c + 2], axis=1)
        zeros = jnp.zeros((POOL_GROUP, POOL_GROUP), F32)
        w_pair = jnp.concatenate(
            [jnp.concatenate([poolw_ref[2 * c], zeros], axis=1),
             jnp.concatenate([zeros, poolw_ref[2 * c + 1]], axis=1)], axis=0).astype(BF16)
        mixed_scr[:, cols] = (_dot(pair, w_pair) * pscale_ref[:, cols]).astype(BF16)

    q_gain = lanes_ref[0:1, :] * (HEAD_DIM ** -0.5 * LOG2E)
    n_gate_chunks = d_model // MIX_COL_CHUNK
    q_chunks_per_gate_chunk = attn_width // LANES // n_gate_chunks
    for c in range(n_gate_chunks):
        cols = slice(c * MIX_COL_CHUNK, (c + 1) * MIX_COL_CHUNK)
        gp = jax.nn.sigmoid(
            _dot(u_scr[...], w_in_ref[:, off_gp + c * MIX_COL_CHUNK: off_gp + (c + 1) * MIX_COL_CHUNK])
            + gbias_ref[:, cols])
        gate_scr[c] = gp
        for qc in range(c * q_chunks_per_gate_chunk, (c + 1) * q_chunks_per_gate_chunk):
            q = q32_scr[qc]
            ms = _head_mean_sq(q, low_half)
            q_scr[qc] = (q * lax.rsqrt(ms + RMS_EPS) * q_gain).astype(BF16)

    k = kv_proj[:, 0:kv_width]
    k = k * lax.rsqrt(_head_mean_sq(k, low_half) + RMS_EPS) * lanes_ref[1:2, :]
    ka0, kb0, ka1, kb1 = _split_kv_heads(k)
    ka_scr[0, BLOCK:BLOCK + tm, :] = ka0
    kb_scr[0, BLOCK:BLOCK + tm, :] = kb0
    ka_scr[1, BLOCK:BLOCK + tm, :] = ka1
    kb_scr[1, BLOCK:BLOCK + tm, :] = kb1
    va0, vb0, va1, vb1 = _split_kv_heads(kv_proj[:, kv_width:2 * kv_width])
    ve_scr[0, BLOCK:BLOCK + tm, 0:LANES] = va0
    vo_scr[0, BLOCK:BLOCK + tm, 0:LANES] = vb0
    ve_scr[1, BLOCK:BLOCK + tm, 0:LANES] = va1
    vo_scr[1, BLOCK:BLOCK + tm, 0:LANES] = vb1

    key = lax.broadcasted_iota(jnp.int32, (2 * BLOCK, BLOCK), 0)
    qry = lax.broadcasted_iota(jnp.int32, (2 * BLOCK, BLOCK), 1)
    band = (key > qry) & (key <= qry + BLOCK)
    first_key = jnp.where(seq_start, BLOCK, 0)
    bias = jnp.where(band, 0.0, MASK_BIAS).astype(BF16)
    bias_first = jnp.where(band & (key >= first_key), 0.0, MASK_BIAS).astype(BF16)
    r4 = lax.broadcasted_iota(jnp.int32, (PAIRS_PER_KV * BLOCK, BLOCK), 0) % BLOCK
    c4 = lax.broadcasted_iota(jnp.int32, (PAIRS_PER_KV * BLOCK, BLOCK), 1)
    onehot = jnp.where(r4 == c4, 1.0, 0.0).astype(BF16)
    sinks2 = lanes_ref[2:2 + N_HEADS // 2, :] * LOG2E

    def score_unit(unit):
        b, kv = divmod(unit, N_KV_HEADS)
        rows = slice(b * BLOCK, (b + 1) * BLOCK)
        keys = slice(b * BLOCK, (b + 2) * BLOCK)
        key_bias = bias_first if b == 0 else bias
        base = kv * PAIRS_PER_KV
        qs = jnp.concatenate([q_scr[base + j, rows, :] for j in range(PAIRS_PER_KV)], axis=0)
        qs = jnp.concatenate([qs, onehot], axis=1)
        k_even = jnp.concatenate([ka_scr[kv, keys, :], key_bias], axis=1)
        k_odd = jnp.concatenate([kb_scr[kv, keys, :], key_bias], axis=1)
        return _dot_nt(qs, k_even), _dot_nt(qs, k_odd)

    def finish_unit(unit, scores):
        b, kv = divmod(unit, N_KV_HEADS)
        rows = slice(b * BLOCK, (b + 1) * BLOCK)
        keys = slice(b * BLOCK, (b + 2) * BLOCK)
        base = kv * PAIRS_PER_KV
        probs = ([], [])
        sink_terms = []
        for j in range(PAIRS_PER_KV):
            m = []
            for par in range(2):
                s = scores[par][j * BLOCK:(j + 1) * BLOCK]
                m.append(jnp.max(s, axis=-1, keepdims=True))
                probs[par].append(jnp.exp2(s - m[par]).astype(BF16))
            sink_terms.append(jnp.exp2(sinks2[base + j:base + j + 1, :]
                                       - jnp.where(low_half, m[0], m[1])))
        p = jnp.concatenate([jnp.concatenate(probs[0], axis=0),
                             jnp.concatenate(probs[1], axis=0)], axis=1)
        v_stack = jnp.concatenate([ve_scr[kv, keys, :], vo_scr[kv, keys, :]], axis=0)
        acc = _dot_rows_split(p, v_stack)
        den = acc[:, LANES:2 * LANES] + jnp.concatenate(sink_terms, axis=0)
        out = (acc[:, 0:LANES] / den).astype(BF16)
        for j in range(PAIRS_PER_KV):
            attn_scr[base + j, rows, :] = out[j * BLOCK:(j + 1) * BLOCK]

    half = tm // 2

    def pool_piece(r0, c):
        rows = slice(r0, r0 + half)
        cols = slice(c * MIX_COL_CHUNK, (c + 1) * MIX_COL_CHUNK)
        gate_scr[c, rows, :] = gate_scr[c, rows, :] * _dot(mixed_scr[rows, :], wpo_ref[:, cols])

    def merge_piece(r0, c):
        rows = slice(r0, r0 + half)
        cols = slice(c * MIX_COL_CHUNK, (c + 1) * MIX_COL_CHUNK)
        attn = jnp.concatenate([attn_scr[j, rows, :] for j in range(attn_width // LANES)], axis=1)
        ba = _dot(attn, wao_ref[:, cols])
        merged_scr[rows, cols] = (gate_scr[c, rows, :]
                                  + gate_scr[n_gate_chunks + c, rows, :] * ba).astype(BF16)

    def out_piece(r0, c):
        rows = slice(r0, r0 + half)
        cols = slice(c * MIX_COL_CHUNK, (c + 1) * MIX_COL_CHUNK)
        o_ref[rows, cols] = h_ref[rows, cols] + _dot(merged_scr[rows, :], wout_ref[:, cols])

    def output_pieces(r0):
        return ([functools.partial(merge_piece, r0, c) for c in range(n_gate_chunks)]
                + [functools.partial(out_piece, r0, c) for c in range(n_gate_chunks)])

    def attn_gate_piece(r0, c):
        rows = slice(r0, r0 + half)
        cols_a = slice(d_model + c * MIX_COL_CHUNK, d_model + (c + 1) * MIX_COL_CHUNK)
        gate_scr[n_gate_chunks + c, rows, :] = jax.nn.sigmoid(
            _dot(u_scr[rows, :], w_in_ref[:, off_ga + c * MIX_COL_CHUNK: off_ga + (c + 1) * MIX_COL_CHUNK])
            + gbias_ref[:, cols_a])

    n_units = (tm // BLOCK) * N_KV_HEADS
    dense = ([(functools.partial(attn_gate_piece, r0, c), functools.partial(pool_piece, r0, c))
              for r0 in (0, half) for c in range(n_gate_chunks)]
             + [(piece,) for piece in output_pieces(0)])
    assert len(dense) == n_units
    scores = score_unit(0)
    for unit in range(n_units):
        next_scores = score_unit(unit + 1) if unit + 1 < n_units else None
        for piece in dense[unit]:
            piece()
        finish_unit(unit, scores)
        scores = next_scores
    for piece in output_pieces(half):
        piece()


def _mixer(h, seq_len, mix_norm, w_in, pool_w, pool_scale, w_pool_out, q_norm, k_norm, sinks,
           w_attn_out, gate_bias, w_out):
    n, d = h.shape
    tm = MIX_TILE
    pool_width = w_pool_out.shape[0]
    attn_width = w_attn_out.shape[0]
    kv_width = N_KV_HEADS * HEAD_DIM
    in_width = w_in.shape[1]
    assert seq_len % tm == 0 and tm % BLOCK == 0
    assert pool_width == len(POOL_WINDOWS) * POOL_GROUP and attn_width == N_HEADS * HEAD_DIM
    assert in_width == pool_width + attn_width + 2 * kv_width + 2 * d and kv_width == LANES
    row_spec = pl.BlockSpec((tm, d), lambda i: (i, 0))
    kern = functools.partial(_mixer_kernel, tiles_per_seq=seq_len // tm, d_model=d,
                             pool_width=pool_width, attn_width=attn_width, kv_width=kv_width)
    lane_rows = jnp.concatenate(
        [jnp.tile(q_norm, LANES // HEAD_DIM)[None], jnp.tile(k_norm, LANES // HEAD_DIM)[None],
         jnp.repeat(sinks.reshape(N_HEADS // 2, 2), HEAD_DIM, axis=1)], axis=0)
    return pl.pallas_call(
        kern,
        grid=(n // tm,),
        in_specs=[row_spec, _resident((1, d)), _resident((d, in_width)), _resident((1, 2 * d)),
                  _resident(pool_w.shape), _resident((1, pool_width)), _resident((pool_width, d)),
                  _resident(lane_rows.shape),
                  _resident((attn_width, d)), _resident((d, d))],
        out_specs=row_spec,
        out_shape=jax.ShapeDtypeStruct((n, d), F32),
        scratch_shapes=[
            pltpu.VMEM((tm, d), BF16),
            pltpu.VMEM((len(POOL_WINDOWS), POOL_HALO + tm, POOL_GROUP), F32),
            pltpu.VMEM((attn_width // LANES, tm, LANES), F32),
            pltpu.VMEM((attn_width // LANES, tm, LANES), BF16),
            pltpu.VMEM((N_KV_HEADS, BLOCK + tm, LANES), BF16),
            pltpu.VMEM((N_KV_HEADS, BLOCK + tm, LANES), BF16),
            pltpu.VMEM((N_KV_HEADS, BLOCK + tm, 2 * LANES), BF16),
            pltpu.VMEM((N_KV_HEADS, BLOCK + tm, 2 * LANES), BF16),
            pltpu.VMEM((attn_width // LANES, tm, LANES), BF16),
            pltpu.VMEM((tm, pool_width), BF16),
            pltpu.VMEM((2 * d // MIX_COL_CHUNK, tm, MIX_COL_CHUNK), F32),
            pltpu.VMEM((tm, d), BF16),
        ],
        compiler_params=pltpu.CompilerParams(dimension_semantics=("arbitrary",),
                                             vmem_limit_bytes=VMEM_LIMIT),
        name="mixer",
    )(h, mix_norm.reshape(1, d), w_in, gate_bias.reshape(1, 2 * d),
      pool_w, pool_scale.reshape(1, pool_width), w_pool_out, lane_rows,
      w_attn_out, w_out)


def kernel(x, ffn1_norm, ffn1_w_gate, ffn1_w_up, ffn1_w_down, mix_norm, w_in, pool_w, pool_scale,
           w_pool_out, q_norm, k_norm, sinks, w_attn_out, gate_bias, w_out, ffn2_norm, ffn2_w_gate,
           ffn2_w_up, ffn2_w_down):
    b, s, d = x.shape
    h = x.reshape(b * s, d)
    h, w_in_b, w_pool_out_b, w_attn_out_b, w_out_b = _ffn(
        h, ffn1_norm, ffn1_w_gate, ffn1_w_up, ffn1_w_down, name="ffn1",
        cast_for_later=(w_in, w_pool_out, w_attn_out, w_out))
    h = _mixer(h, s, mix_norm, w_in_b, pool_w, pool_scale, w_pool_out_b, q_norm, k_norm, sinks,
               w_attn_out_b, gate_bias, w_out_b)
    (h,) = _ffn(h, ffn2_norm, ffn2_w_gate, ffn2_w_up, ffn2_w_down, name="ffn2")
    return h.reshape(b, s, d)
```
